```python
import math
import jax, jax.numpy as jnp
from jax import lax
import numpy as np

D_MODEL = 1024
BATCH = 8
SEQ = 2048
DEPTH = 1
DEC_BATCH = 32
DEC_SEQ = 1
PAST_LEN = 8192
PAGE_SIZE = 128

H_A = 8
DH_A = 64
W_A = H_A * DH_A
MOBA_BLOCK = 256
MOBA_TOPK = 3
Q_BLOCK = 16
N_BUCKETS = 32
MAX_DISTANCE = 128
H_R = 8
DH_R = 64
W_R = H_R * DH_R
LORA_W = 64
LORA_A = 64
LORA_G = 160
RW_COLS = 3 * W_R + LORA_W + LORA_A + LORA_G
IN_COLS = 3 * W_A + RW_COLS
N_GROUPS = 4
EXPERTS_PER_GROUP = 4
N_EXPERTS = N_GROUPS * EXPERTS_PER_GROUP
D_EXPERT = 256
EXPERT_TOPK = 2
PLE_DIM = 256
RMS_EPS = 1e-6
GN_EPS = 64e-5
NEG = -1e30

kernel_name = "hymba_moba_rwkv7_hmoe_decode_step"


def rms_norm(x, g):
    xf = x.astype(jnp.float32)
    y = xf * lax.rsqrt(jnp.mean(xf * xf, axis=-1, keepdims=True) + RMS_EPS)
    return (y * g.astype(jnp.float32)).astype(x.dtype)


def rel_bucket(dist):
    n = jnp.maximum(dist, 0)
    max_exact = N_BUCKETS // 2
    nf = jnp.maximum(n, 1).astype(jnp.float32)
    large = max_exact + (jnp.log(nf / max_exact) / math.log(MAX_DISTANCE / max_exact)
                         * (N_BUCKETS - max_exact)).astype(jnp.int32)
    large = jnp.minimum(large, N_BUCKETS - 1)
    return jnp.where(n < max_exact, n, large)


def moba_attention(q, k_all, v_all, q_pos0, rel_bias):
    B, Sq = q.shape[0], q.shape[1]
    L = k_all.shape[1]
    n_blk = -(-L // MOBA_BLOCK)
    pad = n_blk * MOBA_BLOCK - L
    padw = ((0, 0), (0, pad), (0, 0), (0, 0))
    k_blk = jnp.pad(k_all, padw).reshape(B, n_blk, MOBA_BLOCK, H_A, DH_A).transpose(0, 3, 1, 2, 4)
    v_blk = jnp.pad(v_all, padw).reshape(B, n_blk, MOBA_BLOCK, H_A, DH_A).transpose(0, 3, 1, 2, 4)
    k_mean = jnp.mean(k_blk.astype(jnp.float32), axis=3)
    topk = min(MOBA_TOPK, n_blk)
    chunk = math.gcd(Sq, Q_BLOCK)
    n_chunks = Sq // chunk
    qh = q.transpose(0, 2, 1, 3)
    q_chunks = qh.reshape(B, H_A, n_chunks, chunk, DH_A).transpose(2, 0, 1, 3, 4)
    pos_chunks = (q_pos0 + jnp.arange(Sq, dtype=jnp.int32)).reshape(n_chunks, chunk)
    bias_hb = rel_bias.astype(jnp.float32).T
    b_ix = jnp.arange(B)[:, None, None, None]
    h_ix = jnp.arange(H_A)[None, :, None, None]
    scale = DH_A ** -0.5

    def one_chunk(args):
        qc, pos = args
        own = pos // MOBA_BLOCK
        gate = jnp.einsum('bhcd,bhnd->bhcn', qc.astype(jnp.float32), k_mean)
        fully_past = jnp.arange(n_blk)[None, :] < own[:, None]
        gate = jnp.where(fully_past, gate, NEG)
        _, past_idx = lax.top_k(gate, topk)
        own_b = jnp.broadcast_to(own[None, None, :, None], (B, H_A, chunk, 1)).astype(past_idx.dtype)
        idx = jnp.concatenate([past_idx, own_b], axis=-1)
        slot_ok = jnp.concatenate([jnp.arange(topk)[None, :] < own[:, None],
                                   jnp.ones((chunk, 1), dtype=bool)], axis=-1)
        kg = k_blk[b_ix, h_ix, idx]
        vg = v_blk[b_ix, h_ix, idx]
        s = jnp.einsum('bhcd,bhcnkd->bhcnk', qc, kg).astype(jnp.float32) * scale
        kpos = idx[..., None] * MOBA_BLOCK + jnp.arange(MOBA_BLOCK)
        dist = pos[None, None, :, None, None] - kpos
        s = s + bias_hb[h_ix[..., None], rel_bucket(dist)]
        ok = slot_ok[None, None, :, :, None] & (dist >= 0)
        s = jnp.where(ok, s, NEG)
        pr = jax.nn.softmax(s.reshape(B, H_A, chunk, -1), axis=-1).reshape(s.shape)
        return jnp.einsum('bhcnk,bhcnkd->bhcd', pr.astype(vg.dtype), vg)

    out = lax.map(one_chunk, (q_chunks, pos_chunks))
    return out.transpose(1, 0, 3, 2, 4).reshape(B, Sq, W_A)


def rwkv7_time_mix(z, z_prev, wkv0, shift_mu, w0, w2, a0, a2, g2, k_k, k_a, r_k, lnx_g, lnx_b):
    f32 = jnp.float32
    B, S, _ = z.shape
    z_shift = jnp.concatenate([z_prev[:, None, :].astype(z.dtype), z[:, :-1]], axis=1)
    zs = z + (z_shift - z) * shift_mu
    o1, o2, o3 = W_R, 2 * W_R, 3 * W_R
    o4 = o3 + LORA_W
    o5 = o4 + LORA_A
    r, k, v = zs[..., :o1], zs[..., o1:o2], zs[..., o2:o3]
    xw, xa, xg = zs[..., o3:o4], zs[..., o4:o5], zs[..., o5:]
    w = -jax.nn.softplus(-(w0 + jnp.tanh(xw) @ w2).astype(f32)) - 0.5
    decay = jnp.exp(-jnp.exp(w))
    a = jax.nn.sigmoid((a0 + xa @ a2).astype(f32))
    g = jax.nn.sigmoid(xg) @ g2
    heads = lambda t: t.astype(f32).reshape(B, S, H_R, DH_R)
    kk = heads(k * k_k)
    kk = kk * lax.rsqrt(jnp.maximum(jnp.sum(kk * kk, axis=-1, keepdims=True), 1e-24))
    rh, vh, ah, dh_ = heads(r), heads(v), heads(a), heads(decay)
    kh = heads(k) * (1.0 + (ah - 1.0) * k_a.astype(f32).reshape(H_R, DH_R))

    def step(state, inp):
        r_t, d_t, k_t, v_t, kk_t, a_t = inp
        sa = jnp.einsum('bhij,bhj->bhi', state, -kk_t)
        state = (state * d_t[:, :, None, :] + sa[..., None] * (kk_t * a_t)[:, :, None, :]
                 + v_t[..., None] * k_t[:, :, None, :])
        return state, jnp.einsum('bhij,bhj->bhi', state, r_t)

    tm = lambda t: jnp.swapaxes(t, 0, 1)
    s_fin, y = lax.scan(step, wkv0.astype(f32), (tm(rh), tm(dh_), tm(kh), tm(vh), tm(kk), tm(ah)))
    y = tm(y)
    mu = jnp.mean(y, axis=-1, keepdims=True)
    var = jnp.mean(jnp.square(y - mu), axis=-1, keepdims=True)
    yn = ((y - mu) * lax.rsqrt(var + GN_EPS)).reshape(B, S, W_R) * lnx_g.astype(f32) + lnx_b.astype(f32)
    bonus = (jnp.sum(rh * kh * r_k.astype(f32), axis=-1, keepdims=True) * vh).reshape(B, S, W_R)
    out = ((yn + bonus) * g.astype(f32)).astype(z.dtype)
    return out, s_fin, z[:, -1]


def hier_moe(h, w_rg, b_rg, w_re, b_re, w_eg, w_eu, w_ed):
    B, S, D = h.shape
    t = h.reshape(-1, D)
    gp = jax.nn.softmax((t @ w_rg + b_rg).astype(jnp.float32), axis=-1)
    p_grp, grp = lax.top_k(gp, 1)
    el = (t @ w_re + b_re).astype(jnp.float32).reshape(-1, N_GROUPS, EXPERTS_PER_GROUP)
    el = jnp.take_along_axis(el, grp[:, :, None], axis=1)[:, 0]
    ep = jax.nn.softmax(el, axis=-1)
    p_e, e_loc = lax.top_k(ep, EXPERT_TOPK)
    p_e = p_e / jnp.sum(p_e, axis=-1, keepdims=True)
    e_id = grp * EXPERTS_PER_GROUP + e_loc
    comb = jnp.sum(jax.nn.one_hot(e_id, N_EXPERTS, dtype=jnp.float32) * (p_grp * p_e)[..., None], axis=1)
    hg = jnp.einsum('td,edf->tef', t, w_eg)
    hu = jnp.einsum('td,edf->tef', t, w_eu)
    act = jax.nn.silu(hg) * hu * comb[..., None].astype(t.dtype)
    return jnp.einsum('tef,efd->td', act, w_ed).reshape(B, S, D)


def decoder_layer(x, p_l, past_k, past_v, z_prev, wkv0, q_pos0,
                  norm1_g, w_in, rel_bias, shift_mu, w0, w2, a0, a2, g2, k_k, k_a, r_k,
                  lnx_g, lnx_b, w_out, norm2_g, w_rg, b_rg, w_re, b_re, w_eg, w_eu, w_ed,
                  w_ple, w_pleg):
    B, S, _ = x.shape
    h = rms_norm(x, norm1_g)
    proj = h @ w_in
    q = proj[..., :W_A].reshape(B, S, H_A, DH_A)
    k = proj[..., W_A:2 * W_A].reshape(B, S, H_A, DH_A)
    v = proj[..., 2 * W_A:3 * W_A].reshape(B, S, H_A, DH_A)
    z = proj[..., 3 * W_A:]
    if past_k is None:
        k_all, v_all = k, v
    else:
        k_all = jnp.concatenate([past_k.astype(k.dtype), k], axis=1)
        v_all = jnp.concatenate([past_v.astype(v.dtype), v], axis=1)
    att = moba_attention(q, k_all, v_all, q_pos0, rel_bias)
    rw, wkv_new, shift_new = rwkv7_time_mix(z, z_prev, wkv0, shift_mu, w0, w2, a0, a2, g2,
                                            k_k, k_a, r_k, lnx_g, lnx_b)
    x = x + jnp.concatenate([att, rw.astype(att.dtype)], axis=-1) @ w_out
    x = x + hier_moe(rms_norm(x, norm2_g), w_rg, b_rg, w_re, b_re, w_eg, w_eu, w_ed)
    x = x + (p_l @ w_ple) * jax.nn.sigmoid(x @ w_pleg)
    return x, k, v, wkv_new, shift_new


def setup_inputs(seed: int = 0) -> dict:
    key = jax.random.key(seed)
    ks = iter(jax.random.split(key, 48))
    nrm = lambda shape, s=1.0: jax.random.normal(next(ks), shape, jnp.float32) * s
    n_pages = PAST_LEN // PAGE_SIZE
    n_used = DEC_BATCH * n_pages
    n_pool = n_used + max(1, n_used // 4)
    perm = jax.random.permutation(next(ks), n_pool)
    page_table = perm[:n_used].reshape(DEC_BATCH, n_pages).astype(jnp.int32)
    return {
        "x_prompt": nrm((BATCH, SEQ, D_MODEL)),
        "x_sample": nrm((DEC_BATCH, DEC_SEQ, D_MODEL)),
        "p_prompt": nrm((DEPTH, BATCH, SEQ, PLE_DIM)),
        "p_sample": nrm((DEPTH, DEC_BATCH, DEC_SEQ, PLE_DIM)),
        "cache_k": nrm((DEPTH, n_pool, PAGE_SIZE, H_A, DH_A)),
        "cache_v": nrm((DEPTH, n_pool, PAGE_SIZE, H_A, DH_A)),
        "page_table": page_table,
        "state_wkv": nrm((DEPTH, DEC_BATCH, H_R, DH_R, DH_R), 0.3),
        "state_shift": nrm((DEPTH, DEC_BATCH, RW_COLS)),
        "norm1_g": 1.0 + nrm((DEPTH, D_MODEL), 0.02),
        "w_in": nrm((DEPTH, D_MODEL, IN_COLS), D_MODEL ** -0.5),
        "rel_bias": nrm((N_BUCKETS, H_A), 0.5),
        "shift_mu": jax.random.uniform(next(ks), (DEPTH, RW_COLS), jnp.float32),
        "w0": nrm((DEPTH, W_R), 0.5),
        "w2": nrm((DEPTH, LORA_W, W_R), 0.5 * LORA_W ** -0.5),
        "a0": nrm((DEPTH, W_R), 0.5),
        "a2": nrm((DEPTH, LORA_A, W_R), 0.5 * LORA_A ** -0.5),
        "g2": nrm((DEPTH, LORA_G, W_R), LORA_G ** -0.5),
        "k_k": 0.85 + nrm((DEPTH, W_R), 0.1),
        "k_a": 1.0 + nrm((DEPTH, W_R), 0.1),
        "r_k": nrm((DEPTH, H_R, DH_R), 0.1),
        "lnx_g": 1.0 + nrm((DEPTH, W_R), 0.02),
        "lnx_b": nrm((DEPTH, W_R), 0.02),
        "w_out": nrm((DEPTH, W_A + W_R, D_MODEL), (W_A + W_R) ** -0.5),
        "norm2_g": 1.0 + nrm((DEPTH, D_MODEL), 0.02),
        "w_rg": nrm((DEPTH, D_MODEL, N_GROUPS), D_MODEL ** -0.5),
        "b_rg": nrm((DEPTH, N_GROUPS), 0.01),
        "w_re": nrm((DEPTH, D_MODEL, N_EXPERTS), D_MODEL ** -0.5),
        "b_re": nrm((DEPTH, N_EXPERTS), 0.01),
        "w_eg": nrm((DEPTH, N_EXPERTS, D_MODEL, D_EXPERT), D_MODEL ** -0.5),
        "w_eu": nrm((DEPTH, N_EXPERTS, D_MODEL, D_EXPERT), D_MODEL ** -0.5),
        "w_ed": nrm((DEPTH, N_EXPERTS, D_EXPERT, D_MODEL), D_EXPERT ** -0.5),
        "w_ple": nrm((DEPTH, PLE_DIM, D_MODEL), PLE_DIM ** -0.5),
        "w_pleg": nrm((DEPTH, D_MODEL, D_MODEL), D_MODEL ** -0.5),
        "final_g": 1.0 + nrm((D_MODEL,), 0.02),
    }


def reference(x_prompt, x_sample, p_prompt, p_sample, cache_k, cache_v, page_table, state_wkv,
              state_shift, norm1_g, w_in, rel_bias, shift_mu, w0, w2, a0, a2, g2, k_k, k_a, r_k,
              lnx_g, lnx_b, w_out, norm2_g, w_rg, b_rg, w_re, b_re, w_eg, w_eu, w_ed, w_ple,
              w_pleg, final_g):
    n_prompt = x_prompt.shape[0]
    n_seq = x_sample.shape[0]
    n_pages = page_table.shape[1]
    past_len = n_pages * PAGE_SIZE
    xp, xs = x_prompt, x_sample
    kp_l, vp_l, ks_l, vs_l, wp_l, ws_l, sp_l, ss_l = [], [], [], [], [], [], [], []
    for l in range(DEPTH):
        lw = (norm1_g[l], w_in[l], rel_bias, shift_mu[l], w0[l], w2[l], a0[l], a2[l], g2[l],
              k_k[l], k_a[l], r_k[l], lnx_g[l], lnx_b[l], w_out[l], norm2_g[l], w_rg[l], b_rg[l],
              w_re[l], b_re[l], w_eg[l], w_eu[l], w_ed[l], w_ple[l], w_pleg[l])
        xp, kp, vp, wp, sp = decoder_layer(
            xp, p_prompt[l], None, None,
            jnp.zeros((n_prompt, RW_COLS), xp.dtype),
            jnp.zeros((n_prompt, H_R, DH_R, DH_R), jnp.float32), 0, *lw)
        pk = cache_k[l][page_table].reshape(n_seq, past_len, H_A, DH_A)
        pv = cache_v[l][page_table].reshape(n_seq, past_len, H_A, DH_A)
        xs, ksn, vsn, wsn, ssn = decoder_layer(
            xs, p_sample[l], pk, pv, state_shift[l], state_wkv[l], past_len, *lw)
        kp_l.append(kp); vp_l.append(vp); ks_l.append(ksn); vs_l.append(vsn)
        wp_l.append(wp); ws_l.append(wsn); sp_l.append(sp); ss_l.append(ssn)
    y_prompt = rms_norm(xp, final_g)
    y_sample = rms_norm(xs, final_g)
    return (y_prompt, y_sample, jnp.stack(kp_l), jnp.stack(vp_l), jnp.stack(ks_l), jnp.stack(vs_l),
            jnp.stack(wp_l), jnp.stack(ws_l), jnp.stack(sp_l), jnp.stack(ss_l))
```

```python
import functools
import math

import jax
import jax.numpy as jnp
from jax import lax
from jax.experimental import pallas as pl
from jax.experimental.pallas import tpu as pltpu

F32 = jnp.float32
BF16 = jnp.bfloat16
HIGHEST = lax.Precision.HIGHEST

HEAD_DIM = 64
MOBA_BLOCK = 256
MOBA_TOPK = 3
N_BUCKETS = 32
MAX_DISTANCE = 128
LORA_W = 64
LORA_A = 64
LORA_G = 160
N_GROUPS = 4
EXPERTS_PER_GROUP = 4
RMS_EPS = 1e-6
GN_EPS = 64e-5
NEG = -1e30

LANES = 128
MXU_DIM = 256
VMEM_LIMIT_BYTES = 56 * 1024 * 1024

RWKV_CHUNK = 64
HEADS_PER_TILE = MXU_DIM // HEAD_DIM


def _cparams(n_axes):
    return pltpu.CompilerParams(dimension_semantics=("arbitrary",) * n_axes,
                                vmem_limit_bytes=VMEM_LIMIT_BYTES)


def _mm(a, b, precise):
    if precise:
        return jnp.dot(a.astype(F32), b.astype(F32), precision=HIGHEST, preferred_element_type=F32)
    return jnp.dot(a.astype(BF16), b.astype(BF16), preferred_element_type=F32)


def _mm_nt(a, b):
    return lax.dot_general(a.astype(BF16), b.astype(BF16), (((1,), (1,)), ((), ())),
                           preferred_element_type=F32)


def _mm_tn(a, b):
    return lax.dot_general(a.astype(BF16), b.astype(BF16), (((0,), (0,)), ((), ())),
                           preferred_element_type=F32)


def _mm_split(a_exact, x, terms):
    a16 = a_exact.astype(BF16)
    acc = None
    rem = x
    for _ in range(terms):
        piece = rem.astype(BF16)
        part = jnp.dot(a16, piece, preferred_element_type=F32)
        acc = part if acc is None else acc + part
        rem = rem - piece.astype(F32)
    return acc


def _mm_split_rhs(x, b_exact, terms):
    b16 = b_exact.astype(BF16)
    acc = None
    rem = x
    for _ in range(terms):
        piece = rem.astype(BF16)
        part = jnp.dot(piece, b16, preferred_element_type=F32)
        acc = part if acc is None else acc + part
        rem = rem - piece.astype(F32)
    return acc


def _sigmoid(x):
    return 1.0 / (1.0 + jnp.exp(-x))


def _softplus(x):
    return jnp.maximum(x, 0.0) + jnp.log(1.0 + jnp.exp(-jnp.abs(x)))


def _rms_norm(x, g):
    return x * lax.rsqrt(jnp.mean(x * x, axis=-1, keepdims=True) + RMS_EPS) * g


def _inproj_kernel(x_ref, g_ref, wqkv_ref, wz_ref, qkv_ref, z_ref, *, precise):
    h = _rms_norm(x_ref[...], g_ref[...])
    qkv_ref[...] = _mm(h, wqkv_ref[...], precise)
    z_ref[...] = _mm(h, wz_ref[...], precise)


def _inproj(x2d, g, wqkv, wz, *, tm, precise):
    t, d = x2d.shape
    nq, nz = wqkv.shape[1], wz.shape[1]
    return pl.pallas_call(
        functools.partial(_inproj_kernel, precise=precise),
        grid=(t // tm,),
        in_specs=[pl.BlockSpec((tm, d), lambda i: (i, 0)),
                  pl.BlockSpec((1, d), lambda i: (0, 0)),
                  pl.BlockSpec((d, nq), lambda i: (0, 0)),
                  pl.BlockSpec((d, nz), lambda i: (0, 0))],
        out_specs=[pl.BlockSpec((tm, nq), lambda i: (i, 0)),
                   pl.BlockSpec((tm, nz), lambda i: (i, 0))],
        out_shape=[jax.ShapeDtypeStruct((t, nq), F32), jax.ShapeDtypeStruct((t, nz), F32)],
        compiler_params=_cparams(1),
    )(x2d, g, wqkv, wz)


def _moba_prompt_kernel(q_ref, k_ref, v_ref, tt_ref, o_ref,
                        kbf_s, vt_s, kmean_s, m_s, l_s, acc_s, sel_s):
    blk = MOBA_BLOCK
    i = pl.program_id(1)
    nb = kbf_s.shape[0]
    n_heads = tt_ref.shape[0]
    scale = HEAD_DIM ** -0.5

    @pl.when(i == 0)
    def _():
        for n in range(nb):
            kb = k_ref[0, n * blk:(n + 1) * blk, :]
            kbf_s[n] = kb.astype(BF16)
            kmean_s[n:n + 1, :] = jnp.mean(kb, axis=0, keepdims=True)
            vt_s[n] = v_ref[0, n * blk:(n + 1) * blk, :].T.astype(BF16)

    q_t = q_ref[0].T
    blk_row = lax.broadcasted_iota(jnp.int32, (nb, blk), 0)
    key_idx = lax.broadcasted_iota(jnp.int32, (blk, blk), 0)
    qry_idx = lax.broadcasted_iota(jnp.int32, (blk, blk), 1)
    causal = key_idx <= qry_idx
    upper_half = lax.broadcasted_iota(jnp.int32, (LANES, blk), 0) >= HEAD_DIM

    for pair in range(n_heads // 2):
        lanes = slice(pair * LANES, (pair + 1) * LANES)
        q_pair = q_t[lanes, :]
        outs = []
        for e in range(2):
            h = 2 * pair + e
            q_h = jnp.where(upper_half == (e == 1), q_pair, 0.0)
            gate = jnp.dot(kmean_s[:, lanes], q_h, precision=HIGHEST,
                           preferred_element_type=F32)
            gate = jnp.where(blk_row < i, gate, NEG)
            rank = jnp.zeros((nb, blk), jnp.int32)
            for m in range(nb):
                gm = gate[m:m + 1, :]
                beats = (gm > gate) | ((gm == gate) & (m < blk_row))
                rank = rank + beats.astype(jnp.int32)
            sel_s[...] = jnp.where((rank < MOBA_TOPK) & (blk_row < i), 1.0, 0.0)
            q_bf = (q_h * scale).astype(BF16)
            m_s[...] = jnp.full(m_s.shape, NEG, F32)
            l_s[...] = jnp.zeros(l_s.shape, F32)
            acc_s[...] = jnp.zeros(acc_s.shape, F32)

            def attend(n, bias, mask, h=h, lanes=lanes, q_bf=q_bf):
                s = jnp.dot(kbf_s[n, :, lanes], q_bf, preferred_element_type=F32)
                s = jnp.where(mask, s + bias, NEG)
                m_old = m_s[...]
                m_new = jnp.maximum(m_old, jnp.max(s, axis=0, keepdims=True))
                alpha = jnp.exp(m_old - m_new)
                p = jnp.exp(s - m_new)
                l_s[...] = alpha * l_s[...] + jnp.sum(p, axis=0, keepdims=True)
                pv = jnp.dot(vt_s[n, h * HEAD_DIM:(h + 1) * HEAD_DIM, :], p.astype(BF16),
                             preferred_element_type=F32)
                acc_s[...] = alpha * acc_s[...] + pv
                m_s[...] = m_new

            attend(i, tt_ref[h, :, blk:2 * blk], causal)

            @pl.when(i >= 1)
            def _(h=h, attend=attend):
                n = i - 1
                attend(n, tt_ref[h, :, 0:blk], sel_s[pl.ds(n, 1), :] > 0.5)

            far_bias = tt_ref[h, 0:1, blk - 1:blk]

            def far_body(n, carry, attend=attend, far_bias=far_bias):
                attend(n, far_bias, sel_s[pl.ds(n, 1), :] > 0.5)
                return carry

            lax.fori_loop(0, jnp.maximum(i - 1, 0), far_body, 0)
            outs.append(acc_s[...] / l_s[...])
        o_ref[0, :, lanes] = jnp.concatenate(outs, axis=0).T


def _moba_prompt(qkv3, tt):
    b, s, w3 = qkv3.shape
    w = w3 // 3
    blk = MOBA_BLOCK
    nb = s // blk
    n_heads = w // HEAD_DIM
    return pl.pallas_call(
        _moba_prompt_kernel,
        grid=(b, nb),
        in_specs=[pl.BlockSpec((1, blk, w), lambda bi, i: (bi, i, 0)),
                  pl.BlockSpec((1, s, w), lambda bi, i: (bi, 0, 1)),
                  pl.BlockSpec((1, s, w), lambda bi, i: (bi, 0, 2)),
                  pl.BlockSpec((n_heads, blk, 2 * blk), lambda bi, i: (0, 0, 0))],
        out_specs=pl.BlockSpec((1, blk, w), lambda bi, i: (bi, i, 0)),
        out_shape=jax.ShapeDtypeStruct((b, s, w), F32),
        scratch_shapes=[pltpu.VMEM((nb, blk, w), BF16),
                        pltpu.VMEM((nb, w, blk), BF16),
                        pltpu.VMEM((nb, w), F32),
                        pltpu.VMEM((1, blk), F32),
                        pltpu.VMEM((1, blk), F32),
                        pltpu.VMEM((HEAD_DIM, blk), F32),
                        pltpu.VMEM((nb, blk), F32)],
        compiler_params=_cparams(2),
    )(qkv3, qkv3, qkv3, tt)


def _rel_bias_table(rel_bias, n_dist):
    n = jnp.arange(n_dist, dtype=jnp.int32)
    max_exact = N_BUCKETS // 2
    nf = jnp.maximum(n, 1).astype(F32)
    large = max_exact + (jnp.log(nf / max_exact) / math.log(MAX_DISTANCE / max_exact)
                         * (N_BUCKETS - max_exact)).astype(jnp.int32)
    large = jnp.minimum(large, N_BUCKETS - 1)
    bucket = jnp.where(n < max_exact, n, large)
    return rel_bias.astype(F32).T[:, bucket]


def _bias_tiles(tab):
    blk = MOBA_BLOCK
    k = jnp.arange(blk)[:, None]
    q = jnp.arange(blk)[None, :]
    prev = tab[:, q - k + blk]
    own = tab[:, jnp.maximum(q - k, 0)]
    return jnp.concatenate([prev, own], axis=-1)


def _mm3(a, b):
    a_hi = a.astype(BF16)
    b_hi = b.astype(BF16)
    a_lo = (a - a_hi.astype(F32)).astype(BF16)
    b_lo = (b - b_hi.astype(F32)).astype(BF16)
    return (jnp.dot(a_hi, b_hi, preferred_element_type=F32)
            + jnp.dot(a_hi, b_lo, preferred_element_type=F32)
            + jnp.dot(a_lo, b_hi, preferred_element_type=F32))


def _rwkv_prompt_kernel(zr_ref, zk_ref, zv_ref, zl_ref,
                        mur_ref, muk_ref, muv_ref, mul_ref,
                        w0_ref, w2_ref, a0_ref, a2_ref, g2_ref,
                        kk_ref, ka_ref, rk_ref, lg_ref, lb_ref,
                        rw_ref, st_ref,
                        p_s, last_r, last_k, last_v, last_l):
    t = pl.program_id(2)
    tc = zr_ref.shape[1]
    w = MXU_DIM
    c_len = RWKV_CHUNK
    hd_shift = HEAD_DIM.bit_length() - 1

    @pl.when(t == 0)
    def _():
        p_s[...] = jnp.zeros(p_s.shape, F32)
        last_r[...] = jnp.zeros(last_r.shape, F32)
        last_k[...] = jnp.zeros(last_k.shape, F32)
        last_v[...] = jnp.zeros(last_v.shape, F32)
        last_l[...] = jnp.zeros(last_l.shape, F32)

    def token_shift_mix(z_ref, last_ref, mu_ref):
        z = z_ref[0]
        row = lax.broadcasted_iota(jnp.int32, z.shape, 0)
        prev = jnp.where(row == 0, last_ref[...], pltpu.roll(z, 1, 0))
        last_ref[...] = z[tc - 1:tc, :]
        return z + (prev - z) * mu_ref[...]

    ri = lax.broadcasted_iota(jnp.int32, (w, w), 0)
    ci = lax.broadcasted_iota(jnp.int32, (w, w), 1)
    same_head = (ri >> hd_shift) == (ci >> hd_shift)
    tok_r = ri & (HEAD_DIM - 1)
    tok_c = ci & (HEAD_DIM - 1)
    strict = same_head & (tok_r > tok_c)
    incl = same_head & (tok_r >= tok_c)
    eye = ri == ci
    head_ones = jnp.where(same_head, 1.0, 0.0).astype(BF16)
    tri = jnp.where(lax.broadcasted_iota(jnp.int32, (c_len, c_len), 0)
                    >= lax.broadcasted_iota(jnp.int32, (c_len, c_len), 1), 1.0, 0.0).astype(BF16)

    def head_sum(x):
        return _mm_split_rhs(x, head_ones, 2)

    def stack(x):
        return jnp.where(same_head, jnp.concatenate([x] * HEADS_PER_TILE, axis=0), 0.0)

    def unstack(x):
        out = x[0:c_len]
        for e in range(1, HEADS_PER_TILE):
            out = out + x[e * c_len:(e + 1) * c_len]
        return out

    r = token_shift_mix(zr_ref, last_r, mur_ref)
    k = token_shift_mix(zk_ref, last_k, muk_ref)
    v = token_shift_mix(zv_ref, last_v, muv_ref)
    zl = token_shift_mix(zl_ref, last_l, mul_ref)

    w_pre = w0_ref[...] + _mm(jnp.tanh(zl), w2_ref[...], False)
    logd = -jnp.exp(-_softplus(-w_pre) - 0.5)
    eta = _sigmoid(a0_ref[...] + _mm(zl, a2_ref[...], False))
    gate = _mm(_sigmoid(zl), g2_ref[...], False)
    kk = k * kk_ref[...]
    kk = kk * lax.rsqrt(jnp.maximum(head_sum(kk * kk), 1e-24))
    kh = k * (1.0 + (eta - 1.0) * ka_ref[...])
    bonus = head_sum(r * kh * rk_ref[...]) * v
    kb = kk * eta

    p = p_s[...]
    for c in range(tc // c_len):
        rows = slice(c * c_len, (c + 1) * c_len)
        ld = logd[rows]
        cum = _mm_split(tri, ld, 3)
        cum_end = cum[c_len - 1:c_len, :]
        e_inv = jnp.exp(-cum)
        e_rem = jnp.exp(cum_end - cum)
        a_t = stack(-kk[rows] * jnp.exp(cum - ld))
        b_t = stack(kb[rows] * e_inv)
        k_t = stack(kh[rows] * e_inv)
        r_t = stack(r[rows] * jnp.exp(cum))
        b_h = stack(kb[rows] * e_rem)
        k_h = stack(kh[rows] * e_rem)
        v_s = stack(v[rows])

        prod = _mm_nt(jnp.concatenate([a_t, r_t], axis=0), jnp.concatenate([b_t, k_t], axis=0))
        n_mat = jnp.where(strict, prod[0:w, 0:w], 0.0)
        a_k = jnp.where(strict, prod[0:w, w:2 * w], 0.0)
        m_rb = jnp.where(incl, prod[w:2 * w, 0:w], 0.0)
        m_rk = jnp.where(incl, prod[w:2 * w, w:2 * w], 0.0)

        t_inv = jnp.where(eye, 1.0, 0.0) + n_mat
        n_pow = n_mat
        for _ in range(c_len.bit_length() - 2):
            n_pow = _mm(n_pow, n_pow, False)
            t_inv = t_inv + _mm(t_inv, n_pow, False)

        a_p = _mm(t_inv, a_t, False)
        v_p = _mm(t_inv, _mm(a_k, v_s, False), False)
        r_p = unstack(r_t + _mm(m_rb, a_p, False))
        y_0 = unstack(_mm(m_rb, v_p, False) + _mm(m_rk, v_s, False))
        g_mat = jnp.where(eye, jnp.exp(cum_end), 0.0) + _mm_tn(b_h, a_p)
        h_mat = _mm_tn(b_h, v_p) + _mm_tn(k_h, v_s)

        y = _mm(r_p, p, False) + y_0
        p = _mm3(g_mat, p) + h_mat

        mean = head_sum(y) * (1.0 / HEAD_DIM)
        yc = y - mean
        var = head_sum(yc * yc) * (1.0 / HEAD_DIM)
        yn = yc * lax.rsqrt(var + GN_EPS) * lg_ref[...] + lb_ref[...]
        rw_ref[0, rows, :] = (yn + bonus[rows]) * gate[rows]

    p_s[...] = p
    st_ref[0, 0] = p


def _rwkv_prompt(z3, mu, w0, w2p, a0, a2p, g2p, kk, ka, rk, lg, lb, *, tc):
    b, s, _ = z3.shape
    w = MXU_DIM
    wr = w0.shape[1]
    nt = wr // w
    lw = w2p.shape[0]
    lora_blk = (3 * wr) // lw
    vec = lambda off: pl.BlockSpec((1, w), lambda bi, q, t, off=off: (0, off + q))
    zblk = lambda off: pl.BlockSpec((1, tc, w), lambda bi, q, t, off=off: (bi, t, off + q))
    lora_w = pl.BlockSpec((lw, w), lambda bi, q, t: (0, q))
    return pl.pallas_call(
        _rwkv_prompt_kernel,
        grid=(b, nt, s // tc),
        in_specs=[zblk(0), zblk(nt), zblk(2 * nt),
                  pl.BlockSpec((1, tc, lw), lambda bi, q, t: (bi, t, lora_blk)),
                  vec(0), vec(nt), vec(2 * nt),
                  pl.BlockSpec((1, lw), lambda bi, q, t: (0, lora_blk)),
                  vec(0), lora_w, vec(0), lora_w, lora_w,
                  vec(0), vec(0), vec(0), vec(0), vec(0)],
        out_specs=[pl.BlockSpec((1, tc, w), lambda bi, q, t: (bi, t, q)),
                   pl.BlockSpec((1, 1, w, w), lambda bi, q, t: (bi, q, 0, 0))],
        out_shape=[jax.ShapeDtypeStruct((b, s, wr), F32),
                   jax.ShapeDtypeStruct((b, nt, w, w), F32)],
        scratch_shapes=[pltpu.VMEM((w, w), F32),
                        pltpu.VMEM((1, w), F32), pltpu.VMEM((1, w), F32),
                        pltpu.VMEM((1, w), F32), pltpu.VMEM((1, lw), F32)],
        compiler_params=_cparams(3),
    )(z3, z3, z3, z3, mu, mu, mu, mu, w0, w2p, a0, a2p, g2p, kk, ka, rk, lg, lb)


def _state_from_tiles(st):
    b, nt = st.shape[:2]
    heads = []
    for q in range(nt):
        for e in range(HEADS_PER_TILE):
            sl = slice(e * HEAD_DIM, (e + 1) * HEAD_DIM)
            heads.append(st[:, q, sl, sl])
    return jnp.swapaxes(jnp.stack(heads, axis=1), -1, -2)


def _lora_padded(w2, a2, g2):
    lw = 3 * LANES
    n = w2.shape[1]
    o_a = LORA_W
    o_g = LORA_W + LORA_A
    w2p = jnp.zeros((lw, n), F32).at[0:o_a].set(w2)
    a2p = jnp.zeros((lw, n), F32).at[o_a:o_g].set(a2)
    g2p = jnp.zeros((lw, n), F32).at[o_g:o_g + LORA_G].set(g2)
    return w2p, a2p, g2p


ROUTER_LANE0 = N_GROUPS


def _first_argmax(x, lane):
    mx = jnp.max(x, axis=1, keepdims=True)
    idx = jnp.min(jnp.where(x == mx, lane, LANES), axis=1, keepdims=True)
    return mx, idx


def _outproj_router_kernel(x_ref, att_ref, rw_ref, wo_ref, g_ref, wr_ref, br_ref,
                           x1_ref, h2_ref, comb_ref, *, precise):
    wa = att_ref.shape[1]
    x1 = (x_ref[...] + _mm(att_ref[...], wo_ref[0:wa, :], precise)
          + _mm(rw_ref[...], wo_ref[wa:, :], precise))
    x1_ref[...] = x1
    h2 = _rms_norm(x1, g_ref[...])
    h2_ref[...] = h2.astype(h2_ref.dtype)
    if precise:
        logits = _mm(h2, wr_ref[...], True) + br_ref[...]
    else:
        logits = _mm3(h2, wr_ref[...]) + br_ref[...]
    n_exp = N_GROUPS * EXPERTS_PER_GROUP
    lane = lax.broadcasted_iota(jnp.int32, logits.shape, 1)
    is_grp = lane < N_GROUPS
    gl = jnp.where(is_grp, logits, NEG)
    g_max, grp = _first_argmax(gl, lane)
    p_grp = 1.0 / jnp.sum(jnp.where(is_grp, jnp.exp(gl - g_max), 0.0), axis=1, keepdims=True)
    e_lane = lane - ROUTER_LANE0
    grp_shift = EXPERTS_PER_GROUP.bit_length() - 1
    in_grp = (e_lane >= 0) & (e_lane < n_exp) & ((e_lane >> grp_shift) == grp)
    el = jnp.where(in_grp, logits, NEG)
    m1, i1 = _first_argmax(el, lane)
    el2 = jnp.where(lane == i1, NEG, el)
    m2, i2 = _first_argmax(el2, lane)
    ratio = jnp.exp(m2 - m1)
    p1 = 1.0 / (1.0 + ratio)
    p2 = ratio / (1.0 + ratio)
    comb_ref[...] = p_grp * (jnp.where(lane == i1, p1, 0.0) + jnp.where(lane == i2, p2, 0.0))


def _outproj_router(x2d, att, rw, wo, g, wr, br, *, tm, precise):
    t, d = x2d.shape
    wa, wr_cols = att.shape[1], rw.shape[1]
    h_dtype = F32 if precise else BF16
    tok = lambda n: pl.BlockSpec((tm, n), lambda i: (i, 0))
    full = lambda a: pl.BlockSpec(a.shape, lambda i: (0,) * a.ndim)
    return pl.pallas_call(
        functools.partial(_outproj_router_kernel, precise=precise),
        grid=(t // tm,),
        in_specs=[tok(d), tok(wa), tok(wr_cols), full(wo), full(g), full(wr), full(br)],
        out_specs=[tok(d), tok(d), tok(LANES)],
        out_shape=[jax.ShapeDtypeStruct((t, d), F32), jax.ShapeDtypeStruct((t, d), h_dtype),
                   jax.ShapeDtypeStruct((t, LANES), F32)],
        compiler_params=_cparams(1),
    )(x2d, att, rw, wo, g, wr, br)


def _moe_kernel(h2_ref, comb_ref, x1_ref, weg_ref, weu_ref, wed_ref, x2_ref, *, precise):
    e = pl.program_id(1)

    @pl.when(e == 0)
    def _():
        x2_ref[...] = x1_ref[...]

    h2 = h2_ref[...]
    lane = lax.broadcasted_iota(jnp.int32, comb_ref.shape, 1)
    weight = jnp.sum(jnp.where(lane == e + ROUTER_LANE0, comb_ref[...], 0.0), axis=1, keepdims=True)
    hg = _mm(h2, weg_ref[0], precise)
    hu = _mm(h2, weu_ref[0], precise)
    act = hg * _sigmoid(hg) * hu * weight
    x2_ref[...] += _mm(act, wed_ref[0], precise)


def _moe(h2, comb, x1, weg, weu, wed, *, tm, precise):
    t, d = x1.shape
    n_exp, _, f = weg.shape
    return pl.pallas_call(
        functools.partial(_moe_kernel, precise=precise),
        grid=(t // tm, n_exp),
        in_specs=[pl.BlockSpec((tm, d), lambda i, e: (i, 0)),
                  pl.BlockSpec((tm, LANES), lambda i, e: (i, 0)),
                  pl.BlockSpec((tm, d), lambda i, e: (i, 0)),
                  pl.BlockSpec((1, d, f), lambda i, e: (e, 0, 0)),
                  pl.BlockSpec((1, d, f), lambda i, e: (e, 0, 0)),
                  pl.BlockSpec((1, f, d), lambda i, e: (e, 0, 0))],
        out_specs=pl.BlockSpec((tm, d), lambda i, e: (i, 0)),
        out_shape=jax.ShapeDtypeStruct((t, d), F32),
        compiler_params=_cparams(2),
    )(h2, comb, x1, weg, weu, wed)


def _ple_kernel(x_ref, p_ref, wple_ref, wgate_ref, fg_ref, y_ref, *, precise, final):
    x = x_ref[...]
    emb = _mm(p_ref[...], wple_ref[...], precise)
    x = x + emb * _sigmoid(_mm(x, wgate_ref[...], precise))
    y_ref[...] = _rms_norm(x, fg_ref[...]) if final else x


def _ple(x2d, p2d, wple, wgate, fg, *, tm, precise, final):
    t, d = x2d.shape
    pd = p2d.shape[1]
    full = lambda a: pl.BlockSpec(a.shape, lambda i: (0,) * a.ndim)
    return pl.pallas_call(
        functools.partial(_ple_kernel, precise=precise, final=final),
        grid=(t // tm,),
        in_specs=[pl.BlockSpec((tm, d), lambda i: (i, 0)), pl.BlockSpec((tm, pd), lambda i: (i, 0)),
                  full(wple), full(wgate), full(fg)],
        out_specs=pl.BlockSpec((tm, d), lambda i: (i, 0)),
        out_shape=jax.ShapeDtypeStruct((t, d), F32),
        compiler_params=_cparams(1),
    )(x2d, p2d, wple, wgate, fg)


def _rwkv_sample_prep_kernel(z_ref, zp_ref, mu_ref, w0_ref, w2_ref, a0_ref, a2_ref, g2_ref,
                             kk_ref, ka_ref,
                             r_ref, d_ref, kkn_ref, kb_ref, kh_ref, v_ref, g_ref):
    wr = w0_ref.shape[1]
    hd_shift = HEAD_DIM.bit_length() - 1
    z = z_ref[...]
    zs = z + (zp_ref[...] - z) * mu_ref[...]
    r = zs[:, 0:wr]
    k = zs[:, wr:2 * wr]
    v = zs[:, 2 * wr:3 * wr]
    zl = zs[:, 3 * wr:]
    w_pre = w0_ref[...] + _mm(jnp.tanh(zl), w2_ref[...], True)
    decay = jnp.exp(-jnp.exp(-_softplus(-w_pre) - 0.5))
    eta = _sigmoid(a0_ref[...] + _mm(zl, a2_ref[...], True))
    ri = lax.broadcasted_iota(jnp.int32, (wr, wr), 0)
    ci = lax.broadcasted_iota(jnp.int32, (wr, wr), 1)
    head_ones = jnp.where((ri >> hd_shift) == (ci >> hd_shift), 1.0, 0.0)
    kk = k * kk_ref[...]
    kk = kk * lax.rsqrt(jnp.maximum(_mm(kk * kk, head_ones, True), 1e-24))
    r_ref[...] = r
    d_ref[...] = decay
    kkn_ref[...] = kk
    kb_ref[...] = kk * eta
    kh_ref[...] = k * (1.0 + (eta - 1.0) * ka_ref[...])
    v_ref[...] = v
    g_ref[...] = _mm(_sigmoid(zl), g2_ref[...], True)


def _rwkv_sample_prep(z, zp, mu, w0, w2p, a0, a2p, g2p, kk, ka):
    n = z.shape[0]
    wr = w0.shape[1]
    args = (z, zp, mu, w0, w2p, a0, a2p, g2p, kk, ka)
    full = lambda a: pl.BlockSpec(a.shape, lambda i: (0,) * a.ndim)
    return pl.pallas_call(
        _rwkv_sample_prep_kernel,
        grid=(1,),
        in_specs=[full(a) for a in args],
        out_specs=[pl.BlockSpec((n, wr), lambda i: (0, 0))] * 7,
        out_shape=[jax.ShapeDtypeStruct((n, wr), F32)] * 7,
        compiler_params=_cparams(1),
    )(*args)


def _rwkv_sample_step_kernel(s_ref, r_ref, d_ref, kkn_ref, kb_ref, kh_ref, v_ref, g_ref,
                             rk_ref, lg_ref, lb_ref, so_ref, rw_ref):
    n_heads = s_ref.shape[1]
    eye = (lax.broadcasted_iota(jnp.int32, (HEAD_DIM, HEAD_DIM), 0)
           == lax.broadcasted_iota(jnp.int32, (HEAD_DIM, HEAD_DIM), 1))
    for h in range(n_heads):
        hrow = lambda ref, lead=True: ref[0, h:h + 1, :] if lead else ref[h:h + 1, :]
        st = s_ref[0, h]
        r, v, kh = hrow(r_ref), hrow(v_ref), hrow(kh_ref)
        sa = jnp.sum(st * (-hrow(kkn_ref)), axis=1, keepdims=True)
        v_col = jnp.sum(jnp.where(eye, v, 0.0), axis=1, keepdims=True)
        st = st * hrow(d_ref) + sa * hrow(kb_ref) + v_col * kh
        so_ref[0, h] = st
        y_col = jnp.sum(st * r, axis=1, keepdims=True)
        y = jnp.sum(jnp.where(eye, y_col, 0.0), axis=0, keepdims=True)
        mean = jnp.mean(y, axis=1, keepdims=True)
        yc = y - mean
        var = jnp.mean(yc * yc, axis=1, keepdims=True)
        yn = yc * lax.rsqrt(var + GN_EPS) * hrow(lg_ref, False) + hrow(lb_ref, False)
        bonus = jnp.sum(r * kh * hrow(rk_ref, False), axis=1, keepdims=True) * v
        rw_ref[0, h:h + 1, :] = (yn + bonus) * hrow(g_ref)


def _rwkv_sample_step(state, per_head, rk, lg, lb):
    n, nh = state.shape[:2]
    tok = pl.BlockSpec((1, nh, HEAD_DIM), lambda i: (i, 0, 0))
    par = pl.BlockSpec((nh, HEAD_DIM), lambda i: (0, 0))
    st = pl.BlockSpec((1, nh, HEAD_DIM, HEAD_DIM), lambda i: (i, 0, 0, 0))
    return pl.pallas_call(
        _rwkv_sample_step_kernel,
        grid=(n,),
        in_specs=[st] + [tok] * 7 + [par] * 3,
        out_specs=[st, tok],
        out_shape=[jax.ShapeDtypeStruct(state.shape, F32),
                   jax.ShapeDtypeStruct((n, nh, HEAD_DIM), F32)],
        compiler_params=_cparams(1),
    )(state, *per_head, rk, lg, lb)


PAGES_PER_STEP = 8


def _head_mask(n_heads, width):
    hd_shift = HEAD_DIM.bit_length() - 1
    return (lax.broadcasted_iota(jnp.int32, (n_heads, width), 1) >> hd_shift
            == lax.broadcasted_iota(jnp.int32, (n_heads, width), 0))


def _moba_sample_score_kernel(pt_ref, q_ref, *refs):
    k_refs, s_ref = refs[:-1], refs[-1]
    n_heads = s_ref.shape[1]
    page = k_refs[0].shape[1]
    qb = jnp.where(_head_mask(n_heads, q_ref.shape[2]), q_ref[0], 0.0)
    for i, k_ref in enumerate(k_refs):
        s_ref[0, :, i * page:(i + 1) * page] = lax.dot_general(
            qb, k_ref[0], (((1,), (1,)), ((), ())), precision=HIGHEST, preferred_element_type=F32)


def _moba_sample_scores(page_table, q3, cache_k3, n_heads):
    n, n_pages = page_table.shape
    page, width = cache_k3.shape[1:]
    pps = PAGES_PER_STEP
    k_spec = lambda i: pl.BlockSpec((1, page, width), lambda b, j, pt, i=i: (pt[b, j * pps + i], 0, 0))
    grid_spec = pltpu.PrefetchScalarGridSpec(
        num_scalar_prefetch=1,
        grid=(n, n_pages // pps),
        in_specs=[pl.BlockSpec((1, 1, width), lambda b, j, pt: (b, 0, 0))] + [k_spec(i) for i in range(pps)],
        out_specs=pl.BlockSpec((1, n_heads, pps * page), lambda b, j, pt: (b, 0, j)))
    return pl.pallas_call(
        _moba_sample_score_kernel,
        grid_spec=grid_spec,
        out_shape=jax.ShapeDtypeStruct((n, n_heads, n_pages * page), F32),
        compiler_params=_cparams(2),
    )(page_table, q3, *([cache_k3] * pps))


def _moba_sample_select_kernel(s_ref, bias_ref, q_ref, kn_ref, b0_ref, p_ref, pown_ref, idx_ref):
    n_heads, n_keys = s_ref.shape[1:]
    blk_shift = MOBA_BLOCK.bit_length() - 1
    n_blk = n_keys // MOBA_BLOCK
    scale = HEAD_DIM ** -0.5
    s = s_ref[0]
    member = jnp.where(lax.broadcasted_iota(jnp.int32, (n_keys, LANES), 0) >> blk_shift
                       == lax.broadcasted_iota(jnp.int32, (n_keys, LANES), 1), 1.0, 0.0).astype(BF16)
    lane = lax.broadcasted_iota(jnp.int32, (n_heads, LANES), 1)
    gate = _mm_split_rhs(s, member, 3) * (1.0 / MOBA_BLOCK)
    gate = jnp.where(lane < n_blk, gate, NEG)
    sel = jnp.zeros((n_heads, LANES), F32)
    idx = jnp.zeros((n_heads, LANES), jnp.int32)
    for r in range(MOBA_TOPK):
        _, pick = _first_argmax(gate, lane)
        hit = lane == pick
        sel = jnp.where(hit, 1.0, sel)
        gate = jnp.where(hit, NEG, gate)
        idx = jnp.where(lane == r, pick, idx)
    idx_ref[0] = idx
    key_sel = _mm_nt(sel, member)
    sc = jnp.where(key_sel > 0.5, s * scale + bias_ref[...], NEG)
    qb = jnp.where(_head_mask(n_heads, q_ref.shape[2]), q_ref[0], 0.0)
    s_own = jnp.sum(qb * kn_ref[0], axis=1, keepdims=True) * scale + b0_ref[...]
    m = jnp.maximum(jnp.max(sc, axis=1, keepdims=True), s_own)
    p = jnp.exp(sc - m)
    p_own = jnp.exp(s_own - m)
    inv = 1.0 / (jnp.sum(p, axis=1, keepdims=True) + p_own)
    p_ref[0] = p * inv
    pown_ref[0] = jnp.broadcast_to(p_own * inv, (n_heads, LANES))


def _moba_sample_select(scores, bias_keys, q3, kn3, bias0):
    n, n_heads, n_keys = scores.shape
    width = q3.shape[2]
    tokw = pl.BlockSpec((1, 1, width), lambda b: (b, 0, 0))
    lane_out = pl.BlockSpec((1, n_heads, LANES), lambda b: (b, 0, 0))
    return pl.pallas_call(
        _moba_sample_select_kernel,
        grid=(n,),
        in_specs=[pl.BlockSpec((1, n_heads, n_keys), lambda b: (b, 0, 0)),
                  pl.BlockSpec((n_heads, n_keys), lambda b: (0, 0)),
                  tokw, tokw, pl.BlockSpec((n_heads, 1), lambda b: (0, 0))],
        out_specs=[pl.BlockSpec((1, n_heads, n_keys), lambda b: (b, 0, 0)), lane_out, lane_out],
        out_shape=[jax.ShapeDtypeStruct((n, n_heads, n_keys), F32),
                   jax.ShapeDtypeStruct((n, n_heads, LANES), F32),
                   jax.ShapeDtypeStruct((n, n_heads, LANES), jnp.int32)],
        compiler_params=_cparams(1),
    )(scores, bias_keys, q3, kn3, bias0)


def _moba_sample_pv_kernel(vp_ref, lp_ref, pown_ref, vn_ref, *refs):
    n_sel = (len(refs) - 1) // 4
    v_refs, p_refs, o_ref = refs[:2 * n_sel], refs[2 * n_sel:4 * n_sel], refs[-1]
    hp = pl.program_id(1)
    acc = []
    for e in range(2):
        h = 2 * hp + e
        a = pown_ref[0, pl.ds(h, 1), :] * vn_ref[0]
        for i in range(n_sel):
            j = e * n_sel + i
            a = a + jnp.dot(p_refs[j][0, pl.ds(h, 1), :], v_refs[j][0], precision=HIGHEST,
                            preferred_element_type=F32)
        acc.append(a)
    lane = lax.broadcasted_iota(jnp.int32, (1, LANES), 1)
    o_ref[0] = jnp.where(lane < HEAD_DIM, acc[0], acc[1])


def _moba_sample_pv(pool_pages, logical_pages, probs, p_own, vn3, cache_v3):
    n, n_heads, _ = probs.shape
    page = cache_v3.shape[1]
    n_sel = pool_pages.shape[1] // n_heads
    slot = lambda hp, j: (2 * hp + j // n_sel) * n_sel + j % n_sel
    v_spec = lambda j: pl.BlockSpec((1, page, LANES),
                                    lambda b, hp, vp, lp, j=j: (vp[b, slot(hp, j)], 0, hp))
    p_spec = lambda j: pl.BlockSpec((1, n_heads, page),
                                    lambda b, hp, vp, lp, j=j: (b, 0, lp[b, slot(hp, j)]))
    grid_spec = pltpu.PrefetchScalarGridSpec(
        num_scalar_prefetch=2,
        grid=(n, n_heads // 2),
        in_specs=[pl.BlockSpec((1, n_heads, LANES), lambda b, hp, vp, lp: (b, 0, 0)),
                  pl.BlockSpec((1, 1, LANES), lambda b, hp, vp, lp: (b, 0, hp))]
                 + [v_spec(j) for j in range(2 * n_sel)] + [p_spec(j) for j in range(2 * n_sel)],
        out_specs=pl.BlockSpec((1, 1, LANES), lambda b, hp, vp, lp: (b, 0, hp)))
    return pl.pallas_call(
        _moba_sample_pv_kernel,
        grid_spec=grid_spec,
        out_shape=jax.ShapeDtypeStruct((n, 1, n_heads * HEAD_DIM), F32),
        compiler_params=_cparams(2),
    )(pool_pages, logical_pages, p_own, vn3, *([cache_v3] * (2 * n_sel)), *([probs] * (2 * n_sel)))


def _moba_sample(q, k_new, v_new, cache_k_l, cache_v_l, page_table, tab):
    n, width = q.shape
    n_heads = width // HEAD_DIM
    n_pool, page = cache_k_l.shape[:2]
    n_pages = page_table.shape[1]
    past_len = n_pages * page
    pages_per_blk = MOBA_BLOCK // page
    assert past_len % MOBA_BLOCK == 0 and past_len // MOBA_BLOCK >= MOBA_TOPK
    assert n_pages % PAGES_PER_STEP == 0
    ck = cache_k_l.reshape(n_pool, page, width)
    cv = cache_v_l.reshape(n_pool, page, width)
    q3, kn3, vn3 = q[:, None, :], k_new[:, None, :], v_new[:, None, :]
    scores = _moba_sample_scores(page_table, q3, ck, n_heads)
    dist = jnp.minimum(past_len - jnp.arange(past_len), tab.shape[1] - 1)
    probs, p_own, idx = _moba_sample_select(scores, tab[:, dist], q3, kn3, tab[:, 0:1])
    blocks = idx[:, :, :MOBA_TOPK]
    logical = (blocks[..., None] * pages_per_blk + jnp.arange(pages_per_blk)).reshape(n, -1)
    pool = jnp.take_along_axis(page_table, logical, axis=1)
    return _moba_sample_pv(pool, logical, probs, p_own, vn3, cv).reshape(n, width)


def _layer_weights(l, precise, norm1_g, w_in, shift_mu, w0, w2, a0, a2, g2, k_k, k_a, r_k,
                   lnx_g, lnx_b, w_out, norm2_g, w_rg, b_rg, w_re, b_re, w_eg, w_eu, w_ed,
                   w_ple, w_pleg):
    wd = F32 if precise else BF16
    row = lambda x: x.reshape(1, -1).astype(F32)
    wa3 = 3 * (w_out.shape[1] - w0.shape[1])
    rw_cols = w_in.shape[2] - wa3
    z_pad = -rw_cols % LANES
    w2p, a2p, g2p = _lora_padded(w2[l], a2[l], g2[l])
    n_rt = N_GROUPS + N_GROUPS * EXPERTS_PER_GROUP
    w_router = jnp.pad(jnp.concatenate([w_rg[l], w_re[l]], axis=1), ((0, 0), (0, LANES - n_rt)))
    b_router = jnp.pad(jnp.concatenate([b_rg[l], b_re[l]]), (0, LANES - n_rt))
    return dict(
        norm1=row(norm1_g[l]),
        wqkv=w_in[l][:, :wa3].astype(wd),
        wz=jnp.pad(w_in[l][:, wa3:], ((0, 0), (0, z_pad))).astype(wd),
        mu=jnp.pad(shift_mu[l], (0, z_pad))[None],
        w0=row(w0[l]), w2p=w2p, a0=row(a0[l]), a2p=a2p, g2p=g2p,
        kk=row(k_k[l]), ka=row(k_a[l]), rk=row(r_k[l]), lg=row(lnx_g[l]), lb=row(lnx_b[l]),
        wo=w_out[l].astype(wd), norm2=row(norm2_g[l]), w_router=w_router, b_router=b_router[None],
        weg=w_eg[l].astype(wd), weu=w_eu[l].astype(wd), wed=w_ed[l].astype(wd),
        wple=w_ple[l].astype(wd), wgate=w_pleg[l].astype(wd),
        rw_cols=rw_cols, wa=wa3 // 3)


def _post_attention(x2d, att, rw, p2d, w, fg, *, tm, precise, final):
    x1, h2, comb = _outproj_router(x2d, att, rw, w["wo"], w["norm2"], w["w_router"], w["b_router"],
                                   tm=tm, precise=precise)
    x2 = _moe(h2, comb, x1, w["weg"], w["weu"], w["wed"], tm=tm, precise=precise)
    return _ple(x2, p2d, w["wple"], w["wgate"], fg, tm=tm, precise=precise, final=final)


def kernel(x_prompt, x_sample, p_prompt, p_sample, cache_k, cache_v, page_table, state_wkv, state_shift, norm1_g, w_in, rel_bias, shift_mu, w0, w2, a0, a2, g2, k_k, k_a, r_k, lnx_g, lnx_b, w_out, norm2_g, w_rg, b_rg, w_re, b_re, w_eg, w_eu, w_ed, w_ple, w_pleg, final_g):
    depth = norm1_g.shape[0]
    b, s, d = x_prompt.shape
    n = x_sample.shape[0]
    assert x_sample.shape[1] == 1
    n_heads = state_wkv.shape[2]
    layer_params = (norm1_g, w_in, shift_mu, w0, w2, a0, a2, g2, k_k, k_a, r_k, lnx_g, lnx_b,
                    w_out, norm2_g, w_rg, b_rg, w_re, b_re, w_eg, w_eu, w_ed, w_ple, w_pleg)
    fg = final_g.reshape(1, -1)
    tab = _rel_bias_table(rel_bias, 2 * MOBA_BLOCK)
    bias_tiles = _bias_tiles(tab)
    xp = x_prompt.reshape(b * s, d)
    xs = x_sample.reshape(n, d)
    outs = [[] for _ in range(8)]
    for l in range(depth):
        final = l == depth - 1
        w = _layer_weights(l, False, *layer_params)
        wa, rw_cols = w["wa"], w["rw_cols"]
        heads = lambda a, lead: a.reshape(lead + (wa // HEAD_DIM, HEAD_DIM))
        qkv, z = _inproj(xp, w["norm1"], w["wqkv"], w["wz"], tm=512, precise=False)
        att = _moba_prompt(qkv.reshape(b, s, 3 * wa), bias_tiles).reshape(b * s, wa)
        z3 = z.reshape(b, s, -1)
        rw, st = _rwkv_prompt(z3, w["mu"], w["w0"], w["w2p"], w["a0"], w["a2p"], w["g2p"], w["kk"],
                              w["ka"], w["rk"], w["lg"], w["lb"], tc=128)
        xp = _post_attention(xp, att, rw.reshape(b * s, -1), p_prompt[l].reshape(b * s, -1), w, fg,
                             tm=512, precise=False, final=final)
        outs[0].append(heads(qkv[:, wa:2 * wa], (b, s)))
        outs[1].append(heads(qkv[:, 2 * wa:], (b, s)))
        outs[4].append(_state_from_tiles(st))
        outs[6].append(z3[:, -1, :rw_cols])
        w = _layer_weights(l, True, *layer_params)
        qkv, z = _inproj(xs, w["norm1"], w["wqkv"], w["wz"], tm=n, precise=True)
        q_s, k_s, v_s = qkv[:, :wa], qkv[:, wa:2 * wa], qkv[:, 2 * wa:]
        att = _moba_sample(q_s, k_s, v_s, cache_k[l], cache_v[l], page_table, tab)
        z_prev = jnp.pad(state_shift[l], ((0, 0), (0, z.shape[1] - rw_cols)))
        per_tok = _rwkv_sample_prep(z, z_prev, w["mu"], w["w0"], w["w2p"], w["a0"], w["a2p"], w["g2p"],
                                    w["kk"], w["ka"])
        per_head = [a.reshape(n, n_heads, HEAD_DIM) for a in per_tok]
        hp = lambda a: a.reshape(n_heads, HEAD_DIM)
        st_s, rw_s = _rwkv_sample_step(state_wkv[l], per_head, hp(w["rk"]), hp(w["lg"]), hp(w["lb"]))
        xs = _post_attention(xs, att, rw_s.reshape(n, -1), p_sample[l].reshape(n, -1), w, fg,
                             tm=n, precise=True, final=final)
        outs[2].append(heads(k_s, (n, 1)))
        outs[3].append(heads(v_s, (n, 1)))
        outs[5].append(st_s)
        outs[7].append(z[:, :rw_cols])
    kp, vp, ks, vs, wp, ws, sp, ss = [jnp.stack(o) for o in outs]
    return (xp.reshape(b, s, d), xs.reshape(n, 1, d), kp, vp, ks, vs, wp, ws, sp, ss)
```

```python
import functools
import math

import jax
import jax.numpy as jnp
from jax import lax
from jax.experimental import pallas as pl
from jax.experimental.pallas import tpu as pltpu

F32 = jnp.float32
BF16 = jnp.bfloat16
HIGHEST = lax.Precision.HIGHEST

HEAD_DIM = 64
MOBA_BLOCK = 256
MOBA_TOPK = 3
N_BUCKETS = 32
MAX_DISTANCE = 128
LORA_W = 64
LORA_A = 64
LORA_G = 160
N_GROUPS = 4
EXPERTS_PER_GROUP = 4
RMS_EPS = 1e-6
GN_EPS = 64e-5
NEG = -1e30
LOG2_E = math.log2(math.e)

LANES = 128
MXU_DIM = 256
VMEM_LIMIT_BYTES = 56 * 1024 * 1024

MOE_TOKEN_TILE = 1024
MOBA_HEAD_GROUP = 8
RWKV_CHUNK = 64
RWKV_TOKEN_TILE = 512
HEADS_PER_TILE = MXU_DIM // HEAD_DIM


def _cparams(n_axes):
    return pltpu.CompilerParams(dimension_semantics=("arbitrary",) * n_axes,
                                vmem_limit_bytes=VMEM_LIMIT_BYTES)


def _mm(a, b, precise):
    if precise:
        return jnp.dot(a.astype(F32), b.astype(F32), precision=HIGHEST, preferred_element_type=F32)
    return jnp.dot(a.astype(BF16), b.astype(BF16), preferred_element_type=F32)


def _mm_nt(a, b):
    return lax.dot_general(a.astype(BF16), b.astype(BF16), (((1,), (1,)), ((), ())),
                           preferred_element_type=F32)


def _mm_tn(a, b):
    return lax.dot_general(a.astype(BF16), b.astype(BF16), (((0,), (0,)), ((), ())),
                           preferred_element_type=F32)


def _mm_split(a_exact, x, terms):
    a16 = a_exact.astype(BF16)
    acc = None
    rem = x
    for _ in range(terms):
        piece = rem.astype(BF16)
        part = jnp.dot(a16, piece, preferred_element_type=F32)
        acc = part if acc is None else acc + part
        rem = rem - piece.astype(F32)
    return acc


def _mm_split_rhs(x, b_exact, terms):
    b16 = b_exact.astype(BF16)
    acc = None
    rem = x
    for _ in range(terms):
        piece = rem.astype(BF16)
        part = jnp.dot(piece, b16, preferred_element_type=F32)
        acc = part if acc is None else acc + part
        rem = rem - piece.astype(F32)
    return acc


def _sigmoid(x):
    return 1.0 / (1.0 + jnp.exp(-x))


def _softplus(x):
    return jnp.maximum(x, 0.0) + jnp.log(1.0 + jnp.exp(-jnp.abs(x)))


def _rms_norm(x, g):
    return x * lax.rsqrt(jnp.mean(x * x, axis=-1, keepdims=True) + RMS_EPS) * g


def _inproj_kernel(x_ref, g_ref, wqkv_ref, wz_ref, qkv_ref, z_ref, *, precise):
    h = _rms_norm(x_ref[...], g_ref[...])
    qkv_ref[...] = _mm(h, wqkv_ref[...], precise)
    z_ref[...] = _mm(h, wz_ref[...], precise)


def _inproj(x2d, g, wqkv, wz, *, tm, precise):
    t, d = x2d.shape
    nq, nz = wqkv.shape[1], wz.shape[1]
    return pl.pallas_call(
        functools.partial(_inproj_kernel, precise=precise),
        grid=(t // tm,),
        in_specs=[pl.BlockSpec((tm, d), lambda i: (i, 0)),
                  pl.BlockSpec((1, d), lambda i: (0, 0)),
                  pl.BlockSpec((d, nq), lambda i: (0, 0)),
                  pl.BlockSpec((d, nz), lambda i: (0, 0))],
        out_specs=[pl.BlockSpec((tm, nq), lambda i: (i, 0)),
                   pl.BlockSpec((tm, nz), lambda i: (i, 0))],
        out_shape=[jax.ShapeDtypeStruct((t, nq), F32), jax.ShapeDtypeStruct((t, nz), F32)],
        compiler_params=_cparams(1),
    )(x2d, g, wqkv, wz)


def _moba_prompt_kernel(q_ref, k_ref, v_ref, tt_ref, o_ref,
                        kbf_s, vt_s, kmean_s, m_s, l_s, acc_s, sel_s, qbf_s):
    blk = MOBA_BLOCK
    i = pl.program_id(1)
    nb = kbf_s.shape[0]
    n_heads = tt_ref.shape[0]
    scale = HEAD_DIM ** -0.5 * LOG2_E

    @pl.when(i == 0)
    def _():
        for n in range(nb):
            kb = k_ref[0, n * blk:(n + 1) * blk, :]
            kbf_s[n] = kb.astype(BF16)
            kmean_s[n:n + 1, :] = jnp.mean(kb, axis=0, keepdims=True)
            vt_s[n] = v_ref[0, n * blk:(n + 1) * blk, :].T.astype(BF16)

    q_t = q_ref[0].T
    blk_row = lax.broadcasted_iota(jnp.int32, (nb, blk), 0)
    key_idx = lax.broadcasted_iota(jnp.int32, (blk, blk), 0)
    qry_idx = lax.broadcasted_iota(jnp.int32, (blk, blk), 1)
    causal = key_idx <= qry_idx
    upper_half = lax.broadcasted_iota(jnp.int32, (LANES, blk), 0) >= HEAD_DIM

    for h in range(n_heads):
        lanes = slice((h // 2) * LANES, (h // 2 + 1) * LANES)
        q_h = jnp.where(upper_half == (h % 2 == 1), q_t[lanes, :], 0.0)
        gate = jnp.dot(kmean_s[:, lanes], q_h, precision=HIGHEST, preferred_element_type=F32)
        gate = jnp.where(blk_row < i, gate, NEG)
        rank = jnp.zeros((nb, blk), jnp.int32)
        for m in range(nb):
            gm = gate[m:m + 1, :]
            beats = (gm > gate) | ((gm == gate) & (m < blk_row))
            rank = rank + beats.astype(jnp.int32)
        sel_s[h] = jnp.where((rank < MOBA_TOPK) & (blk_row < i), 1.0, 0.0)
        qbf_s[h] = (q_h * scale).astype(BF16)
    m_s[...] = jnp.full(m_s.shape, NEG, F32)
    l_s[...] = jnp.zeros(l_s.shape, F32)
    acc_s[...] = jnp.zeros(acc_s.shape, F32)

    def attend(group, n, bias_of, mask_of, uniform_bias=False):
        scores = [jnp.dot(kbf_s[n, :, (h // 2) * LANES:(h // 2 + 1) * LANES], qbf_s[h],
                          preferred_element_type=F32) for h in group]
        probs, alphas = [], []
        for h, s in zip(group, scores):
            m_old = m_s[h:h + 1, :]
            if uniform_bias:
                bias = bias_of(h)
                s = jnp.where(mask_of(h), s, NEG)
                m_new = jnp.maximum(m_old, jnp.max(s, axis=0, keepdims=True) + bias)
                p = jnp.exp2(s - (m_new - bias))
            else:
                s = jnp.where(mask_of(h), s + bias_of(h), NEG)
                m_new = jnp.maximum(m_old, jnp.max(s, axis=0, keepdims=True))
                p = jnp.exp2(s - m_new)
            alpha = jnp.exp2(m_old - m_new)
            l_s[h:h + 1, :] = alpha * l_s[h:h + 1, :] + jnp.sum(p, axis=0, keepdims=True)
            m_s[h:h + 1, :] = m_new
            probs.append(p.astype(BF16))
            alphas.append(alpha)
        pvs = [jnp.dot(vt_s[n, h * HEAD_DIM:(h + 1) * HEAD_DIM, :], p, preferred_element_type=F32)
               for h, p in zip(group, probs)]
        for h, alpha, pv in zip(group, alphas, pvs):
            acc_s[h] = alpha * acc_s[h] + pv

    for g0 in range(0, n_heads, MOBA_HEAD_GROUP):
        group = range(g0, g0 + MOBA_HEAD_GROUP)
        attend(group, i, lambda h: tt_ref[h, :, blk:2 * blk], lambda h: causal)

        @pl.when(i >= 1)
        def _(group=group):
            n = i - 1
            attend(group, n, lambda h: tt_ref[h, :, 0:blk], lambda h: sel_s[h, pl.ds(n, 1), :] > 0.5)

        def far_body(n, carry, group=group):
            attend(group, n, lambda h: tt_ref[h, 0:1, blk - 1:blk],
                   lambda h: sel_s[h, pl.ds(n, 1), :] > 0.5, uniform_bias=True)
            return carry

        lax.fori_loop(0, jnp.maximum(i - 1, 0), far_body, 0)

    for pair in range(n_heads // 2):
        outs = [acc_s[h] / l_s[h:h + 1, :] for h in (2 * pair, 2 * pair + 1)]
        o_ref[0, :, pair * LANES:(pair + 1) * LANES] = jnp.concatenate(outs, axis=0).T


def _moba_prompt(qkv3, tt):
    b, s, w3 = qkv3.shape
    w = w3 // 3
    blk = MOBA_BLOCK
    nb = s // blk
    n_heads = w // HEAD_DIM
    return pl.pallas_call(
        _moba_prompt_kernel,
        grid=(b, nb),
        in_specs=[pl.BlockSpec((1, blk, w), lambda bi, i: (bi, i, 0)),
                  pl.BlockSpec((1, s, w), lambda bi, i: (bi, 0, 1)),
                  pl.BlockSpec((1, s, w), lambda bi, i: (bi, 0, 2)),
                  pl.BlockSpec((n_heads, blk, 2 * blk), lambda bi, i: (0, 0, 0))],
        out_specs=pl.BlockSpec((1, blk, w), lambda bi, i: (bi, i, 0)),
        out_shape=jax.ShapeDtypeStruct((b, s, w), F32),
        scratch_shapes=[pltpu.VMEM((nb, blk, w), BF16),
                        pltpu.VMEM((nb, w, blk), BF16),
                        pltpu.VMEM((nb, w), F32),
                        pltpu.VMEM((n_heads, blk), F32),
                        pltpu.VMEM((n_heads, blk), F32),
                        pltpu.VMEM((n_heads, HEAD_DIM, blk), F32),
                        pltpu.VMEM((n_heads, nb, blk), F32),
                        pltpu.VMEM((n_heads, LANES, blk), BF16)],
        compiler_params=_cparams(2),
    )(qkv3, qkv3, qkv3, tt)


def _rel_bias_table(rel_bias, n_dist):
    n = jnp.arange(n_dist, dtype=jnp.int32)
    max_exact = N_BUCKETS // 2
    nf = jnp.maximum(n, 1).astype(F32)
    large = max_exact + (jnp.log(nf / max_exact) / math.log(MAX_DISTANCE / max_exact)
                         * (N_BUCKETS - max_exact)).astype(jnp.int32)
    large = jnp.minimum(large, N_BUCKETS - 1)
    bucket = jnp.where(n < max_exact, n, large)
    return rel_bias.astype(F32).T[:, bucket]


def _bias_tiles(tab):
    blk = MOBA_BLOCK
    period = 2 * blk
    n_heads = tab.shape[0]
    base = jnp.concatenate([tab[:, blk:period], tab[:, 0:blk]], axis=1)
    stride = 2 * period - 1
    reps = -(-(blk * stride) // period)
    seq = jnp.tile(base, (1, reps))[:, :blk * stride]
    return seq.reshape(n_heads, blk, stride)[:, :, :period]


def _mm3(a, b):
    a_hi = a.astype(BF16)
    b_hi = b.astype(BF16)
    a_lo = (a - a_hi.astype(F32)).astype(BF16)
    b_lo = (b - b_hi.astype(F32)).astype(BF16)
    return (jnp.dot(a_hi, b_hi, preferred_element_type=F32)
            + jnp.dot(a_hi, b_lo, preferred_element_type=F32)
            + jnp.dot(a_lo, b_hi, preferred_element_type=F32))


def _rwkv_prompt_kernel(zr_ref, zk_ref, zv_ref, zl_ref,
                        mur_ref, muk_ref, muv_ref, mul_ref,
                        w0_ref, w2_ref, a0_ref, a2_ref, g2_ref,
                        kk_ref, ka_ref, rk_ref, lg_ref, lb_ref,
                        rw_ref, st_ref,
                        p_s, last_r, last_k, last_v, last_l):
    t = pl.program_id(2)
    tc = zr_ref.shape[1]
    w = MXU_DIM
    c_len = RWKV_CHUNK
    hd_shift = HEAD_DIM.bit_length() - 1

    @pl.when(t == 0)
    def _():
        p_s[...] = jnp.zeros(p_s.shape, F32)
        last_r[...] = jnp.zeros(last_r.shape, F32)
        last_k[...] = jnp.zeros(last_k.shape, F32)
        last_v[...] = jnp.zeros(last_v.shape, F32)
        last_l[...] = jnp.zeros(last_l.shape, F32)

    def token_shift_mix(z_ref, last_ref, mu_ref):
        z = z_ref[0]
        row = lax.broadcasted_iota(jnp.int32, z.shape, 0)
        prev = jnp.where(row == 0, last_ref[...], pltpu.roll(z, 1, 0))
        last_ref[...] = z[tc - 1:tc, :]
        return z + (prev - z) * mu_ref[...]

    ri = lax.broadcasted_iota(jnp.int32, (w, w), 0)
    ci = lax.broadcasted_iota(jnp.int32, (w, w), 1)
    same_head = (ri >> hd_shift) == (ci >> hd_shift)
    tok_r = ri & (HEAD_DIM - 1)
    tok_c = ci & (HEAD_DIM - 1)
    strict = same_head & (tok_r > tok_c)
    incl = same_head & (tok_r >= tok_c)
    eye = ri == ci
    head_ones = jnp.where(same_head, 1.0, 0.0).astype(BF16)
    tri = jnp.where(lax.broadcasted_iota(jnp.int32, (c_len, c_len), 0)
                    >= lax.broadcasted_iota(jnp.int32, (c_len, c_len), 1), 1.0, 0.0).astype(BF16)

    def head_sum(x):
        return _mm_split_rhs(x, head_ones, 2)

    def stack(x):
        return jnp.where(same_head, jnp.concatenate([x] * HEADS_PER_TILE, axis=0), 0.0)

    def unstack(x):
        out = x[0:c_len]
        for e in range(1, HEADS_PER_TILE):
            out = out + x[e * c_len:(e + 1) * c_len]
        return out

    r = token_shift_mix(zr_ref, last_r, mur_ref)
    k = token_shift_mix(zk_ref, last_k, muk_ref)
    v = token_shift_mix(zv_ref, last_v, muv_ref)
    zl = token_shift_mix(zl_ref, last_l, mul_ref)

    w_pre = w0_ref[...] + _mm(jnp.tanh(zl), w2_ref[...], False)
    logd = -jnp.exp(-_softplus(-w_pre) - 0.5)
    eta = _sigmoid(a0_ref[...] + _mm(zl, a2_ref[...], False))
    gate = _mm(_sigmoid(zl), g2_ref[...], False)
    kk = k * kk_ref[...]
    kk = kk * lax.rsqrt(jnp.maximum(head_sum(kk * kk), 1e-24))
    kh = k * (1.0 + (eta - 1.0) * ka_ref[...])
    bonus = head_sum(r * kh * rk_ref[...]) * v
    kb = kk * eta

    chunks = range(tc // c_len)
    each = lambda f, *lists: [f(*(x[c] for x in lists)) for c in chunks]
    rows = [slice(c * c_len, (c + 1) * c_len) for c in chunks]
    ld = [logd[rw] for rw in rows]
    cum = each(lambda x: _mm_split(tri, x, 3), ld)
    cum_end = each(lambda x: x[c_len - 1:c_len, :], cum)
    e_inv = each(lambda x: jnp.exp(-x), cum)
    e_rem = each(lambda x, xe: jnp.exp(xe - x), cum, cum_end)
    a_t = [stack(-kk[rows[c]] * jnp.exp(cum[c] - ld[c])) for c in chunks]
    b_t = [stack(kb[rows[c]] * e_inv[c]) for c in chunks]
    k_t = [stack(kh[rows[c]] * e_inv[c]) for c in chunks]
    r_t = [stack(r[rows[c]] * jnp.exp(cum[c])) for c in chunks]
    b_h = [stack(kb[rows[c]] * e_rem[c]) for c in chunks]
    k_h = [stack(kh[rows[c]] * e_rem[c]) for c in chunks]
    v_s = [stack(v[rows[c]]) for c in chunks]

    prod = each(lambda a, rr, bb, kt: _mm_nt(jnp.concatenate([a, rr], axis=0),
                                             jnp.concatenate([bb, kt], axis=0)), a_t, r_t, b_t, k_t)
    n_mat = each(lambda x: jnp.where(strict, x[0:w, 0:w], 0.0), prod)
    a_k = each(lambda x: jnp.where(strict, x[0:w, w:2 * w], 0.0), prod)
    m_rb = each(lambda x: jnp.where(incl, x[w:2 * w, 0:w], 0.0), prod)
    m_rk = each(lambda x: jnp.where(incl, x[w:2 * w, w:2 * w], 0.0), prod)

    t_inv = each(lambda x: jnp.where(eye, 1.0, 0.0) + x, n_mat)
    n_pow = n_mat
    for _ in range(c_len.bit_length() - 2):
        n_pow = each(lambda x: _mm(x, x, False), n_pow)
        t_inv = each(lambda x, y: x + _mm(x, y, False), t_inv, n_pow)

    a_p = each(lambda x, y: _mm(x, y, False), t_inv, a_t)
    a_v = each(lambda x, y: _mm(x, y, False), a_k, v_s)
    v_p = each(lambda x, y: _mm(x, y, False), t_inv, a_v)
    r_p = each(lambda rr, m, ap: unstack(rr + _mm(m, ap, False)), r_t, m_rb, a_p)
    y_0 = each(lambda mb, vp, mk, vs: unstack(_mm(mb, vp, False) + _mm(mk, vs, False)),
               m_rb, v_p, m_rk, v_s)
    g_mat = each(lambda ce, bh, ap: jnp.where(eye, jnp.exp(ce), 0.0) + _mm_tn(bh, ap), cum_end, b_h, a_p)
    h_mat = each(lambda bh, vp, kh_, vs: _mm_tn(bh, vp) + _mm_tn(kh_, vs), b_h, v_p, k_h, v_s)

    p = p_s[...]
    ys = []
    for c in chunks:
        ys.append(_mm(r_p[c], p, False) + y_0[c])
        p = _mm3(g_mat[c], p) + h_mat[c]
    p_s[...] = p
    st_ref[0, 0] = p

    y = jnp.concatenate(ys, axis=0)
    mean = head_sum(y) * (1.0 / HEAD_DIM)
    yc = y - mean
    var = head_sum(yc * yc) * (1.0 / HEAD_DIM)
    yn = yc * lax.rsqrt(var + GN_EPS) * lg_ref[...] + lb_ref[...]
    rw_ref[0] = (yn + bonus) * gate


def _rwkv_prompt(z3, mu, w0, w2p, a0, a2p, g2p, kk, ka, rk, lg, lb, *, tc):
    b, s, _ = z3.shape
    w = MXU_DIM
    wr = w0.shape[1]
    nt = wr // w
    lw = w2p.shape[0]
    lora_blk = (3 * wr) // lw
    vec = lambda off: pl.BlockSpec((1, w), lambda bi, q, t, off=off: (0, off + q))
    zblk = lambda off: pl.BlockSpec((1, tc, w), lambda bi, q, t, off=off: (bi, t, off + q))
    lora_w = pl.BlockSpec((lw, w), lambda bi, q, t: (0, q))
    return pl.pallas_call(
        _rwkv_prompt_kernel,
        grid=(b, nt, s // tc),
        in_specs=[zblk(0), zblk(nt), zblk(2 * nt),
                  pl.BlockSpec((1, tc, lw), lambda bi, q, t: (bi, t, lora_blk)),
                  vec(0), vec(nt), vec(2 * nt),
                  pl.BlockSpec((1, lw), lambda bi, q, t: (0, lora_blk)),
                  vec(0), lora_w, vec(0), lora_w, lora_w,
                  vec(0), vec(0), vec(0), vec(0), vec(0)],
        out_specs=[pl.BlockSpec((1, tc, w), lambda bi, q, t: (bi, t, q)),
                   pl.BlockSpec((1, 1, w, w), lambda bi, q, t: (bi, q, 0, 0))],
        out_shape=[jax.ShapeDtypeStruct((b, s, wr), F32),
                   jax.ShapeDtypeStruct((b, nt, w, w), F32)],
        scratch_shapes=[pltpu.VMEM((w, w), F32),
                        pltpu.VMEM((1, w), F32), pltpu.VMEM((1, w), F32),
                        pltpu.VMEM((1, w), F32), pltpu.VMEM((1, lw), F32)],
        compiler_params=_cparams(3),
    )(z3, z3, z3, z3, mu, mu, mu, mu, w0, w2p, a0, a2p, g2p, kk, ka, rk, lg, lb)


def _state_from_tiles(st):
    b, nt = st.shape[:2]
    heads = []
    for q in range(nt):
        for e in range(HEADS_PER_TILE):
            sl = slice(e * HEAD_DIM, (e + 1) * HEAD_DIM)
            heads.append(st[:, q, sl, sl])
    return jnp.swapaxes(jnp.stack(heads, axis=1), -1, -2)


def _lora_padded(w2, a2, g2):
    lw = 3 * LANES
    n = w2.shape[1]
    o_a = LORA_W
    o_g = LORA_W + LORA_A
    w2p = jnp.zeros((lw, n), F32).at[0:o_a].set(w2)
    a2p = jnp.zeros((lw, n), F32).at[o_a:o_g].set(a2)
    g2p = jnp.zeros((lw, n), F32).at[o_g:o_g + LORA_G].set(g2)
    return w2p, a2p, g2p


ROUTER_LANE0 = N_GROUPS


def _first_argmax(x, lane):
    mx = jnp.max(x, axis=1, keepdims=True)
    idx = jnp.min(jnp.where(x == mx, lane, LANES), axis=1, keepdims=True)
    return mx, idx


def _outproj_router_kernel(x_ref, att_ref, rw_ref, wo_ref, g_ref, wr_ref, br_ref,
                           x1_ref, h2_ref, comb_ref, *, precise):
    wa = att_ref.shape[1]
    x1 = (x_ref[...] + _mm(att_ref[...], wo_ref[0:wa, :], precise)
          + _mm(rw_ref[...], wo_ref[wa:, :], precise))
    x1_ref[...] = x1
    h2 = _rms_norm(x1, g_ref[...])
    h2_ref[...] = h2.astype(h2_ref.dtype)
    if precise:
        logits = _mm(h2, wr_ref[...], True) + br_ref[...]
    else:
        logits = _mm3(h2, wr_ref[...]) + br_ref[...]
    n_exp = N_GROUPS * EXPERTS_PER_GROUP
    lane = lax.broadcasted_iota(jnp.int32, logits.shape, 1)
    is_grp = lane < N_GROUPS
    gl = jnp.where(is_grp, logits, NEG)
    g_max, grp = _first_argmax(gl, lane)
    p_grp = 1.0 / jnp.sum(jnp.where(is_grp, jnp.exp(gl - g_max), 0.0), axis=1, keepdims=True)
    e_lane = lane - ROUTER_LANE0
    grp_shift = EXPERTS_PER_GROUP.bit_length() - 1
    in_grp = (e_lane >= 0) & (e_lane < n_exp) & ((e_lane >> grp_shift) == grp)
    el = jnp.where(in_grp, logits, NEG)
    m1, i1 = _first_argmax(el, lane)
    el2 = jnp.where(lane == i1, NEG, el)
    m2, i2 = _first_argmax(el2, lane)
    ratio = jnp.exp(m2 - m1)
    p1 = 1.0 / (1.0 + ratio)
    p2 = ratio / (1.0 + ratio)
    comb_ref[...] = p_grp * (jnp.where(lane == i1, p1, 0.0) + jnp.where(lane == i2, p2, 0.0))


def _outproj_router(x2d, att, rw, wo, g, wr, br, *, tm, precise):
    t, d = x2d.shape
    wa, wr_cols = att.shape[1], rw.shape[1]
    h_dtype = F32 if precise else BF16
    tok = lambda n: pl.BlockSpec((tm, n), lambda i: (i, 0))
    full = lambda a: pl.BlockSpec(a.shape, lambda i: (0,) * a.ndim)
    return pl.pallas_call(
        functools.partial(_outproj_router_kernel, precise=precise),
        grid=(t // tm,),
        in_specs=[tok(d), tok(wa), tok(wr_cols), full(wo), full(g), full(wr), full(br)],
        out_specs=[tok(d), tok(d), tok(LANES)],
        out_shape=[jax.ShapeDtypeStruct((t, d), F32), jax.ShapeDtypeStruct((t, d), h_dtype),
                   jax.ShapeDtypeStruct((t, LANES), F32)],
        compiler_params=_cparams(1),
    )(x2d, att, rw, wo, g, wr, br)


def _moe_kernel(h2_ref, comb_ref, x1_ref, weg_ref, weu_ref, wed_ref, x2_ref, *, precise):
    e = pl.program_id(1)

    @pl.when(e == 0)
    def _():
        x2_ref[...] = x1_ref[...]

    h2 = h2_ref[...]
    lane = lax.broadcasted_iota(jnp.int32, comb_ref.shape, 1)
    weight = jnp.sum(jnp.where(lane == e + ROUTER_LANE0, comb_ref[...], 0.0), axis=1, keepdims=True)
    hg = _mm(h2, weg_ref[0], precise)
    hu = _mm(h2, weu_ref[0], precise)
    act = hg * _sigmoid(hg) * hu * weight
    x2_ref[...] += _mm(act, wed_ref[0], precise)


def _moe(h2, comb, x1, weg, weu, wed, *, tm, precise):
    t, d = x1.shape
    n_exp, _, f = weg.shape
    return pl.pallas_call(
        functools.partial(_moe_kernel, precise=precise),
        grid=(t // tm, n_exp),
        in_specs=[pl.BlockSpec((tm, d), lambda i, e: (i, 0)),
                  pl.BlockSpec((tm, LANES), lambda i, e: (i, 0)),
                  pl.BlockSpec((tm, d), lambda i, e: (i, 0)),
                  pl.BlockSpec((1, d, f), lambda i, e: (e, 0, 0)),
                  pl.BlockSpec((1, d, f), lambda i, e: (e, 0, 0)),
                  pl.BlockSpec((1, f, d), lambda i, e: (e, 0, 0))],
        out_specs=pl.BlockSpec((tm, d), lambda i, e: (i, 0)),
        out_shape=jax.ShapeDtypeStruct((t, d), F32),
        compiler_params=_cparams(2),
    )(h2, comb, x1, weg, weu, wed)


def _ple_kernel(x_ref, p_ref, wple_ref, wgate_ref, fg_ref, y_ref, *, precise, final):
    x = x_ref[...]
    emb = _mm(p_ref[...], wple_ref[...], precise)
    x = x + emb * _sigmoid(_mm(x, wgate_ref[...], precise))
    y_ref[...] = _rms_norm(x, fg_ref[...]) if final else x


def _ple(x2d, p2d, wple, wgate, fg, *, tm, precise, final):
    t, d = x2d.shape
    pd = p2d.shape[1]
    full = lambda a: pl.BlockSpec(a.shape, lambda i: (0,) * a.ndim)
    return pl.pallas_call(
        functools.partial(_ple_kernel, precise=precise, final=final),
        grid=(t // tm,),
        in_specs=[pl.BlockSpec((tm, d), lambda i: (i, 0)), pl.BlockSpec((tm, pd), lambda i: (i, 0)),
                  full(wple), full(wgate), full(fg)],
        out_specs=pl.BlockSpec((tm, d), lambda i: (i, 0)),
        out_shape=jax.ShapeDtypeStruct((t, d), F32),
        compiler_params=_cparams(1),
    )(x2d, p2d, wple, wgate, fg)


def _rwkv_sample_prep_kernel(z_ref, zp_ref, mu_ref, w0_ref, w2_ref, a0_ref, a2_ref, g2_ref,
                             kk_ref, ka_ref,
                             r_ref, d_ref, kkn_ref, kb_ref, kh_ref, v_ref, g_ref):
    wr = w0_ref.shape[1]
    hd_shift = HEAD_DIM.bit_length() - 1
    z = z_ref[...]
    zs = z + (zp_ref[...] - z) * mu_ref[...]
    r = zs[:, 0:wr]
    k = zs[:, wr:2 * wr]
    v = zs[:, 2 * wr:3 * wr]
    zl = zs[:, 3 * wr:]
    w_pre = w0_ref[...] + _mm(jnp.tanh(zl), w2_ref[...], True)
    decay = jnp.exp(-jnp.exp(-_softplus(-w_pre) - 0.5))
    eta = _sigmoid(a0_ref[...] + _mm(zl, a2_ref[...], True))
    ri = lax.broadcasted_iota(jnp.int32, (wr, wr), 0)
    ci = lax.broadcasted_iota(jnp.int32, (wr, wr), 1)
    head_ones = jnp.where((ri >> hd_shift) == (ci >> hd_shift), 1.0, 0.0)
    kk = k * kk_ref[...]
    kk = kk * lax.rsqrt(jnp.maximum(_mm(kk * kk, head_ones, True), 1e-24))
    r_ref[...] = r
    d_ref[...] = decay
    kkn_ref[...] = kk
    kb_ref[...] = kk * eta
    kh_ref[...] = k * (1.0 + (eta - 1.0) * ka_ref[...])
    v_ref[...] = v
    g_ref[...] = _mm(_sigmoid(zl), g2_ref[...], True)


def _rwkv_sample_prep(z, zp, mu, w0, w2p, a0, a2p, g2p, kk, ka):
    n = z.shape[0]
    wr = w0.shape[1]
    args = (z, zp, mu, w0, w2p, a0, a2p, g2p, kk, ka)
    full = lambda a: pl.BlockSpec(a.shape, lambda i: (0,) * a.ndim)
    return pl.pallas_call(
        _rwkv_sample_prep_kernel,
        grid=(1,),
        in_specs=[full(a) for a in args],
        out_specs=[pl.BlockSpec((n, wr), lambda i: (0, 0))] * 7,
        out_shape=[jax.ShapeDtypeStruct((n, wr), F32)] * 7,
        compiler_params=_cparams(1),
    )(*args)


def _rwkv_sample_step_kernel(s_ref, r_ref, d_ref, kkn_ref, kb_ref, kh_ref, v_ref, g_ref,
                             rk_ref, lg_ref, lb_ref, so_ref, rw_ref):
    n_heads = s_ref.shape[1]
    eye = (lax.broadcasted_iota(jnp.int32, (HEAD_DIM, HEAD_DIM), 0)
           == lax.broadcasted_iota(jnp.int32, (HEAD_DIM, HEAD_DIM), 1))
    for h in range(n_heads):
        hrow = lambda ref, lead=True: ref[0, h:h + 1, :] if lead else ref[h:h + 1, :]
        st = s_ref[0, h]
        r, v, kh = hrow(r_ref), hrow(v_ref), hrow(kh_ref)
        sa = jnp.sum(st * (-hrow(kkn_ref)), axis=1, keepdims=True)
        v_col = jnp.sum(jnp.where(eye, v, 0.0), axis=1, keepdims=True)
        st = st * hrow(d_ref) + sa * hrow(kb_ref) + v_col * kh
        so_ref[0, h] = st
        y_col = jnp.sum(st * r, axis=1, keepdims=True)
        y = jnp.sum(jnp.where(eye, y_col, 0.0), axis=0, keepdims=True)
        mean = jnp.mean(y, axis=1, keepdims=True)
        yc = y - mean
        var = jnp.mean(yc * yc, axis=1, keepdims=True)
        yn = yc * lax.rsqrt(var + GN_EPS) * hrow(lg_ref, False) + hrow(lb_ref, False)
        bonus = jnp.sum(r * kh * hrow(rk_ref, False), axis=1, keepdims=True) * v
        rw_ref[0, h:h + 1, :] = (yn + bonus) * hrow(g_ref)


def _rwkv_sample_step(state, per_head, rk, lg, lb):
    n, nh = state.shape[:2]
    tok = pl.BlockSpec((1, nh, HEAD_DIM), lambda i: (i, 0, 0))
    par = pl.BlockSpec((nh, HEAD_DIM), lambda i: (0, 0))
    st = pl.BlockSpec((1, nh, HEAD_DIM, HEAD_DIM), lambda i: (i, 0, 0, 0))
    return pl.pallas_call(
        _rwkv_sample_step_kernel,
        grid=(n,),
        in_specs=[st] + [tok] * 7 + [par] * 3,
        out_specs=[st, tok],
        out_shape=[jax.ShapeDtypeStruct(state.shape, F32),
                   jax.ShapeDtypeStruct((n, nh, HEAD_DIM), F32)],
        compiler_params=_cparams(1),
    )(state, *per_head, rk, lg, lb)


PAGES_PER_STEP = 8


def _interleave_mask(n_heads, width):
    return ((lax.broadcasted_iota(jnp.int32, (n_heads, width), 1) & (n_heads - 1))
            == lax.broadcasted_iota(jnp.int32, (n_heads, width), 0))


def _moba_sample_score_kernel(pt_ref, q_ref, *refs):
    pps = PAGES_PER_STEP
    k_refs, s_ref, gate_ref = refs[:pps], refs[pps], refs[pps + 1]
    page, n_heads, _ = k_refs[0].shape[1:]
    pages_per_blk = MOBA_BLOCK // page
    j = pl.program_id(1)
    q = q_ref[0]
    own = _interleave_mask(n_heads, page * n_heads)
    lane = lax.broadcasted_iota(jnp.int32, (n_heads, LANES), 1)

    @pl.when(j == 0)
    def _():
        gate_ref[0] = jnp.zeros(gate_ref.shape[1:], F32)

    gate = gate_ref[0]
    for blk_i in range(pps // pages_per_blk):
        k_sum = jnp.zeros(q.shape, F32)
        for half in range(pages_per_blk):
            i = blk_i * pages_per_blk + half
            kp = k_refs[i][0]
            k_sum = k_sum + jnp.sum(kp, axis=0)
            res = _mm_nt(q, kp.reshape(page * n_heads, HEAD_DIM))
            s_ref[0, i:i + 1, :] = jnp.sum(jnp.where(own, res, 0.0), axis=0, keepdims=True)
        g_col = jnp.sum(q * k_sum, axis=1, keepdims=True) * (1.0 / MOBA_BLOCK)
        gate = jnp.where(lane == j * (pps // pages_per_blk) + blk_i, g_col, gate)
    gate_ref[0] = gate


def _moba_sample_scores(page_table, q3, cache_k_l):
    n, n_pages = page_table.shape
    page, n_heads, hd = cache_k_l.shape[1:]
    pps = PAGES_PER_STEP
    k_spec = lambda i: pl.BlockSpec((1, page, n_heads, hd),
                                    lambda b, j, pt, i=i: (pt[b, j * pps + i], 0, 0, 0))
    grid_spec = pltpu.PrefetchScalarGridSpec(
        num_scalar_prefetch=1,
        grid=(n, n_pages // pps),
        in_specs=[pl.BlockSpec((1, n_heads, hd), lambda b, j, pt: (b, 0, 0))]
                 + [k_spec(i) for i in range(pps)],
        out_specs=[pl.BlockSpec((1, pps, page * n_heads), lambda b, j, pt: (b, j, 0)),
                   pl.BlockSpec((1, n_heads, LANES), lambda b, j, pt: (b, 0, 0))])
    return pl.pallas_call(
        _moba_sample_score_kernel,
        grid_spec=grid_spec,
        out_shape=[jax.ShapeDtypeStruct((n, n_pages, page * n_heads), F32),
                   jax.ShapeDtypeStruct((n, n_heads, LANES), F32)],
        compiler_params=_cparams(2),
    )(page_table, q3, *([cache_k_l] * pps))


def _moba_sample_select_kernel(s_ref, gate_ref, bias_ref, q_ref, kn_ref, b0_ref, p_ref, pown_ref):
    n_pages, width = s_ref.shape[1:]
    n_heads = q_ref.shape[1]
    page = width // n_heads
    blk_shift = (MOBA_BLOCK // page).bit_length() - 1
    n_blk = n_pages >> blk_shift
    scale = HEAD_DIM ** -0.5
    lane = lax.broadcasted_iota(jnp.int32, (n_heads, LANES), 1)
    gate = jnp.where(lane < n_blk, gate_ref[0], NEG)
    sel = jnp.zeros((n_heads, LANES), F32)
    for _ in range(MOBA_TOPK):
        _, pick = _first_argmax(gate, lane)
        hit = lane == pick
        sel = jnp.where(hit, 1.0, sel)
        gate = jnp.where(hit, NEG, gate)
    in_blk = jnp.where(lax.broadcasted_iota(jnp.int32, (n_pages, LANES), 1)
                       == (lax.broadcasted_iota(jnp.int32, (n_pages, LANES), 0) >> blk_shift), 1.0, 0.0)
    own = _interleave_mask(n_heads, width)
    page_sel = _mm_nt(in_blk, sel)
    key_sel = _mm(page_sel, jnp.where(own, 1.0, 0.0), False)
    sc = jnp.where(key_sel > 0.5, s_ref[0] * scale + bias_ref[...], NEG)
    per_head = lambda row, fill, red: red(jnp.where(own, row, fill), axis=1, keepdims=True)
    to_lanes = lambda col: jnp.sum(jnp.where(own, col, 0.0), axis=0, keepdims=True)
    s_own = jnp.sum(q_ref[0] * kn_ref[0], axis=1, keepdims=True) * scale + b0_ref[...]
    m = jnp.maximum(per_head(jnp.max(sc, axis=0, keepdims=True), NEG, jnp.max), s_own)
    p = jnp.exp(sc - to_lanes(m))
    p_own = jnp.exp(s_own - m)
    inv = 1.0 / (per_head(jnp.sum(p, axis=0, keepdims=True), 0.0, jnp.sum) + p_own)
    p_ref[0] = p * to_lanes(inv)
    pown_ref[0] = jnp.broadcast_to(p_own * inv, (n_heads, LANES))


def _moba_sample_select(scores, gate, bias_keys, q3, kn3, bias0):
    n, n_pages, width = scores.shape
    n_heads, hd = q3.shape[1:]
    tok = pl.BlockSpec((1, n_heads, hd), lambda b: (b, 0, 0))
    lane_blk = pl.BlockSpec((1, n_heads, LANES), lambda b: (b, 0, 0))
    keys = pl.BlockSpec((1, n_pages, width), lambda b: (b, 0, 0))
    return pl.pallas_call(
        _moba_sample_select_kernel,
        grid=(n,),
        in_specs=[keys, lane_blk, pl.BlockSpec((n_pages, width), lambda b: (0, 0)),
                  tok, tok, pl.BlockSpec((n_heads, 1), lambda b: (0, 0))],
        out_specs=[keys, lane_blk],
        out_shape=[jax.ShapeDtypeStruct((n, n_pages, width), F32),
                   jax.ShapeDtypeStruct((n, n_heads, LANES), F32)],
        compiler_params=_cparams(1),
    )(scores, gate, bias_keys, q3, kn3, bias0)


def _moba_sample_pv_kernel(pt_ref, p_ref, pown_ref, vn_ref, *refs):
    pps = PAGES_PER_STEP
    v_refs, o_ref = refs[:pps], refs[pps]
    page, n_heads, _ = v_refs[0].shape[1:]
    own = _interleave_mask(n_heads, page * n_heads)

    @pl.when(pl.program_id(1) == 0)
    def _():
        o_ref[0] = pown_ref[0, :, 0:HEAD_DIM] * vn_ref[0]

    acc = o_ref[0]
    for i in range(pps):
        w = jnp.where(own, p_ref[0, i:i + 1, :], 0.0)
        acc = acc + _mm(w, v_refs[i][0].reshape(page * n_heads, HEAD_DIM), False)
    o_ref[0] = acc


def _moba_sample_pv(page_table, probs, p_own, vn3, cache_v_l):
    n, n_pages = page_table.shape
    page, n_heads, hd = cache_v_l.shape[1:]
    pps = PAGES_PER_STEP
    v_spec = lambda i: pl.BlockSpec((1, page, n_heads, hd),
                                    lambda b, j, pt, i=i: (pt[b, j * pps + i], 0, 0, 0))
    tok = pl.BlockSpec((1, n_heads, hd), lambda b, j, pt: (b, 0, 0))
    grid_spec = pltpu.PrefetchScalarGridSpec(
        num_scalar_prefetch=1,
        grid=(n, n_pages // pps),
        in_specs=[pl.BlockSpec((1, pps, page * n_heads), lambda b, j, pt: (b, j, 0)),
                  pl.BlockSpec((1, n_heads, LANES), lambda b, j, pt: (b, 0, 0)), tok]
                 + [v_spec(i) for i in range(pps)],
        out_specs=tok)
    return pl.pallas_call(
        _moba_sample_pv_kernel,
        grid_spec=grid_spec,
        out_shape=jax.ShapeDtypeStruct((n, n_heads, hd), F32),
        compiler_params=_cparams(2),
    )(page_table, probs, p_own, vn3, *([cache_v_l] * pps))


def _sample_key_bias(tab, n_pages, page):
    n_heads, n_dist = tab.shape
    past_len = n_pages * page
    assert past_len >= n_dist
    tab_x = jnp.concatenate([tab, tab[:, -1:]], axis=1)
    near = jnp.flip(tab_x[:, 1:n_dist + 1], axis=1).T
    far = jnp.broadcast_to(tab[:, -1], (past_len - n_dist, n_heads))
    return jnp.concatenate([far, near], axis=0).reshape(n_pages, page * n_heads)


def _moba_sample(q, k_new, v_new, cache_k_l, cache_v_l, page_table, tab):
    n, width = q.shape
    n_pool, page, n_heads, hd = cache_k_l.shape
    n_pages = page_table.shape[1]
    past_len = n_pages * page
    assert hd == HEAD_DIM and n_heads * hd == width and n_heads & (n_heads - 1) == 0
    assert past_len % MOBA_BLOCK == 0 and past_len // MOBA_BLOCK >= MOBA_TOPK
    assert n_pages % PAGES_PER_STEP == 0 and PAGES_PER_STEP % (MOBA_BLOCK // page) == 0
    heads = lambda a: a.reshape(n, n_heads, hd)
    q3, kn3, vn3 = heads(q), heads(k_new), heads(v_new)
    scores, gate = _moba_sample_scores(page_table, q3, cache_k_l)
    probs, p_own = _moba_sample_select(scores, gate, _sample_key_bias(tab, n_pages, page), q3, kn3,
                                       tab[:, 0:1])
    return _moba_sample_pv(page_table, probs, p_own, vn3, cache_v_l).reshape(n, width)


def _layer_weights(l, precise, norm1_g, w_in, shift_mu, w0, w2, a0, a2, g2, k_k, k_a, r_k,
                   lnx_g, lnx_b, w_out, norm2_g, w_rg, b_rg, w_re, b_re, w_eg, w_eu, w_ed,
                   w_ple, w_pleg):
    wd = F32 if precise else BF16
    row = lambda x: x.reshape(1, -1).astype(F32)
    wa3 = 3 * (w_out.shape[1] - w0.shape[1])
    rw_cols = w_in.shape[2] - wa3
    z_pad = -rw_cols % LANES
    w2p, a2p, g2p = _lora_padded(w2[l], a2[l], g2[l])
    n_rt = N_GROUPS + N_GROUPS * EXPERTS_PER_GROUP
    w_router = jnp.pad(jnp.concatenate([w_rg[l], w_re[l]], axis=1), ((0, 0), (0, LANES - n_rt)))
    b_router = jnp.pad(jnp.concatenate([b_rg[l], b_re[l]]), (0, LANES - n_rt))
    return dict(
        norm1=row(norm1_g[l]),
        wqkv=w_in[l][:, :wa3].astype(wd),
        wz=jnp.pad(w_in[l][:, wa3:], ((0, 0), (0, z_pad))).astype(wd),
        mu=jnp.pad(shift_mu[l], (0, z_pad))[None],
        w0=row(w0[l]), w2p=w2p, a0=row(a0[l]), a2p=a2p, g2p=g2p,
        kk=row(k_k[l]), ka=row(k_a[l]), rk=row(r_k[l]), lg=row(lnx_g[l]), lb=row(lnx_b[l]),
        wo=w_out[l].astype(wd), norm2=row(norm2_g[l]), w_router=w_router, b_router=b_router[None],
        weg=w_eg[l].astype(wd), weu=w_eu[l].astype(wd), wed=w_ed[l].astype(wd),
        wple=w_ple[l].astype(wd), wgate=w_pleg[l].astype(wd),
        rw_cols=rw_cols, wa=wa3 // 3)


def _post_attention(x2d, att, rw, p2d, w, fg, *, tm, precise, final):
    x1, h2, comb = _outproj_router(x2d, att, rw, w["wo"], w["norm2"], w["w_router"], w["b_router"],
                                   tm=tm, precise=precise)
    tm_moe = min(x2d.shape[0], MOE_TOKEN_TILE)
    x2 = _moe(h2, comb, x1, w["weg"], w["weu"], w["wed"], tm=tm_moe, precise=precise)
    return _ple(x2, p2d, w["wple"], w["wgate"], fg, tm=tm, precise=precise, final=final)


def kernel(x_prompt, x_sample, p_prompt, p_sample, cache_k, cache_v, page_table, state_wkv, state_shift, norm1_g, w_in, rel_bias, shift_mu, w0, w2, a0, a2, g2, k_k, k_a, r_k, lnx_g, lnx_b, w_out, norm2_g, w_rg, b_rg, w_re, b_re, w_eg, w_eu, w_ed, w_ple, w_pleg, final_g):
    depth = norm1_g.shape[0]
    b, s, d = x_prompt.shape
    n = x_sample.shape[0]
    assert x_sample.shape[1] == 1
    n_heads = state_wkv.shape[2]
    layer_params = (norm1_g, w_in, shift_mu, w0, w2, a0, a2, g2, k_k, k_a, r_k, lnx_g, lnx_b,
                    w_out, norm2_g, w_rg, b_rg, w_re, b_re, w_eg, w_eu, w_ed, w_ple, w_pleg)
    fg = final_g.reshape(1, -1)
    tab = _rel_bias_table(rel_bias, 2 * MOBA_BLOCK)
    bias_tiles = _bias_tiles(tab * LOG2_E)
    xp = x_prompt.reshape(b * s, d)
    xs = x_sample.reshape(n, d)
    outs = [[] for _ in range(8)]
    for l in range(depth):
        final = l == depth - 1
        w = _layer_weights(l, False, *layer_params)
        wa, rw_cols = w["wa"], w["rw_cols"]
        heads = lambda a, lead: a.reshape(lead + (wa // HEAD_DIM, HEAD_DIM))
        qkv, z = _inproj(xp, w["norm1"], w["wqkv"], w["wz"], tm=512, precise=False)
        att = _moba_prompt(qkv.reshape(b, s, 3 * wa), bias_tiles).reshape(b * s, wa)
        z3 = z.reshape(b, s, -1)
        rw, st = _rwkv_prompt(z3, w["mu"], w["w0"], w["w2p"], w["a0"], w["a2p"], w["g2p"], w["kk"],
                              w["ka"], w["rk"], w["lg"], w["lb"], tc=RWKV_TOKEN_TILE)
        xp = _post_attention(xp, att, rw.reshape(b * s, -1), p_prompt[l].reshape(b * s, -1), w, fg,
                             tm=512, precise=False, final=final)
        outs[0].append(heads(qkv[:, wa:2 * wa], (b, s)))
        outs[1].append(heads(qkv[:, 2 * wa:], (b, s)))
        outs[4].append(_state_from_tiles(st))
        outs[6].append(z3[:, -1, :rw_cols])
        w = _layer_weights(l, True, *layer_params)
        qkv, z = _inproj(xs, w["norm1"], w["wqkv"], w["wz"], tm=n, precise=True)
        q_s, k_s, v_s = qkv[:, :wa], qkv[:, wa:2 * wa], qkv[:, 2 * wa:]
        att = _moba_sample(q_s, k_s, v_s, cache_k[l], cache_v[l], page_table, tab)
        z_prev = jnp.pad(state_shift[l], ((0, 0), (0, z.shape[1] - rw_cols)))
        per_tok = _rwkv_sample_prep(z, z_prev, w["mu"], w["w0"], w["w2p"], w["a0"], w["a2p"], w["g2p"],
                                    w["kk"], w["ka"])
        per_head = [a.reshape(n, n_heads, HEAD_DIM) for a in per_tok]
        hp = lambda a: a.reshape(n_heads, HEAD_DIM)
        st_s, rw_s = _rwkv_sample_step(state_wkv[l], per_head, hp(w["rk"]), hp(w["lg"]), hp(w["lb"]))
        xs = _post_attention(xs, att, rw_s.reshape(n, -1), p_sample[l].reshape(n, -1), w, fg,
                             tm=n, precise=True, final=final)
        outs[2].append(heads(k_s, (n, 1)))
        outs[3].append(heads(v_s, (n, 1)))
        outs[5].append(st_s)
        outs[7].append(z[:, :rw_cols])
    kp, vp, ks, vs, wp, ws, sp, ss = [jnp.stack(o) for o in outs]
    return (xp.reshape(b, s, d), xs.reshape(n, 1, d), kp, vp, ks, vs, wp, ws, sp, ss)
```

```python
import functools
import math

import jax
import jax.numpy as jnp
from jax import lax
from jax.experimental import pallas as pl
from jax.experimental.pallas import tpu as pltpu

F32 = jnp.float32
BF16 = jnp.bfloat16
HIGHEST = lax.Precision.HIGHEST

HEAD_DIM = 64
MOBA_BLOCK = 256
MOBA_TOPK = 3
N_BUCKETS = 32
MAX_DISTANCE = 128
LORA_W = 64
LORA_A = 64
LORA_G = 160
N_GROUPS = 4
EXPERTS_PER_GROUP = 4
RMS_EPS = 1e-6
GN_EPS = 64e-5
NEG = -1e30
LOG2_E = math.log2(math.e)

LANES = 128
MXU_DIM = 256
VMEM_LIMIT_BYTES = 56 * 1024 * 1024

MOE_TOKEN_TILE = 1024
MOBA_HEAD_GROUP = 8
RWKV_CHUNK = 64
RWKV_TOKEN_TILE = 512
HEADS_PER_TILE = MXU_DIM // HEAD_DIM


def _cparams(n_axes):
    return pltpu.CompilerParams(dimension_semantics=("arbitrary",) * n_axes,
                                vmem_limit_bytes=VMEM_LIMIT_BYTES)


def _mm(a, b, precise):
    if precise:
        return jnp.dot(a.astype(F32), b.astype(F32), precision=HIGHEST, preferred_element_type=F32)
    return jnp.dot(a.astype(BF16), b.astype(BF16), preferred_element_type=F32)


def _mm_nt(a, b):
    return lax.dot_general(a.astype(BF16), b.astype(BF16), (((1,), (1,)), ((), ())),
                           preferred_element_type=F32)


def _mm_tn(a, b):
    return lax.dot_general(a.astype(BF16), b.astype(BF16), (((0,), (0,)), ((), ())),
                           preferred_element_type=F32)


def _mm_split(a_exact, x, terms):
    a16 = a_exact.astype(BF16)
    acc = None
    rem = x
    for _ in range(terms):
        piece = rem.astype(BF16)
        part = jnp.dot(a16, piece, preferred_element_type=F32)
        acc = part if acc is None else acc + part
        rem = rem - piece.astype(F32)
    return acc


def _mm_split_rhs(x, b_exact, terms):
    b16 = b_exact.astype(BF16)
    acc = None
    rem = x
    for _ in range(terms):
        piece = rem.astype(BF16)
        part = jnp.dot(piece, b16, preferred_element_type=F32)
        acc = part if acc is None else acc + part
        rem = rem - piece.astype(F32)
    return acc


def _sigmoid(x):
    return 1.0 / (1.0 + jnp.exp(-x))


def _softplus(x):
    return jnp.maximum(x, 0.0) + jnp.log(1.0 + jnp.exp(-jnp.abs(x)))


def _rms_norm(x, g):
    return x * lax.rsqrt(jnp.mean(x * x, axis=-1, keepdims=True) + RMS_EPS) * g


def _inproj_kernel(x_ref, g_ref, wqkv_ref, wz_ref, qkv_ref, z_ref, *, precise):
    h = _rms_norm(x_ref[...], g_ref[...])
    qkv_ref[...] = _mm(h, wqkv_ref[...], precise)
    z_ref[...] = _mm(h, wz_ref[...], precise)


def _inproj(x2d, g, wqkv, wz, *, tm, precise):
    t, d = x2d.shape
    nq, nz = wqkv.shape[1], wz.shape[1]
    return pl.pallas_call(
        functools.partial(_inproj_kernel, precise=precise),
        grid=(t // tm,),
        in_specs=[pl.BlockSpec((tm, d), lambda i: (i, 0)),
                  pl.BlockSpec((1, d), lambda i: (0, 0)),
                  pl.BlockSpec((d, nq), lambda i: (0, 0)),
                  pl.BlockSpec((d, nz), lambda i: (0, 0))],
        out_specs=[pl.BlockSpec((tm, nq), lambda i: (i, 0)),
                   pl.BlockSpec((tm, nz), lambda i: (i, 0))],
        out_shape=[jax.ShapeDtypeStruct((t, nq), F32), jax.ShapeDtypeStruct((t, nz), F32)],
        compiler_params=_cparams(1),
    )(x2d, g, wqkv, wz)


def _moba_prompt_kernel(q_ref, k_ref, v_ref, tt_ref, o_ref,
                        kbf_s, vt_s, kmean_s, m_s, l_s, acc_s, sel_s, qbf_s):
    blk = MOBA_BLOCK
    i = pl.program_id(1)
    nb = kbf_s.shape[0]
    n_heads = tt_ref.shape[0]
    scale = HEAD_DIM ** -0.5 * LOG2_E

    @pl.when(i == 0)
    def _():
        for n in range(nb):
            kb = k_ref[0, n * blk:(n + 1) * blk, :]
            kbf_s[n] = kb.astype(BF16)
            kmean_s[n:n + 1, :] = jnp.mean(kb, axis=0, keepdims=True)
            vt_s[n] = v_ref[0, n * blk:(n + 1) * blk, :].T.astype(BF16)

    q_t = q_ref[0].T
    blk_row = lax.broadcasted_iota(jnp.int32, (nb, blk), 0)
    key_idx = lax.broadcasted_iota(jnp.int32, (blk, blk), 0)
    qry_idx = lax.broadcasted_iota(jnp.int32, (blk, blk), 1)
    causal = key_idx <= qry_idx
    upper_half = lax.broadcasted_iota(jnp.int32, (LANES, blk), 0) >= HEAD_DIM

    for h in range(n_heads):
        lanes = slice((h // 2) * LANES, (h // 2 + 1) * LANES)
        q_h = jnp.where(upper_half == (h % 2 == 1), q_t[lanes, :], 0.0)
        gate = jnp.dot(kmean_s[:, lanes], q_h, precision=HIGHEST, preferred_element_type=F32)
        gate = jnp.where(blk_row < i, gate, NEG)
        rank = jnp.zeros((nb, blk), jnp.int32)
        for m in range(nb):
            gm = gate[m:m + 1, :]
            beats = (gm > gate) | ((gm == gate) & (m < blk_row))
            rank = rank + beats.astype(jnp.int32)
        sel_s[h] = jnp.where((rank < MOBA_TOPK) & (blk_row < i), 1.0, 0.0)
        qbf_s[h] = (q_h * scale).astype(BF16)
    m_s[...] = jnp.full(m_s.shape, NEG, F32)
    l_s[...] = jnp.zeros(l_s.shape, F32)
    acc_s[...] = jnp.zeros(acc_s.shape, F32)

    def attend(group, n, bias_of, mask_of, uniform_bias=False):
        scores = [jnp.dot(kbf_s[n, :, (h // 2) * LANES:(h // 2 + 1) * LANES], qbf_s[h],
                          preferred_element_type=F32) for h in group]
        probs, alphas = [], []
        for h, s in zip(group, scores):
            m_old = m_s[h:h + 1, :]
            if uniform_bias:
                bias = bias_of(h)
                s = jnp.where(mask_of(h), s, NEG)
                m_new = jnp.maximum(m_old, jnp.max(s, axis=0, keepdims=True) + bias)
                p = jnp.exp2(s - (m_new - bias))
            else:
                s = jnp.where(mask_of(h), s + bias_of(h), NEG)
                m_new = jnp.maximum(m_old, jnp.max(s, axis=0, keepdims=True))
                p = jnp.exp2(s - m_new)
            alpha = jnp.exp2(m_old - m_new)
            l_s[h:h + 1, :] = alpha * l_s[h:h + 1, :] + jnp.sum(p, axis=0, keepdims=True)
            m_s[h:h + 1, :] = m_new
            probs.append(p.astype(BF16))
            alphas.append(alpha)
        pvs = [jnp.dot(vt_s[n, h * HEAD_DIM:(h + 1) * HEAD_DIM, :], p, preferred_element_type=F32)
               for h, p in zip(group, probs)]
        for h, alpha, pv in zip(group, alphas, pvs):
            acc_s[h] = alpha * acc_s[h] + pv

    for g0 in range(0, n_heads, MOBA_HEAD_GROUP):
        group = range(g0, g0 + MOBA_HEAD_GROUP)
        attend(group, i, lambda h: tt_ref[h, :, blk:2 * blk], lambda h: causal)

        @pl.when(i >= 1)
        def _(group=group):
            n = i - 1
            attend(group, n, lambda h: tt_ref[h, :, 0:blk], lambda h: sel_s[h, pl.ds(n, 1), :] > 0.5)

        def far_body(n, carry, group=group):
            attend(group, n, lambda h: tt_ref[h, 0:1, blk - 1:blk],
                   lambda h: sel_s[h, pl.ds(n, 1), :] > 0.5, uniform_bias=True)
            return carry

        lax.fori_loop(0, jnp.maximum(i - 1, 0), far_body, 0)

    for pair in range(n_heads // 2):
        outs = [acc_s[h] / l_s[h:h + 1, :] for h in (2 * pair, 2 * pair + 1)]
        o_ref[0, :, pair * LANES:(pair + 1) * LANES] = jnp.concatenate(outs, axis=0).T


def _moba_prompt(qkv3, tt):
    b, s, w3 = qkv3.shape
    w = w3 // 3
    blk = MOBA_BLOCK
    nb = s // blk
    n_heads = w // HEAD_DIM
    return pl.pallas_call(
        _moba_prompt_kernel,
        grid=(b, nb),
        in_specs=[pl.BlockSpec((1, blk, w), lambda bi, i: (bi, i, 0)),
                  pl.BlockSpec((1, s, w), lambda bi, i: (bi, 0, 1)),
                  pl.BlockSpec((1, s, w), lambda bi, i: (bi, 0, 2)),
                  pl.BlockSpec((n_heads, blk, 2 * blk), lambda bi, i: (0, 0, 0))],
        out_specs=pl.BlockSpec((1, blk, w), lambda bi, i: (bi, i, 0)),
        out_shape=jax.ShapeDtypeStruct((b, s, w), F32),
        scratch_shapes=[pltpu.VMEM((nb, blk, w), BF16),
                        pltpu.VMEM((nb, w, blk), BF16),
                        pltpu.VMEM((nb, w), F32),
                        pltpu.VMEM((n_heads, blk), F32),
                        pltpu.VMEM((n_heads, blk), F32),
                        pltpu.VMEM((n_heads, HEAD_DIM, blk), F32),
                        pltpu.VMEM((n_heads, nb, blk), F32),
                        pltpu.VMEM((n_heads, LANES, blk), BF16)],
        compiler_params=_cparams(2),
    )(qkv3, qkv3, qkv3, tt)


def _rel_bias_table(rel_bias, n_dist):
    n = jnp.arange(n_dist, dtype=jnp.int32)
    max_exact = N_BUCKETS // 2
    nf = jnp.maximum(n, 1).astype(F32)
    large = max_exact + (jnp.log(nf / max_exact) / math.log(MAX_DISTANCE / max_exact)
                         * (N_BUCKETS - max_exact)).astype(jnp.int32)
    large = jnp.minimum(large, N_BUCKETS - 1)
    bucket = jnp.where(n < max_exact, n, large)
    return rel_bias.astype(F32).T[:, bucket]


def _bias_tiles(tab):
    blk = MOBA_BLOCK
    period = 2 * blk
    n_heads = tab.shape[0]
    base = jnp.concatenate([tab[:, blk:period], tab[:, 0:blk]], axis=1)
    stride = 2 * period - 1
    reps = -(-(blk * stride) // period)
    seq = jnp.tile(base, (1, reps))[:, :blk * stride]
    return seq.reshape(n_heads, blk, stride)[:, :, :period]


def _mm3(a, b):
    a_hi = a.astype(BF16)
    b_hi = b.astype(BF16)
    a_lo = (a - a_hi.astype(F32)).astype(BF16)
    b_lo = (b - b_hi.astype(F32)).astype(BF16)
    return (jnp.dot(a_hi, b_hi, preferred_element_type=F32)
            + jnp.dot(a_hi, b_lo, preferred_element_type=F32)
            + jnp.dot(a_lo, b_hi, preferred_element_type=F32))


def _rwkv_prompt_kernel(zr_ref, zk_ref, zv_ref, zl_ref,
                        mur_ref, muk_ref, muv_ref, mul_ref,
                        w0_ref, w2_ref, a0_ref, a2_ref, g2_ref,
                        kk_ref, ka_ref, rk_ref, lg_ref, lb_ref,
                        rw_ref, st_ref,
                        p_s, last_r, last_k, last_v, last_l):
    t = pl.program_id(2)
    tc = zr_ref.shape[1]
    w = MXU_DIM
    c_len = RWKV_CHUNK
    hd_shift = HEAD_DIM.bit_length() - 1

    @pl.when(t == 0)
    def _():
        p_s[...] = jnp.zeros(p_s.shape, F32)
        last_r[...] = jnp.zeros(last_r.shape, F32)
        last_k[...] = jnp.zeros(last_k.shape, F32)
        last_v[...] = jnp.zeros(last_v.shape, F32)
        last_l[...] = jnp.zeros(last_l.shape, F32)

    def token_shift_mix(z_ref, last_ref, mu_ref):
        z = z_ref[0]
        row = lax.broadcasted_iota(jnp.int32, z.shape, 0)
        prev = jnp.where(row == 0, last_ref[...], pltpu.roll(z, 1, 0))
        last_ref[...] = z[tc - 1:tc, :]
        return z + (prev - z) * mu_ref[...]

    ri = lax.broadcasted_iota(jnp.int32, (w, w), 0)
    ci = lax.broadcasted_iota(jnp.int32, (w, w), 1)
    same_head = (ri >> hd_shift) == (ci >> hd_shift)
    tok_r = ri & (HEAD_DIM - 1)
    tok_c = ci & (HEAD_DIM - 1)
    strict = same_head & (tok_r > tok_c)
    incl = same_head & (tok_r >= tok_c)
    eye = ri == ci
    head_ones = jnp.where(same_head, 1.0, 0.0).astype(BF16)
    tri = jnp.where(lax.broadcasted_iota(jnp.int32, (c_len, c_len), 0)
                    >= lax.broadcasted_iota(jnp.int32, (c_len, c_len), 1), 1.0, 0.0).astype(BF16)

    def head_sum(x):
        return _mm_split_rhs(x, head_ones, 2)

    def stack(x):
        return jnp.where(same_head, jnp.concatenate([x] * HEADS_PER_TILE, axis=0), 0.0)

    def unstack(x):
        out = x[0:c_len]
        for e in range(1, HEADS_PER_TILE):
            out = out + x[e * c_len:(e + 1) * c_len]
        return out

    r = token_shift_mix(zr_ref, last_r, mur_ref)
    k = token_shift_mix(zk_ref, last_k, muk_ref)
    v = token_shift_mix(zv_ref, last_v, muv_ref)
    zl = token_shift_mix(zl_ref, last_l, mul_ref)

    w_pre = w0_ref[...] + _mm(jnp.tanh(zl), w2_ref[...], False)
    logd = -jnp.exp(-_softplus(-w_pre) - 0.5)
    eta = _sigmoid(a0_ref[...] + _mm(zl, a2_ref[...], False))
    gate = _mm(_sigmoid(zl), g2_ref[...], False)
    kk = k * kk_ref[...]
    kk = kk * lax.rsqrt(jnp.maximum(head_sum(kk * kk), 1e-24))
    kh = k * (1.0 + (eta - 1.0) * ka_ref[...])
    bonus = head_sum(r * kh * rk_ref[...]) * v
    kb = kk * eta

    chunks = range(tc // c_len)
    each = lambda f, *lists: [f(*(x[c] for x in lists)) for c in chunks]
    rows = [slice(c * c_len, (c + 1) * c_len) for c in chunks]
    ld = [logd[rw] for rw in rows]
    cum = each(lambda x: _mm_split(tri, x, 3), ld)
    cum_end = each(lambda x: x[c_len - 1:c_len, :], cum)
    e_inv = each(lambda x: jnp.exp(-x), cum)
    e_rem = each(lambda x, xe: jnp.exp(xe - x), cum, cum_end)
    a_t = [stack(-kk[rows[c]] * jnp.exp(cum[c] - ld[c])) for c in chunks]
    b_t = [stack(kb[rows[c]] * e_inv[c]) for c in chunks]
    k_t = [stack(kh[rows[c]] * e_inv[c]) for c in chunks]
    r_t = [stack(r[rows[c]] * jnp.exp(cum[c])) for c in chunks]
    b_h = [stack(kb[rows[c]] * e_rem[c]) for c in chunks]
    k_h = [stack(kh[rows[c]] * e_rem[c]) for c in chunks]
    v_s = [stack(v[rows[c]]) for c in chunks]

    prod = each(lambda a, rr, bb, kt: _mm_nt(jnp.concatenate([a, rr], axis=0),
                                             jnp.concatenate([bb, kt], axis=0)), a_t, r_t, b_t, k_t)
    n_mat = each(lambda x: jnp.where(strict, x[0:w, 0:w], 0.0), prod)
    a_k = each(lambda x: jnp.where(strict, x[0:w, w:2 * w], 0.0), prod)
    m_rb = each(lambda x: jnp.where(incl, x[w:2 * w, 0:w], 0.0), prod)
    m_rk = each(lambda x: jnp.where(incl, x[w:2 * w, w:2 * w], 0.0), prod)

    t_inv = each(lambda x: jnp.where(eye, 1.0, 0.0) + x, n_mat)
    n_pow = n_mat
    for _ in range(c_len.bit_length() - 2):
        n_pow = each(lambda x: _mm(x, x, False), n_pow)
        t_inv = each(lambda x, y: x + _mm(x, y, False), t_inv, n_pow)

    a_p = each(lambda x, y: _mm(x, y, False), t_inv, a_t)
    a_v = each(lambda x, y: _mm(x, y, False), a_k, v_s)
    v_p = each(lambda x, y: _mm(x, y, False), t_inv, a_v)
    r_p = each(lambda rr, m, ap: unstack(rr + _mm(m, ap, False)), r_t, m_rb, a_p)
    y_0 = each(lambda mb, vp, mk, vs: unstack(_mm(mb, vp, False) + _mm(mk, vs, False)),
               m_rb, v_p, m_rk, v_s)
    decay = each(jnp.exp, cum_end)
    mix = each(_mm_tn, a_p, b_h)
    add = each(lambda vp, bh, vs, kh_: _mm_tn(vp, bh) + _mm_tn(vs, kh_), v_p, b_h, v_s, k_h)

    st = p_s[...]
    ys = []
    for c in chunks:
        ys.append(_mm_nt(r_p[c], st) + y_0[c])
        st = st * decay[c] + _mm(st, mix[c], False) + add[c]
    p_s[...] = st
    st_ref[0, 0] = st

    y = jnp.concatenate(ys, axis=0)
    mean = head_sum(y) * (1.0 / HEAD_DIM)
    yc = y - mean
    var = head_sum(yc * yc) * (1.0 / HEAD_DIM)
    yn = yc * lax.rsqrt(var + GN_EPS) * lg_ref[...] + lb_ref[...]
    rw_ref[0] = (yn + bonus) * gate


def _rwkv_prompt(z3, mu, w0, w2p, a0, a2p, g2p, kk, ka, rk, lg, lb, *, tc):
    b, s, _ = z3.shape
    w = MXU_DIM
    wr = w0.shape[1]
    nt = wr // w
    lw = w2p.shape[0]
    lora_blk = (3 * wr) // lw
    vec = lambda off: pl.BlockSpec((1, w), lambda bi, q, t, off=off: (0, off + q))
    zblk = lambda off: pl.BlockSpec((1, tc, w), lambda bi, q, t, off=off: (bi, t, off + q))
    lora_w = pl.BlockSpec((lw, w), lambda bi, q, t: (0, q))
    return pl.pallas_call(
        _rwkv_prompt_kernel,
        grid=(b, nt, s // tc),
        in_specs=[zblk(0), zblk(nt), zblk(2 * nt),
                  pl.BlockSpec((1, tc, lw), lambda bi, q, t: (bi, t, lora_blk)),
                  vec(0), vec(nt), vec(2 * nt),
                  pl.BlockSpec((1, lw), lambda bi, q, t: (0, lora_blk)),
                  vec(0), lora_w, vec(0), lora_w, lora_w,
                  vec(0), vec(0), vec(0), vec(0), vec(0)],
        out_specs=[pl.BlockSpec((1, tc, w), lambda bi, q, t: (bi, t, q)),
                   pl.BlockSpec((1, 1, w, w), lambda bi, q, t: (bi, q, 0, 0))],
        out_shape=[jax.ShapeDtypeStruct((b, s, wr), F32),
                   jax.ShapeDtypeStruct((b, nt, w, w), F32)],
        scratch_shapes=[pltpu.VMEM((w, w), F32),
                        pltpu.VMEM((1, w), F32), pltpu.VMEM((1, w), F32),
                        pltpu.VMEM((1, w), F32), pltpu.VMEM((1, lw), F32)],
        compiler_params=_cparams(3),
    )(z3, z3, z3, z3, mu, mu, mu, mu, w0, w2p, a0, a2p, g2p, kk, ka, rk, lg, lb)


def _state_from_tiles(st):
    b, nt = st.shape[:2]
    heads = []
    for q in range(nt):
        for e in range(HEADS_PER_TILE):
            sl = slice(e * HEAD_DIM, (e + 1) * HEAD_DIM)
            heads.append(st[:, q, sl, sl])
    return jnp.stack(heads, axis=1)


def _lora_padded(w2, a2, g2):
    lw = 3 * LANES
    n = w2.shape[1]
    o_a = LORA_W
    o_g = LORA_W + LORA_A
    w2p = jnp.zeros((lw, n), F32).at[0:o_a].set(w2)
    a2p = jnp.zeros((lw, n), F32).at[o_a:o_g].set(a2)
    g2p = jnp.zeros((lw, n), F32).at[o_g:o_g + LORA_G].set(g2)
    return w2p, a2p, g2p


ROUTER_LANE0 = N_GROUPS


def _first_argmax(x, lane):
    mx = jnp.max(x, axis=1, keepdims=True)
    idx = jnp.min(jnp.where(x == mx, lane, LANES), axis=1, keepdims=True)
    return mx, idx


def _outproj_router_kernel(x_ref, att_ref, rw_ref, wo_ref, g_ref, wr_ref, br_ref,
                           x1_ref, h2_ref, comb_ref, *, precise):
    wa = att_ref.shape[1]
    x1 = (x_ref[...] + _mm(att_ref[...], wo_ref[0:wa, :], precise)
          + _mm(rw_ref[...], wo_ref[wa:, :], precise))
    x1_ref[...] = x1
    h2 = _rms_norm(x1, g_ref[...])
    h2_ref[...] = h2.astype(h2_ref.dtype)
    if precise:
        logits = _mm(h2, wr_ref[...], True) + br_ref[...]
    else:
        logits = _mm3(h2, wr_ref[...]) + br_ref[...]
    n_exp = N_GROUPS * EXPERTS_PER_GROUP
    lane = lax.broadcasted_iota(jnp.int32, logits.shape, 1)
    is_grp = lane < N_GROUPS
    gl = jnp.where(is_grp, logits, NEG)
    g_max, grp = _first_argmax(gl, lane)
    p_grp = 1.0 / jnp.sum(jnp.where(is_grp, jnp.exp(gl - g_max), 0.0), axis=1, keepdims=True)
    e_lane = lane - ROUTER_LANE0
    grp_shift = EXPERTS_PER_GROUP.bit_length() - 1
    in_grp = (e_lane >= 0) & (e_lane < n_exp) & ((e_lane >> grp_shift) == grp)
    el = jnp.where(in_grp, logits, NEG)
    m1, i1 = _first_argmax(el, lane)
    el2 = jnp.where(lane == i1, NEG, el)
    m2, i2 = _first_argmax(el2, lane)
    ratio = jnp.exp(m2 - m1)
    p1 = 1.0 / (1.0 + ratio)
    p2 = ratio / (1.0 + ratio)
    comb_ref[...] = p_grp * (jnp.where(lane == i1, p1, 0.0) + jnp.where(lane == i2, p2, 0.0))


def _outproj_router(x2d, att, rw, wo, g, wr, br, *, tm, precise):
    t, d = x2d.shape
    wa, wr_cols = att.shape[1], rw.shape[1]
    h_dtype = F32 if precise else BF16
    tok = lambda n: pl.BlockSpec((tm, n), lambda i: (i, 0))
    full = lambda a: pl.BlockSpec(a.shape, lambda i: (0,) * a.ndim)
    return pl.pallas_call(
        functools.partial(_outproj_router_kernel, precise=precise),
        grid=(t // tm,),
        in_specs=[tok(d), tok(wa), tok(wr_cols), full(wo), full(g), full(wr), full(br)],
        out_specs=[tok(d), tok(d), tok(LANES)],
        out_shape=[jax.ShapeDtypeStruct((t, d), F32), jax.ShapeDtypeStruct((t, d), h_dtype),
                   jax.ShapeDtypeStruct((t, LANES), F32)],
        compiler_params=_cparams(1),
    )(x2d, att, rw, wo, g, wr, br)


def _moe_kernel(h2_ref, comb_ref, x1_ref, weg_ref, weu_ref, wed_ref, x2_ref, *, precise):
    e = pl.program_id(1)

    @pl.when(e == 0)
    def _():
        x2_ref[...] = x1_ref[...]

    h2 = h2_ref[...]
    lane = lax.broadcasted_iota(jnp.int32, comb_ref.shape, 1)
    weight = jnp.sum(jnp.where(lane == e + ROUTER_LANE0, comb_ref[...], 0.0), axis=1, keepdims=True)
    hg = _mm(h2, weg_ref[0], precise)
    hu = _mm(h2, weu_ref[0], precise)
    act = hg * _sigmoid(hg) * hu * weight
    x2_ref[...] += _mm(act, wed_ref[0], precise)


def _moe(h2, comb, x1, weg, weu, wed, *, tm, precise):
    t, d = x1.shape
    n_exp, _, f = weg.shape
    return pl.pallas_call(
        functools.partial(_moe_kernel, precise=precise),
        grid=(t // tm, n_exp),
        in_specs=[pl.BlockSpec((tm, d), lambda i, e: (i, 0)),
                  pl.BlockSpec((tm, LANES), lambda i, e: (i, 0)),
                  pl.BlockSpec((tm, d), lambda i, e: (i, 0)),
                  pl.BlockSpec((1, d, f), lambda i, e: (e, 0, 0)),
                  pl.BlockSpec((1, d, f), lambda i, e: (e, 0, 0)),
                  pl.BlockSpec((1, f, d), lambda i, e: (e, 0, 0))],
        out_specs=pl.BlockSpec((tm, d), lambda i, e: (i, 0)),
        out_shape=jax.ShapeDtypeStruct((t, d), F32),
        compiler_params=_cparams(2),
    )(h2, comb, x1, weg, weu, wed)


def _ple_kernel(x_ref, p_ref, wple_ref, wgate_ref, fg_ref, y_ref, *, precise, final):
    x = x_ref[...]
    emb = _mm(p_ref[...], wple_ref[...], precise)
    x = x + emb * _sigmoid(_mm(x, wgate_ref[...], precise))
    y_ref[...] = _rms_norm(x, fg_ref[...]) if final else x


def _ple(x2d, p2d, wple, wgate, fg, *, tm, precise, final):
    t, d = x2d.shape
    pd = p2d.shape[1]
    full = lambda a: pl.BlockSpec(a.shape, lambda i: (0,) * a.ndim)
    return pl.pallas_call(
        functools.partial(_ple_kernel, precise=precise, final=final),
        grid=(t // tm,),
        in_specs=[pl.BlockSpec((tm, d), lambda i: (i, 0)), pl.BlockSpec((tm, pd), lambda i: (i, 0)),
                  full(wple), full(wgate), full(fg)],
        out_specs=pl.BlockSpec((tm, d), lambda i: (i, 0)),
        out_shape=jax.ShapeDtypeStruct((t, d), F32),
        compiler_params=_cparams(1),
    )(x2d, p2d, wple, wgate, fg)


def _rwkv_sample_prep_kernel(z_ref, zp_ref, mu_ref, w0_ref, w2_ref, a0_ref, a2_ref, g2_ref,
                             kk_ref, ka_ref,
                             r_ref, d_ref, kkn_ref, kb_ref, kh_ref, v_ref, g_ref):
    wr = w0_ref.shape[1]
    hd_shift = HEAD_DIM.bit_length() - 1
    z = z_ref[...]
    zs = z + (zp_ref[...] - z) * mu_ref[...]
    r = zs[:, 0:wr]
    k = zs[:, wr:2 * wr]
    v = zs[:, 2 * wr:3 * wr]
    zl = zs[:, 3 * wr:]
    w_pre = w0_ref[...] + _mm(jnp.tanh(zl), w2_ref[...], True)
    decay = jnp.exp(-jnp.exp(-_softplus(-w_pre) - 0.5))
    eta = _sigmoid(a0_ref[...] + _mm(zl, a2_ref[...], True))
    ri = lax.broadcasted_iota(jnp.int32, (wr, wr), 0)
    ci = lax.broadcasted_iota(jnp.int32, (wr, wr), 1)
    head_ones = jnp.where((ri >> hd_shift) == (ci >> hd_shift), 1.0, 0.0)
    kk = k * kk_ref[...]
    kk = kk * lax.rsqrt(jnp.maximum(_mm(kk * kk, head_ones, True), 1e-24))
    r_ref[...] = r
    d_ref[...] = decay
    kkn_ref[...] = kk
    kb_ref[...] = kk * eta
    kh_ref[...] = k * (1.0 + (eta - 1.0) * ka_ref[...])
    v_ref[...] = v
    g_ref[...] = _mm(_sigmoid(zl), g2_ref[...], True)


def _rwkv_sample_prep(z, zp, mu, w0, w2p, a0, a2p, g2p, kk, ka):
    n = z.shape[0]
    wr = w0.shape[1]
    args = (z, zp, mu, w0, w2p, a0, a2p, g2p, kk, ka)
    full = lambda a: pl.BlockSpec(a.shape, lambda i: (0,) * a.ndim)
    return pl.pallas_call(
        _rwkv_sample_prep_kernel,
        grid=(1,),
        in_specs=[full(a) for a in args],
        out_specs=[pl.BlockSpec((n, wr), lambda i: (0, 0))] * 7,
        out_shape=[jax.ShapeDtypeStruct((n, wr), F32)] * 7,
        compiler_params=_cparams(1),
    )(*args)


def _rwkv_sample_step_kernel(s_ref, r_ref, d_ref, kkn_ref, kb_ref, kh_ref, v_ref, g_ref,
                             rk_ref, lg_ref, lb_ref, so_ref, rw_ref):
    n_heads = s_ref.shape[1]
    eye = (lax.broadcasted_iota(jnp.int32, (HEAD_DIM, HEAD_DIM), 0)
           == lax.broadcasted_iota(jnp.int32, (HEAD_DIM, HEAD_DIM), 1))
    for h in range(n_heads):
        hrow = lambda ref, lead=True: ref[0, h:h + 1, :] if lead else ref[h:h + 1, :]
        st = s_ref[0, h]
        r, v, kh = hrow(r_ref), hrow(v_ref), hrow(kh_ref)
        sa = jnp.sum(st * (-hrow(kkn_ref)), axis=1, keepdims=True)
        v_col = jnp.sum(jnp.where(eye, v, 0.0), axis=1, keepdims=True)
        st = st * hrow(d_ref) + sa * hrow(kb_ref) + v_col * kh
        so_ref[0, h] = st
        y_col = jnp.sum(st * r, axis=1, keepdims=True)
        y = jnp.sum(jnp.where(eye, y_col, 0.0), axis=0, keepdims=True)
        mean = jnp.mean(y, axis=1, keepdims=True)
        yc = y - mean
        var = jnp.mean(yc * yc, axis=1, keepdims=True)
        yn = yc * lax.rsqrt(var + GN_EPS) * hrow(lg_ref, False) + hrow(lb_ref, False)
        bonus = jnp.sum(r * kh * hrow(rk_ref, False), axis=1, keepdims=True) * v
        rw_ref[0, h:h + 1, :] = (yn + bonus) * hrow(g_ref)


def _rwkv_sample_step(state, per_head, rk, lg, lb):
    n, nh = state.shape[:2]
    tok = pl.BlockSpec((1, nh, HEAD_DIM), lambda i: (i, 0, 0))
    par = pl.BlockSpec((nh, HEAD_DIM), lambda i: (0, 0))
    st = pl.BlockSpec((1, nh, HEAD_DIM, HEAD_DIM), lambda i: (i, 0, 0, 0))
    return pl.pallas_call(
        _rwkv_sample_step_kernel,
        grid=(n,),
        in_specs=[st] + [tok] * 7 + [par] * 3,
        out_specs=[st, tok],
        out_shape=[jax.ShapeDtypeStruct(state.shape, F32),
                   jax.ShapeDtypeStruct((n, nh, HEAD_DIM), F32)],
        compiler_params=_cparams(1),
    )(state, *per_head, rk, lg, lb)


PAGES_PER_STEP = 8


def _moba_sample_score_kernel(pt_ref, qc_ref, *refs):
    pps = PAGES_PER_STEP
    k_refs, s_ref, gate_ref = refs[:pps], refs[pps], refs[pps + 1]
    n_heads, _, page = k_refs[0].shape[1:]
    pages_per_blk = MOBA_BLOCK // page
    blks_per_step = pps // pages_per_blk
    j = pl.program_id(1)
    qc = qc_ref[0]
    lane = lax.broadcasted_iota(jnp.int32, (n_heads, LANES), 1)

    @pl.when(j == 0)
    def _():
        gate_ref[0] = jnp.zeros(gate_ref.shape[1:], F32)

    gate = gate_ref[0]
    for blk_i in range(blks_per_step):
        blk_sum = jnp.zeros((n_heads, page), F32)
        for half in range(pages_per_blk):
            i = blk_i * pages_per_blk + half
            s = jnp.sum(k_refs[i][0] * qc, axis=1)
            s_ref[0, :, i * page:(i + 1) * page] = s
            blk_sum = blk_sum + s
        g_col = jnp.sum(blk_sum, axis=1, keepdims=True) * (1.0 / MOBA_BLOCK)
        gate = jnp.where(lane == j * blks_per_step + blk_i, g_col, gate)
    gate_ref[0] = gate


def _moba_sample_scores(page_table, q_cols, cache_t):
    n, n_pages = page_table.shape
    n_heads, hd, page = cache_t.shape[1:]
    pps = PAGES_PER_STEP
    k_spec = lambda i: pl.BlockSpec((1, n_heads, hd, page),
                                    lambda b, j, pt, i=i: (pt[b, j * pps + i], 0, 0, 0))
    grid_spec = pltpu.PrefetchScalarGridSpec(
        num_scalar_prefetch=1,
        grid=(n, n_pages // pps),
        in_specs=[pl.BlockSpec((1, n_heads, hd, page), lambda b, j, pt: (b, 0, 0, 0))]
                 + [k_spec(i) for i in range(pps)],
        out_specs=[pl.BlockSpec((1, n_heads, pps * page), lambda b, j, pt: (b, 0, j)),
                   pl.BlockSpec((1, n_heads, LANES), lambda b, j, pt: (b, 0, 0))])
    return pl.pallas_call(
        _moba_sample_score_kernel,
        grid_spec=grid_spec,
        out_shape=[jax.ShapeDtypeStruct((n, n_heads, n_pages * page), F32),
                   jax.ShapeDtypeStruct((n, n_heads, LANES), F32)],
        compiler_params=_cparams(2),
    )(page_table, q_cols, *([cache_t] * pps))


def _moba_sample_select_kernel(s_ref, gate_ref, bias_ref, q_ref, kn_ref, b0_ref, p_ref, pown_ref):
    n_heads, n_keys = s_ref.shape[1:]
    blk_shift = MOBA_BLOCK.bit_length() - 1
    n_blk = n_keys // MOBA_BLOCK
    scale = HEAD_DIM ** -0.5
    s = s_ref[0]
    member = jnp.where(lax.broadcasted_iota(jnp.int32, (LANES, n_keys), 1) >> blk_shift
                       == lax.broadcasted_iota(jnp.int32, (LANES, n_keys), 0), 1.0, 0.0).astype(BF16)
    lane = lax.broadcasted_iota(jnp.int32, (n_heads, LANES), 1)
    gate = jnp.where(lane < n_blk, gate_ref[0], NEG)
    sel = jnp.zeros((n_heads, LANES), F32)
    for _ in range(MOBA_TOPK):
        _, pick = _first_argmax(gate, lane)
        hit = lane == pick
        sel = jnp.where(hit, 1.0, sel)
        gate = jnp.where(hit, NEG, gate)
    key_sel = _mm(sel, member, False)
    sc = jnp.where(key_sel > 0.5, s * scale + bias_ref[...], NEG)
    s_own = jnp.sum(q_ref[0] * kn_ref[0], axis=1, keepdims=True) * scale + b0_ref[...]
    m = jnp.maximum(jnp.max(sc, axis=1, keepdims=True), s_own)
    p = jnp.exp(sc - m)
    p_own = jnp.exp(s_own - m)
    inv = 1.0 / (jnp.sum(p, axis=1, keepdims=True) + p_own)
    p_ref[0] = p * inv
    pown_ref[0] = jnp.broadcast_to(p_own * inv, (n_heads, LANES))


def _moba_sample_select(scores, gate, bias_keys, q3, kn3, bias0):
    n, n_heads, n_keys = scores.shape
    hd = q3.shape[2]
    tok = pl.BlockSpec((1, n_heads, hd), lambda b: (b, 0, 0))
    keys = pl.BlockSpec((1, n_heads, n_keys), lambda b: (b, 0, 0))
    lane_blk = pl.BlockSpec((1, n_heads, LANES), lambda b: (b, 0, 0))
    return pl.pallas_call(
        _moba_sample_select_kernel,
        grid=(n,),
        in_specs=[keys, lane_blk, pl.BlockSpec((n_heads, n_keys), lambda b: (0, 0)), tok, tok,
                  pl.BlockSpec((n_heads, 1), lambda b: (0, 0))],
        out_specs=[keys, lane_blk],
        out_shape=[jax.ShapeDtypeStruct((n, n_heads, n_keys), F32),
                   jax.ShapeDtypeStruct((n, n_heads, LANES), F32)],
        compiler_params=_cparams(1),
    )(scores, gate, bias_keys, q3, kn3, bias0)


def _moba_sample_pv_kernel(pt_ref, p_ref, pown_ref, vn_ref, *refs):
    pps = PAGES_PER_STEP
    v_refs, o_ref, acc_s = refs[:pps], refs[pps], refs[pps + 1]
    n_heads, hd, page = v_refs[0].shape[1:]
    j = pl.program_id(1)

    @pl.when(j == 0)
    def _():
        acc_s[...] = jnp.zeros(acc_s.shape, F32)

    acc = acc_s[...]
    for i, v_ref in enumerate(v_refs):
        acc = acc + v_ref[0] * p_ref[0, :, i * page:(i + 1) * page][:, None, :]
    acc_s[...] = acc

    @pl.when(j == pl.num_programs(1) - 1)
    def _():
        eye = (lax.broadcasted_iota(jnp.int32, (hd, hd), 0)
               == lax.broadcasted_iota(jnp.int32, (hd, hd), 1))
        for h in range(n_heads):
            col = jnp.sum(acc_s[h], axis=1, keepdims=True)
            row = jnp.sum(jnp.where(eye, col, 0.0), axis=0, keepdims=True)
            o_ref[0, h:h + 1, :] = row + pown_ref[0, h:h + 1, 0:hd] * vn_ref[0, h:h + 1, :]


def _moba_sample_pv(page_table, probs, p_own, vn3, cache_t):
    n, n_pages = page_table.shape
    n_heads, hd, page = cache_t.shape[1:]
    pps = PAGES_PER_STEP
    v_spec = lambda i: pl.BlockSpec((1, n_heads, hd, page),
                                    lambda b, j, pt, i=i: (pt[b, j * pps + i], 0, 0, 0))
    tok = pl.BlockSpec((1, n_heads, hd), lambda b, j, pt: (b, 0, 0))
    grid_spec = pltpu.PrefetchScalarGridSpec(
        num_scalar_prefetch=1,
        grid=(n, n_pages // pps),
        in_specs=[pl.BlockSpec((1, n_heads, pps * page), lambda b, j, pt: (b, 0, j)),
                  pl.BlockSpec((1, n_heads, LANES), lambda b, j, pt: (b, 0, 0)), tok]
                 + [v_spec(i) for i in range(pps)],
        out_specs=tok,
        scratch_shapes=[pltpu.VMEM((n_heads, hd, page), F32)])
    return pl.pallas_call(
        _moba_sample_pv_kernel,
        grid_spec=grid_spec,
        out_shape=jax.ShapeDtypeStruct((n, n_heads, hd), F32),
        compiler_params=_cparams(2),
    )(page_table, probs, p_own, vn3, *([cache_t] * pps))


def _sample_key_bias(tab, past_len):
    n_heads, n_dist = tab.shape
    assert past_len >= n_dist
    tab_x = jnp.concatenate([tab, tab[:, -1:]], axis=1)
    near = jnp.flip(tab_x[:, 1:n_dist + 1], axis=1)
    far = jnp.broadcast_to(tab[:, -1:], (n_heads, past_len - n_dist))
    return jnp.concatenate([far, near], axis=1)


def _moba_sample(q, k_new, v_new, cache_k_l, cache_v_l, page_table, tab):
    n, width = q.shape
    n_pool, page, n_heads, hd = cache_k_l.shape
    n_pages = page_table.shape[1]
    past_len = n_pages * page
    assert hd == HEAD_DIM and n_heads * hd == width and page == LANES
    assert past_len % MOBA_BLOCK == 0 and past_len // MOBA_BLOCK >= MOBA_TOPK
    assert n_pages % PAGES_PER_STEP == 0
    heads = lambda a: a.reshape(n, n_heads, hd)
    q3, kn3, vn3 = heads(q), heads(k_new), heads(v_new)
    ck = jnp.transpose(cache_k_l, (0, 2, 3, 1))
    cv = jnp.transpose(cache_v_l, (0, 2, 3, 1))
    q_cols = jnp.broadcast_to(q3[..., None], (n, n_heads, hd, page))
    assert PAGES_PER_STEP % (MOBA_BLOCK // page) == 0
    scores, gate = _moba_sample_scores(page_table, q_cols, ck)
    probs, p_own = _moba_sample_select(scores, gate, _sample_key_bias(tab, past_len), q3, kn3,
                                       tab[:, 0:1])
    return _moba_sample_pv(page_table, probs, p_own, vn3, cv).reshape(n, width)


def _layer_weights(l, precise, norm1_g, w_in, shift_mu, w0, w2, a0, a2, g2, k_k, k_a, r_k,
                   lnx_g, lnx_b, w_out, norm2_g, w_rg, b_rg, w_re, b_re, w_eg, w_eu, w_ed,
                   w_ple, w_pleg):
    wd = F32 if precise else BF16
    row = lambda x: x.reshape(1, -1).astype(F32)
    wa3 = 3 * (w_out.shape[1] - w0.shape[1])
    rw_cols = w_in.shape[2] - wa3
    z_pad = -rw_cols % LANES
    w2p, a2p, g2p = _lora_padded(w2[l], a2[l], g2[l])
    n_rt = N_GROUPS + N_GROUPS * EXPERTS_PER_GROUP
    w_router = jnp.pad(jnp.concatenate([w_rg[l], w_re[l]], axis=1), ((0, 0), (0, LANES - n_rt)))
    b_router = jnp.pad(jnp.concatenate([b_rg[l], b_re[l]]), (0, LANES - n_rt))
    return dict(
        norm1=row(norm1_g[l]),
        wqkv=w_in[l][:, :wa3].astype(wd),
        wz=jnp.pad(w_in[l][:, wa3:], ((0, 0), (0, z_pad))).astype(wd),
        mu=jnp.pad(shift_mu[l], (0, z_pad))[None],
        w0=row(w0[l]), w2p=w2p, a0=row(a0[l]), a2p=a2p, g2p=g2p,
        kk=row(k_k[l]), ka=row(k_a[l]), rk=row(r_k[l]), lg=row(lnx_g[l]), lb=row(lnx_b[l]),
        wo=w_out[l].astype(wd), norm2=row(norm2_g[l]), w_router=w_router, b_router=b_router[None],
        weg=w_eg[l].astype(wd), weu=w_eu[l].astype(wd), wed=w_ed[l].astype(wd),
        wple=w_ple[l].astype(wd), wgate=w_pleg[l].astype(wd),
        rw_cols=rw_cols, wa=wa3 // 3)


def _post_attention(x2d, att, rw, p2d, w, fg, *, tm, precise, final):
    x1, h2, comb = _outproj_router(x2d, att, rw, w["wo"], w["norm2"], w["w_router"], w["b_router"],
                                   tm=tm, precise=precise)
    tm_moe = min(x2d.shape[0], MOE_TOKEN_TILE)
    x2 = _moe(h2, comb, x1, w["weg"], w["weu"], w["wed"], tm=tm_moe, precise=precise)
    return _ple(x2, p2d, w["wple"], w["wgate"], fg, tm=tm, precise=precise, final=final)


def kernel(x_prompt, x_sample, p_prompt, p_sample, cache_k, cache_v, page_table, state_wkv, state_shift, norm1_g, w_in, rel_bias, shift_mu, w0, w2, a0, a2, g2, k_k, k_a, r_k, lnx_g, lnx_b, w_out, norm2_g, w_rg, b_rg, w_re, b_re, w_eg, w_eu, w_ed, w_ple, w_pleg, final_g):
    depth = norm1_g.shape[0]
    b, s, d = x_prompt.shape
    n = x_sample.shape[0]
    assert x_sample.shape[1] == 1
    n_heads = state_wkv.shape[2]
    layer_params = (norm1_g, w_in, shift_mu, w0, w2, a0, a2, g2, k_k, k_a, r_k, lnx_g, lnx_b,
                    w_out, norm2_g, w_rg, b_rg, w_re, b_re, w_eg, w_eu, w_ed, w_ple, w_pleg)
    fg = final_g.reshape(1, -1)
    tab = _rel_bias_table(rel_bias, 2 * MOBA_BLOCK)
    bias_tiles = _bias_tiles(tab * LOG2_E)
    xp = x_prompt.reshape(b * s, d)
    xs = x_sample.reshape(n, d)
    outs = [[] for _ in range(8)]
    for l in range(depth):
        final = l == depth - 1
        w = _layer_weights(l, False, *layer_params)
        wa, rw_cols = w["wa"], w["rw_cols"]
        heads = lambda a, lead: a.reshape(lead + (wa // HEAD_DIM, HEAD_DIM))
        qkv, z = _inproj(xp, w["norm1"], w["wqkv"], w["wz"], tm=512, precise=False)
        att = _moba_prompt(qkv.reshape(b, s, 3 * wa), bias_tiles).reshape(b * s, wa)
        z3 = z.reshape(b, s, -1)
        rw, st = _rwkv_prompt(z3, w["mu"], w["w0"], w["w2p"], w["a0"], w["a2p"], w["g2p"], w["kk"],
                              w["ka"], w["rk"], w["lg"], w["lb"], tc=RWKV_TOKEN_TILE)
        xp = _post_attention(xp, att, rw.reshape(b * s, -1), p_prompt[l].reshape(b * s, -1), w, fg,
                             tm=512, precise=False, final=final)
        outs[0].append(heads(qkv[:, wa:2 * wa], (b, s)))
        outs[1].append(heads(qkv[:, 2 * wa:], (b, s)))
        outs[4].append(_state_from_tiles(st))
        outs[6].append(z3[:, -1, :rw_cols])
        w = _layer_weights(l, True, *layer_params)
        qkv, z = _inproj(xs, w["norm1"], w["wqkv"], w["wz"], tm=n, precise=True)
        q_s, k_s, v_s = qkv[:, :wa], qkv[:, wa:2 * wa], qkv[:, 2 * wa:]
        att = _moba_sample(q_s, k_s, v_s, cache_k[l], cache_v[l], page_table, tab)
        z_prev = jnp.pad(state_shift[l], ((0, 0), (0, z.shape[1] - rw_cols)))
        per_tok = _rwkv_sample_prep(z, z_prev, w["mu"], w["w0"], w["w2p"], w["a0"], w["a2p"], w["g2p"],
                                    w["kk"], w["ka"])
        per_head = [a.reshape(n, n_heads, HEAD_DIM) for a in per_tok]
        hp = lambda a: a.reshape(n_heads, HEAD_DIM)
        st_s, rw_s = _rwkv_sample_step(state_wkv[l], per_head, hp(w["rk"]), hp(w["lg"]), hp(w["lb"]))
        xs = _post_attention(xs, att, rw_s.reshape(n, -1), p_sample[l].reshape(n, -1), w, fg,
                             tm=n, precise=True, final=final)
        outs[2].append(heads(k_s, (n, 1)))
        outs[3].append(heads(v_s, (n, 1)))
        outs[5].append(st_s)
        outs[7].append(z[:, :rw_cols])
    kp, vp, ks, vs, wp, ws, sp, ss = [jnp.stack(o) for o in outs]
    return (xp.reshape(b, s, d), xs.reshape(n, 1, d), kp, vp, ks, vs, wp, ws, sp, ss)
```

```python
import functools
import math

import jax
import jax.numpy as jnp
from jax import lax
from jax.experimental import pallas as pl
from jax.experimental.pallas import tpu as pltpu

F32 = jnp.float32
BF16 = jnp.bfloat16
HIGHEST = lax.Precision.HIGHEST

HEAD_DIM = 64
MOBA_BLOCK = 256
MOBA_TOPK = 3
N_BUCKETS = 32
MAX_DISTANCE = 128
LORA_W = 64
LORA_A = 64
LORA_G = 160
N_GROUPS = 4
EXPERTS_PER_GROUP = 4
RMS_EPS = 1e-6
GN_EPS = 64e-5
NEG = -1e30
LOG2_E = math.log2(math.e)

LANES = 128
MXU_DIM = 256
VMEM_LIMIT_BYTES = 56 * 1024 * 1024

MOE_TOKEN_TILE = 1024
MOBA_HEAD_GROUP = 8
RWKV_CHUNK = 64
RWKV_TOKEN_TILE = 512
HEADS_PER_TILE = MXU_DIM // HEAD_DIM


def _cparams(n_axes):
    return pltpu.CompilerParams(dimension_semantics=("arbitrary",) * n_axes,
                                vmem_limit_bytes=VMEM_LIMIT_BYTES)


def _mm(a, b, precise):
    if precise:
        return jnp.dot(a.astype(F32), b.astype(F32), precision=HIGHEST, preferred_element_type=F32)
    return jnp.dot(a.astype(BF16), b.astype(BF16), preferred_element_type=F32)


def _mm_nt(a, b):
    return lax.dot_general(a.astype(BF16), b.astype(BF16), (((1,), (1,)), ((), ())),
                           preferred_element_type=F32)


def _mm_tn(a, b):
    return lax.dot_general(a.astype(BF16), b.astype(BF16), (((0,), (0,)), ((), ())),
                           preferred_element_type=F32)


def _mm_split(a_exact, x, terms):
    a16 = a_exact.astype(BF16)
    acc = None
    rem = x
    for _ in range(terms):
        piece = rem.astype(BF16)
        part = jnp.dot(a16, piece, preferred_element_type=F32)
        acc = part if acc is None else acc + part
        rem = rem - piece.astype(F32)
    return acc


def _mm_split_rhs(x, b_exact, terms):
    b16 = b_exact.astype(BF16)
    acc = None
    rem = x
    for _ in range(terms):
        piece = rem.astype(BF16)
        part = jnp.dot(piece, b16, preferred_element_type=F32)
        acc = part if acc is None else acc + part
        rem = rem - piece.astype(F32)
    return acc


def _sigmoid(x):
    return 1.0 / (1.0 + jnp.exp(-x))


def _softplus(x):
    return jnp.maximum(x, 0.0) + jnp.log(1.0 + jnp.exp(-jnp.abs(x)))


def _rms_norm(x, g):
    return x * lax.rsqrt(jnp.mean(x * x, axis=-1, keepdims=True) + RMS_EPS) * g


def _inproj_kernel(x_ref, g_ref, wqkv_ref, wz_ref, qkv_ref, z_ref, *, precise):
    h = _rms_norm(x_ref[...], g_ref[...])
    qkv_ref[...] = _mm(h, wqkv_ref[...], precise)
    z_ref[...] = _mm(h, wz_ref[...], precise)


def _inproj(x2d, g, wqkv, wz, *, tm, precise):
    t, d = x2d.shape
    nq, nz = wqkv.shape[1], wz.shape[1]
    return pl.pallas_call(
        functools.partial(_inproj_kernel, precise=precise),
        grid=(t // tm,),
        in_specs=[pl.BlockSpec((tm, d), lambda i: (i, 0)),
                  pl.BlockSpec((1, d), lambda i: (0, 0)),
                  pl.BlockSpec((d, nq), lambda i: (0, 0)),
                  pl.BlockSpec((d, nz), lambda i: (0, 0))],
        out_specs=[pl.BlockSpec((tm, nq), lambda i: (i, 0)),
                   pl.BlockSpec((tm, nz), lambda i: (i, 0))],
        out_shape=[jax.ShapeDtypeStruct((t, nq), F32), jax.ShapeDtypeStruct((t, nz), F32)],
        compiler_params=_cparams(1),
    )(x2d, g, wqkv, wz)


def _inproj_seq_kernel(x_ref, g_ref, wqkv_ref, wz_ref, qk_ref, kt_ref, vt_ref, z_ref):
    wa = kt_ref.shape[1]
    h = _rms_norm(x_ref[...], g_ref[...])
    qkv = _mm(h, wqkv_ref[...], False)
    qk_ref[...] = qkv[:, :2 * wa]
    kt_ref[0] = qkv[:, wa:2 * wa].T
    vt_ref[0] = qkv[:, 2 * wa:].T
    z_ref[...] = _mm(h, wz_ref[...], False)


def _inproj_seq(x2d, g, wqkv, wz, *, b, tm):
    t, d = x2d.shape
    s = t // b
    nq, nz = wqkv.shape[1], wz.shape[1]
    wa = nq // 3
    per_seq = s // tm
    feat_major = pl.BlockSpec((1, wa, tm), lambda i: (i // per_seq, 0, i % per_seq))
    return pl.pallas_call(
        _inproj_seq_kernel,
        grid=(t // tm,),
        in_specs=[pl.BlockSpec((tm, d), lambda i: (i, 0)),
                  pl.BlockSpec((1, d), lambda i: (0, 0)),
                  pl.BlockSpec((d, nq), lambda i: (0, 0)),
                  pl.BlockSpec((d, nz), lambda i: (0, 0))],
        out_specs=[pl.BlockSpec((tm, 2 * wa), lambda i: (i, 0)), feat_major, feat_major,
                   pl.BlockSpec((tm, nz), lambda i: (i, 0))],
        out_shape=[jax.ShapeDtypeStruct((t, 2 * wa), F32), jax.ShapeDtypeStruct((b, wa, s), F32),
                   jax.ShapeDtypeStruct((b, wa, s), F32), jax.ShapeDtypeStruct((t, nz), F32)],
        compiler_params=_cparams(1),
    )(x2d, g, wqkv, wz)


def _moba_prompt_kernel(q_ref, k_ref, vt_ref, tt_ref, o_ref,
                        kbf_s, vt_s, kmean_s, m_s, l_s, acc_s, sel_s, qbf_s):
    blk = MOBA_BLOCK
    i = pl.program_id(1)
    nb = kbf_s.shape[0]
    n_heads = tt_ref.shape[0]
    scale = HEAD_DIM ** -0.5 * LOG2_E

    @pl.when(i == 0)
    def _():
        for n in range(nb):
            kb = k_ref[0, n * blk:(n + 1) * blk, :]
            kbf_s[n] = kb.astype(BF16)
            kmean_s[n:n + 1, :] = jnp.mean(kb, axis=0, keepdims=True)
            vt_s[n] = vt_ref[0, :, n * blk:(n + 1) * blk].astype(BF16)

    q_t = q_ref[0].T
    blk_row = lax.broadcasted_iota(jnp.int32, (nb, blk), 0)
    key_idx = lax.broadcasted_iota(jnp.int32, (blk, blk), 0)
    qry_idx = lax.broadcasted_iota(jnp.int32, (blk, blk), 1)
    causal = key_idx <= qry_idx
    upper_half = lax.broadcasted_iota(jnp.int32, (LANES, blk), 0) >= HEAD_DIM

    for h in range(n_heads):
        lanes = slice((h // 2) * LANES, (h // 2 + 1) * LANES)
        q_h = jnp.where(upper_half == (h % 2 == 1), q_t[lanes, :], 0.0)
        gate = jnp.dot(kmean_s[:, lanes], q_h, precision=HIGHEST, preferred_element_type=F32)
        gate = jnp.where(blk_row < i, gate, NEG)
        rank = jnp.zeros((nb, blk), jnp.int32)
        for m in range(nb):
            gm = gate[m:m + 1, :]
            beats = (gm > gate) | ((gm == gate) & (m < blk_row))
            rank = rank + beats.astype(jnp.int32)
        sel_s[h] = jnp.where((rank < MOBA_TOPK) & (blk_row < i), 1.0, 0.0)
        qbf_s[h] = (q_h * scale).astype(BF16)
    m_s[...] = jnp.full(m_s.shape, NEG, F32)
    l_s[...] = jnp.zeros(l_s.shape, F32)
    acc_s[...] = jnp.zeros(acc_s.shape, F32)

    def attend(group, n, bias_of, mask_of, uniform_bias=False):
        scores = [jnp.dot(kbf_s[n, :, (h // 2) * LANES:(h // 2 + 1) * LANES], qbf_s[h],
                          preferred_element_type=F32) for h in group]
        probs, alphas = [], []
        for h, s in zip(group, scores):
            m_old = m_s[h:h + 1, :]
            if uniform_bias:
                bias = bias_of(h)
                s = jnp.where(mask_of(h), s, NEG)
                m_new = jnp.maximum(m_old, jnp.max(s, axis=0, keepdims=True) + bias)
                p = jnp.exp2(s - (m_new - bias))
            else:
                s = jnp.where(mask_of(h), s + bias_of(h), NEG)
                m_new = jnp.maximum(m_old, jnp.max(s, axis=0, keepdims=True))
                p = jnp.exp2(s - m_new)
            alpha = jnp.exp2(m_old - m_new)
            l_s[h:h + 1, :] = alpha * l_s[h:h + 1, :] + jnp.sum(p, axis=0, keepdims=True)
            m_s[h:h + 1, :] = m_new
            probs.append(p.astype(BF16))
            alphas.append(alpha)
        pvs = [jnp.dot(vt_s[n, h * HEAD_DIM:(h + 1) * HEAD_DIM, :], p, preferred_element_type=F32)
               for h, p in zip(group, probs)]
        for h, alpha, pv in zip(group, alphas, pvs):
            acc_s[h] = alpha * acc_s[h] + pv

    for g0 in range(0, n_heads, MOBA_HEAD_GROUP):
        group = range(g0, g0 + MOBA_HEAD_GROUP)
        attend(group, i, lambda h: tt_ref[h, :, blk:2 * blk], lambda h: causal)

        @pl.when(i >= 1)
        def _(group=group):
            n = i - 1
            attend(group, n, lambda h: tt_ref[h, :, 0:blk], lambda h: sel_s[h, pl.ds(n, 1), :] > 0.5)

        def far_body(n, carry, group=group):
            attend(group, n, lambda h: tt_ref[h, 0:1, blk - 1:blk],
                   lambda h: sel_s[h, pl.ds(n, 1), :] > 0.5, uniform_bias=True)
            return carry

        lax.fori_loop(0, jnp.maximum(i - 1, 0), far_body, 0)

    for pair in range(n_heads // 2):
        outs = [acc_s[h] / l_s[h:h + 1, :] for h in (2 * pair, 2 * pair + 1)]
        o_ref[0, :, pair * LANES:(pair + 1) * LANES] = jnp.concatenate(outs, axis=0).T


def _moba_prompt(qk3, vt3, tt):
    b, s, w2 = qk3.shape
    w = w2 // 2
    blk = MOBA_BLOCK
    nb = s // blk
    n_heads = w // HEAD_DIM
    return pl.pallas_call(
        _moba_prompt_kernel,
        grid=(b, nb),
        in_specs=[pl.BlockSpec((1, blk, w), lambda bi, i: (bi, i, 0)),
                  pl.BlockSpec((1, s, w), lambda bi, i: (bi, 0, 1)),
                  pl.BlockSpec((1, w, s), lambda bi, i: (bi, 0, 0)),
                  pl.BlockSpec((n_heads, blk, 2 * blk), lambda bi, i: (0, 0, 0))],
        out_specs=pl.BlockSpec((1, blk, w), lambda bi, i: (bi, i, 0)),
        out_shape=jax.ShapeDtypeStruct((b, s, w), F32),
        scratch_shapes=[pltpu.VMEM((nb, blk, w), BF16),
                        pltpu.VMEM((nb, w, blk), BF16),
                        pltpu.VMEM((nb, w), F32),
                        pltpu.VMEM((n_heads, blk), F32),
                        pltpu.VMEM((n_heads, blk), F32),
                        pltpu.VMEM((n_heads, HEAD_DIM, blk), F32),
                        pltpu.VMEM((n_heads, nb, blk), F32),
                        pltpu.VMEM((n_heads, LANES, blk), BF16)],
        compiler_params=_cparams(2),
    )(qk3, qk3, vt3, tt)


def _rel_bias_table(rel_bias, n_dist):
    n = jnp.arange(n_dist, dtype=jnp.int32)
    max_exact = N_BUCKETS // 2
    nf = jnp.maximum(n, 1).astype(F32)
    large = max_exact + (jnp.log(nf / max_exact) / math.log(MAX_DISTANCE / max_exact)
                         * (N_BUCKETS - max_exact)).astype(jnp.int32)
    large = jnp.minimum(large, N_BUCKETS - 1)
    bucket = jnp.where(n < max_exact, n, large)
    return rel_bias.astype(F32).T[:, bucket]


def _bias_tiles(tab):
    blk = MOBA_BLOCK
    period = 2 * blk
    n_heads = tab.shape[0]
    base = jnp.concatenate([tab[:, blk:period], tab[:, 0:blk]], axis=1)
    stride = 2 * period - 1
    reps = -(-(blk * stride) // period)
    seq = jnp.tile(base, (1, reps))[:, :blk * stride]
    return seq.reshape(n_heads, blk, stride)[:, :, :period]


def _mm3(a, b):
    a_hi = a.astype(BF16)
    b_hi = b.astype(BF16)
    a_lo = (a - a_hi.astype(F32)).astype(BF16)
    b_lo = (b - b_hi.astype(F32)).astype(BF16)
    return (jnp.dot(a_hi, b_hi, preferred_element_type=F32)
            + jnp.dot(a_hi, b_lo, preferred_element_type=F32)
            + jnp.dot(a_lo, b_hi, preferred_element_type=F32))


def _rwkv_prompt_kernel(zr_ref, zk_ref, zv_ref, zl_ref,
                        mur_ref, muk_ref, muv_ref, mul_ref,
                        w0_ref, w2_ref, a0_ref, a2_ref, g2_ref,
                        kk_ref, ka_ref, rk_ref, lg_ref, lb_ref,
                        rw_ref, st_ref,
                        p_s, last_r, last_k, last_v, last_l):
    t = pl.program_id(2)
    tc = zr_ref.shape[1]
    w = MXU_DIM
    c_len = RWKV_CHUNK
    hd_shift = HEAD_DIM.bit_length() - 1

    @pl.when(t == 0)
    def _():
        p_s[...] = jnp.zeros(p_s.shape, F32)
        last_r[...] = jnp.zeros(last_r.shape, F32)
        last_k[...] = jnp.zeros(last_k.shape, F32)
        last_v[...] = jnp.zeros(last_v.shape, F32)
        last_l[...] = jnp.zeros(last_l.shape, F32)

    def token_shift_mix(z_ref, last_ref, mu_ref):
        z = z_ref[0]
        row = lax.broadcasted_iota(jnp.int32, z.shape, 0)
        prev = jnp.where(row == 0, last_ref[...], pltpu.roll(z, 1, 0))
        last_ref[...] = z[tc - 1:tc, :]
        return z + (prev - z) * mu_ref[...]

    ri = lax.broadcasted_iota(jnp.int32, (w, w), 0)
    ci = lax.broadcasted_iota(jnp.int32, (w, w), 1)
    same_head = (ri >> hd_shift) == (ci >> hd_shift)
    tok_r = ri & (HEAD_DIM - 1)
    tok_c = ci & (HEAD_DIM - 1)
    strict = same_head & (tok_r > tok_c)
    incl = same_head & (tok_r >= tok_c)
    eye = ri == ci
    head_ones = jnp.where(same_head, 1.0, 0.0).astype(BF16)
    tri = jnp.where(lax.broadcasted_iota(jnp.int32, (c_len, c_len), 0)
                    >= lax.broadcasted_iota(jnp.int32, (c_len, c_len), 1), 1.0, 0.0).astype(BF16)

    def head_sum(x):
        return _mm_split_rhs(x, head_ones, 2)

    def stack(x):
        return jnp.where(same_head, jnp.concatenate([x] * HEADS_PER_TILE, axis=0), 0.0)

    def unstack(x):
        out = x[0:c_len]
        for e in range(1, HEADS_PER_TILE):
            out = out + x[e * c_len:(e + 1) * c_len]
        return out

    r = token_shift_mix(zr_ref, last_r, mur_ref)
    k = token_shift_mix(zk_ref, last_k, muk_ref)
    v = token_shift_mix(zv_ref, last_v, muv_ref)
    zl = token_shift_mix(zl_ref, last_l, mul_ref)

    w_pre = w0_ref[...] + _mm(jnp.tanh(zl), w2_ref[...], False)
    logd = -jnp.exp(-_softplus(-w_pre) - 0.5)
    eta = _sigmoid(a0_ref[...] + _mm(zl, a2_ref[...], False))
    gate = _mm(_sigmoid(zl), g2_ref[...], False)
    kk = k * kk_ref[...]
    kk = kk * lax.rsqrt(jnp.maximum(head_sum(kk * kk), 1e-24))
    kh = k * (1.0 + (eta - 1.0) * ka_ref[...])
    bonus = head_sum(r * kh * rk_ref[...]) * v
    kb = kk * eta

    chunks = range(tc // c_len)
    each = lambda f, *lists: [f(*(x[c] for x in lists)) for c in chunks]
    rows = [slice(c * c_len, (c + 1) * c_len) for c in chunks]
    ld = [logd[rw] for rw in rows]
    cum = each(lambda x: _mm_split(tri, x, 3), ld)
    cum_end = each(lambda x: x[c_len - 1:c_len, :], cum)
    e_inv = each(lambda x: jnp.exp(-x), cum)
    e_rem = each(lambda x, xe: jnp.exp(xe - x), cum, cum_end)
    a_t = [stack(-kk[rows[c]] * jnp.exp(cum[c] - ld[c])) for c in chunks]
    b_t = [stack(kb[rows[c]] * e_inv[c]) for c in chunks]
    k_t = [stack(kh[rows[c]] * e_inv[c]) for c in chunks]
    r_t = [stack(r[rows[c]] * jnp.exp(cum[c])) for c in chunks]
    b_h = [stack(kb[rows[c]] * e_rem[c]) for c in chunks]
    k_h = [stack(kh[rows[c]] * e_rem[c]) for c in chunks]
    v_s = [stack(v[rows[c]]) for c in chunks]

    prod = each(lambda a, rr, bb, kt: _mm_nt(jnp.concatenate([a, rr], axis=0),
                                             jnp.concatenate([bb, kt], axis=0)), a_t, r_t, b_t, k_t)
    n_mat = each(lambda x: jnp.where(strict, x[0:w, 0:w], 0.0), prod)
    a_k = each(lambda x: jnp.where(strict, x[0:w, w:2 * w], 0.0), prod)
    m_rb = each(lambda x: jnp.where(incl, x[w:2 * w, 0:w], 0.0), prod)
    m_rk = each(lambda x: jnp.where(incl, x[w:2 * w, w:2 * w], 0.0), prod)

    t_inv = each(lambda x: jnp.where(eye, 1.0, 0.0) + x, n_mat)
    n_pow = n_mat
    for _ in range(c_len.bit_length() - 2):
        n_pow = each(lambda x: _mm(x, x, False), n_pow)
        t_inv = each(lambda x, y: x + _mm(x, y, False), t_inv, n_pow)

    a_p = each(lambda x, y: _mm(x, y, False), t_inv, a_t)
    a_v = each(lambda x, y: _mm(x, y, False), a_k, v_s)
    v_p = each(lambda x, y: _mm(x, y, False), t_inv, a_v)
    r_p = each(lambda rr, m, ap: unstack(rr + _mm(m, ap, False)), r_t, m_rb, a_p)
    y_0 = each(lambda mb, vp, mk, vs: unstack(_mm(mb, vp, False) + _mm(mk, vs, False)),
               m_rb, v_p, m_rk, v_s)
    decay = each(jnp.exp, cum_end)
    mix = each(_mm_tn, a_p, b_h)
    add = each(lambda vp, bh, vs, kh_: _mm_tn(vp, bh) + _mm_tn(vs, kh_), v_p, b_h, v_s, k_h)

    st = p_s[...]
    ys = []
    for c in chunks:
        ys.append(_mm_nt(r_p[c], st) + y_0[c])
        st = st * decay[c] + _mm(st, mix[c], False) + add[c]
    p_s[...] = st
    st_ref[0, 0] = st

    y = jnp.concatenate(ys, axis=0)
    mean = head_sum(y) * (1.0 / HEAD_DIM)
    yc = y - mean
    var = head_sum(yc * yc) * (1.0 / HEAD_DIM)
    yn = yc * lax.rsqrt(var + GN_EPS) * lg_ref[...] + lb_ref[...]
    rw_ref[0] = (yn + bonus) * gate


def _rwkv_prompt(z3, mu, w0, w2p, a0, a2p, g2p, kk, ka, rk, lg, lb, *, tc):
    b, s, _ = z3.shape
    w = MXU_DIM
    wr = w0.shape[1]
    nt = wr // w
    lw = w2p.shape[0]
    lora_blk = (3 * wr) // lw
    vec = lambda off: pl.BlockSpec((1, w), lambda bi, q, t, off=off: (0, off + q))
    zblk = lambda off: pl.BlockSpec((1, tc, w), lambda bi, q, t, off=off: (bi, t, off + q))
    lora_w = pl.BlockSpec((lw, w), lambda bi, q, t: (0, q))
    return pl.pallas_call(
        _rwkv_prompt_kernel,
        grid=(b, nt, s // tc),
        in_specs=[zblk(0), zblk(nt), zblk(2 * nt),
                  pl.BlockSpec((1, tc, lw), lambda bi, q, t: (bi, t, lora_blk)),
                  vec(0), vec(nt), vec(2 * nt),
                  pl.BlockSpec((1, lw), lambda bi, q, t: (0, lora_blk)),
                  vec(0), lora_w, vec(0), lora_w, lora_w,
                  vec(0), vec(0), vec(0), vec(0), vec(0)],
        out_specs=[pl.BlockSpec((1, tc, w), lambda bi, q, t: (bi, t, q)),
                   pl.BlockSpec((1, 1, w, w), lambda bi, q, t: (bi, q, 0, 0))],
        out_shape=[jax.ShapeDtypeStruct((b, s, wr), F32),
                   jax.ShapeDtypeStruct((b, nt, w, w), F32)],
        scratch_shapes=[pltpu.VMEM((w, w), F32),
                        pltpu.VMEM((1, w), F32), pltpu.VMEM((1, w), F32),
                        pltpu.VMEM((1, w), F32), pltpu.VMEM((1, lw), F32)],
        compiler_params=_cparams(3),
    )(z3, z3, z3, z3, mu, mu, mu, mu, w0, w2p, a0, a2p, g2p, kk, ka, rk, lg, lb)


def _state_from_tiles(st):
    b, nt = st.shape[:2]
    heads = []
    for q in range(nt):
        for e in range(HEADS_PER_TILE):
            sl = slice(e * HEAD_DIM, (e + 1) * HEAD_DIM)
            heads.append(st[:, q, sl, sl])
    return jnp.stack(heads, axis=1)


def _lora_padded(w2, a2, g2):
    lw = 3 * LANES
    n = w2.shape[1]
    o_a = LORA_W
    o_g = LORA_W + LORA_A
    w2p = jnp.zeros((lw, n), F32).at[0:o_a].set(w2)
    a2p = jnp.zeros((lw, n), F32).at[o_a:o_g].set(a2)
    g2p = jnp.zeros((lw, n), F32).at[o_g:o_g + LORA_G].set(g2)
    return w2p, a2p, g2p


ROUTER_LANE0 = N_GROUPS


def _first_argmax(x, lane):
    mx = jnp.max(x, axis=1, keepdims=True)
    idx = jnp.min(jnp.where(x == mx, lane, LANES), axis=1, keepdims=True)
    return mx, idx


def _outproj_router_kernel(x_ref, att_ref, rw_ref, wo_ref, g_ref, wr_ref, br_ref,
                           x1_ref, h2_ref, comb_ref, *, precise):
    wa = att_ref.shape[1]
    x1 = (x_ref[...] + _mm(att_ref[...], wo_ref[0:wa, :], precise)
          + _mm(rw_ref[...], wo_ref[wa:, :], precise))
    x1_ref[...] = x1
    h2 = _rms_norm(x1, g_ref[...])
    h2_ref[...] = h2.astype(h2_ref.dtype)
    if precise:
        logits = _mm(h2, wr_ref[...], True) + br_ref[...]
    else:
        logits = _mm3(h2, wr_ref[...]) + br_ref[...]
    n_exp = N_GROUPS * EXPERTS_PER_GROUP
    lane = lax.broadcasted_iota(jnp.int32, logits.shape, 1)
    is_grp = lane < N_GROUPS
    gl = jnp.where(is_grp, logits, NEG)
    g_max, grp = _first_argmax(gl, lane)
    p_grp = 1.0 / jnp.sum(jnp.where(is_grp, jnp.exp(gl - g_max), 0.0), axis=1, keepdims=True)
    e_lane = lane - ROUTER_LANE0
    grp_shift = EXPERTS_PER_GROUP.bit_length() - 1
    in_grp = (e_lane >= 0) & (e_lane < n_exp) & ((e_lane >> grp_shift) == grp)
    el = jnp.where(in_grp, logits, NEG)
    m1, i1 = _first_argmax(el, lane)
    el2 = jnp.where(lane == i1, NEG, el)
    m2, i2 = _first_argmax(el2, lane)
    ratio = jnp.exp(m2 - m1)
    p1 = 1.0 / (1.0 + ratio)
    p2 = ratio / (1.0 + ratio)
    comb_ref[...] = p_grp * (jnp.where(lane == i1, p1, 0.0) + jnp.where(lane == i2, p2, 0.0))


def _outproj_router(x2d, att, rw, wo, g, wr, br, *, tm, precise):
    t, d = x2d.shape
    wa, wr_cols = att.shape[1], rw.shape[1]
    h_dtype = F32 if precise else BF16
    tok = lambda n: pl.BlockSpec((tm, n), lambda i: (i, 0))
    full = lambda a: pl.BlockSpec(a.shape, lambda i: (0,) * a.ndim)
    return pl.pallas_call(
        functools.partial(_outproj_router_kernel, precise=precise),
        grid=(t // tm,),
        in_specs=[tok(d), tok(wa), tok(wr_cols), full(wo), full(g), full(wr), full(br)],
        out_specs=[tok(d), tok(d), tok(LANES)],
        out_shape=[jax.ShapeDtypeStruct((t, d), F32), jax.ShapeDtypeStruct((t, d), h_dtype),
                   jax.ShapeDtypeStruct((t, LANES), F32)],
        compiler_params=_cparams(1),
    )(x2d, att, rw, wo, g, wr, br)


def _moe_kernel(h2_ref, comb_ref, x1_ref, weg_ref, weu_ref, wed_ref, x2_ref, *, precise):
    e = pl.program_id(1)

    @pl.when(e == 0)
    def _():
        x2_ref[...] = x1_ref[...]

    h2 = h2_ref[...]
    lane = lax.broadcasted_iota(jnp.int32, comb_ref.shape, 1)
    weight = jnp.sum(jnp.where(lane == e + ROUTER_LANE0, comb_ref[...], 0.0), axis=1, keepdims=True)
    hg = _mm(h2, weg_ref[0], precise)
    hu = _mm(h2, weu_ref[0], precise)
    act = hg * _sigmoid(hg) * hu * weight
    x2_ref[...] += _mm(act, wed_ref[0], precise)


def _moe(h2, comb, x1, weg, weu, wed, *, tm, precise):
    t, d = x1.shape
    n_exp, _, f = weg.shape
    return pl.pallas_call(
        functools.partial(_moe_kernel, precise=precise),
        grid=(t // tm, n_exp),
        in_specs=[pl.BlockSpec((tm, d), lambda i, e: (i, 0)),
                  pl.BlockSpec((tm, LANES), lambda i, e: (i, 0)),
                  pl.BlockSpec((tm, d), lambda i, e: (i, 0)),
                  pl.BlockSpec((1, d, f), lambda i, e: (e, 0, 0)),
                  pl.BlockSpec((1, d, f), lambda i, e: (e, 0, 0)),
                  pl.BlockSpec((1, f, d), lambda i, e: (e, 0, 0))],
        out_specs=pl.BlockSpec((tm, d), lambda i, e: (i, 0)),
        out_shape=jax.ShapeDtypeStruct((t, d), F32),
        compiler_params=_cparams(2),
    )(h2, comb, x1, weg, weu, wed)


def _ple_kernel(x_ref, p_ref, wple_ref, wgate_ref, fg_ref, y_ref, *, precise, final):
    x = x_ref[...]
    emb = _mm(p_ref[...], wple_ref[...], precise)
    x = x + emb * _sigmoid(_mm(x, wgate_ref[...], precise))
    y_ref[...] = _rms_norm(x, fg_ref[...]) if final else x


def _ple(x2d, p2d, wple, wgate, fg, *, tm, precise, final):
    t, d = x2d.shape
    pd = p2d.shape[1]
    full = lambda a: pl.BlockSpec(a.shape, lambda i: (0,) * a.ndim)
    return pl.pallas_call(
        functools.partial(_ple_kernel, precise=precise, final=final),
        grid=(t // tm,),
        in_specs=[pl.BlockSpec((tm, d), lambda i: (i, 0)), pl.BlockSpec((tm, pd), lambda i: (i, 0)),
                  full(wple), full(wgate), full(fg)],
        out_specs=pl.BlockSpec((tm, d), lambda i: (i, 0)),
        out_shape=jax.ShapeDtypeStruct((t, d), F32),
        compiler_params=_cparams(1),
    )(x2d, p2d, wple, wgate, fg)


def _rwkv_sample_prep_kernel(z_ref, zp_ref, mu_ref, w0_ref, w2_ref, a0_ref, a2_ref, g2_ref,
                             kk_ref, ka_ref,
                             r_ref, d_ref, kkn_ref, kb_ref, kh_ref, v_ref, g_ref):
    wr = w0_ref.shape[1]
    hd_shift = HEAD_DIM.bit_length() - 1
    z = z_ref[...]
    zs = z + (zp_ref[...] - z) * mu_ref[...]
    r = zs[:, 0:wr]
    k = zs[:, wr:2 * wr]
    v = zs[:, 2 * wr:3 * wr]
    zl = zs[:, 3 * wr:]
    w_pre = w0_ref[...] + _mm(jnp.tanh(zl), w2_ref[...], True)
    decay = jnp.exp(-jnp.exp(-_softplus(-w_pre) - 0.5))
    eta = _sigmoid(a0_ref[...] + _mm(zl, a2_ref[...], True))
    ri = lax.broadcasted_iota(jnp.int32, (wr, wr), 0)
    ci = lax.broadcasted_iota(jnp.int32, (wr, wr), 1)
    head_ones = jnp.where((ri >> hd_shift) == (ci >> hd_shift), 1.0, 0.0)
    kk = k * kk_ref[...]
    kk = kk * lax.rsqrt(jnp.maximum(_mm(kk * kk, head_ones, True), 1e-24))
    r_ref[...] = r
    d_ref[...] = decay
    kkn_ref[...] = kk
    kb_ref[...] = kk * eta
    kh_ref[...] = k * (1.0 + (eta - 1.0) * ka_ref[...])
    v_ref[...] = v
    g_ref[...] = _mm(_sigmoid(zl), g2_ref[...], True)


def _rwkv_sample_prep(z, zp, mu, w0, w2p, a0, a2p, g2p, kk, ka):
    n = z.shape[0]
    wr = w0.shape[1]
    args = (z, zp, mu, w0, w2p, a0, a2p, g2p, kk, ka)
    full = lambda a: pl.BlockSpec(a.shape, lambda i: (0,) * a.ndim)
    return pl.pallas_call(
        _rwkv_sample_prep_kernel,
        grid=(1,),
        in_specs=[full(a) for a in args],
        out_specs=[pl.BlockSpec((n, wr), lambda i: (0, 0))] * 7,
        out_shape=[jax.ShapeDtypeStruct((n, wr), F32)] * 7,
        compiler_params=_cparams(1),
    )(*args)


def _rwkv_sample_step_kernel(s_ref, r_ref, d_ref, kkn_ref, kb_ref, kh_ref, v_ref, g_ref,
                             rk_ref, lg_ref, lb_ref, so_ref, rw_ref):
    n_heads = s_ref.shape[1]
    eye = (lax.broadcasted_iota(jnp.int32, (HEAD_DIM, HEAD_DIM), 0)
           == lax.broadcasted_iota(jnp.int32, (HEAD_DIM, HEAD_DIM), 1))
    for h in range(n_heads):
        hrow = lambda ref, lead=True: ref[0, h:h + 1, :] if lead else ref[h:h + 1, :]
        st = s_ref[0, h]
        r, v, kh = hrow(r_ref), hrow(v_ref), hrow(kh_ref)
        sa = jnp.sum(st * (-hrow(kkn_ref)), axis=1, keepdims=True)
        v_col = jnp.sum(jnp.where(eye, v, 0.0), axis=1, keepdims=True)
        st = st * hrow(d_ref) + sa * hrow(kb_ref) + v_col * kh
        so_ref[0, h] = st
        y_col = jnp.sum(st * r, axis=1, keepdims=True)
        y = jnp.sum(jnp.where(eye, y_col, 0.0), axis=0, keepdims=True)
        mean = jnp.mean(y, axis=1, keepdims=True)
        yc = y - mean
        var = jnp.mean(yc * yc, axis=1, keepdims=True)
        yn = yc * lax.rsqrt(var + GN_EPS) * hrow(lg_ref, False) + hrow(lb_ref, False)
        bonus = jnp.sum(r * kh * hrow(rk_ref, False), axis=1, keepdims=True) * v
        rw_ref[0, h:h + 1, :] = (yn + bonus) * hrow(g_ref)


def _rwkv_sample_step(state, per_head, rk, lg, lb):
    n, nh = state.shape[:2]
    tok = pl.BlockSpec((1, nh, HEAD_DIM), lambda i: (i, 0, 0))
    par = pl.BlockSpec((nh, HEAD_DIM), lambda i: (0, 0))
    st = pl.BlockSpec((1, nh, HEAD_DIM, HEAD_DIM), lambda i: (i, 0, 0, 0))
    return pl.pallas_call(
        _rwkv_sample_step_kernel,
        grid=(n,),
        in_specs=[st] + [tok] * 7 + [par] * 3,
        out_specs=[st, tok],
        out_shape=[jax.ShapeDtypeStruct(state.shape, F32),
                   jax.ShapeDtypeStruct((n, nh, HEAD_DIM), F32)],
        compiler_params=_cparams(1),
    )(state, *per_head, rk, lg, lb)


PAGES_PER_STEP = 32


def _moba_sample_score_kernel(pt_ref, qc_ref, *refs):
    pps = PAGES_PER_STEP
    k_refs, s_ref, gate_ref = refs[:pps], refs[pps], refs[pps + 1]
    n_heads, _, page = k_refs[0].shape[1:]
    pages_per_blk = MOBA_BLOCK // page
    blks_per_step = pps // pages_per_blk
    j = pl.program_id(1)
    qc = qc_ref[0]
    lane = lax.broadcasted_iota(jnp.int32, (n_heads, LANES), 1)

    @pl.when(j == 0)
    def _():
        gate_ref[0] = jnp.zeros(gate_ref.shape[1:], F32)

    gate = gate_ref[0]
    for blk_i in range(blks_per_step):
        blk_sum = jnp.zeros((n_heads, page), F32)
        for half in range(pages_per_blk):
            i = blk_i * pages_per_blk + half
            s = jnp.sum(k_refs[i][0] * qc, axis=1)
            s_ref[0, :, i * page:(i + 1) * page] = s
            blk_sum = blk_sum + s
        g_col = jnp.sum(blk_sum, axis=1, keepdims=True) * (1.0 / MOBA_BLOCK)
        gate = jnp.where(lane == j * blks_per_step + blk_i, g_col, gate)
    gate_ref[0] = gate


def _moba_sample_scores(page_table, q_cols, cache_t):
    n, n_pages = page_table.shape
    n_heads, hd, page = cache_t.shape[1:]
    pps = PAGES_PER_STEP
    k_spec = lambda i: pl.BlockSpec((1, n_heads, hd, page),
                                    lambda b, j, pt, i=i: (pt[b, j * pps + i], 0, 0, 0))
    grid_spec = pltpu.PrefetchScalarGridSpec(
        num_scalar_prefetch=1,
        grid=(n, n_pages // pps),
        in_specs=[pl.BlockSpec((1, n_heads, hd, page), lambda b, j, pt: (b, 0, 0, 0))]
                 + [k_spec(i) for i in range(pps)],
        out_specs=[pl.BlockSpec((1, n_heads, pps * page), lambda b, j, pt: (b, 0, j)),
                   pl.BlockSpec((1, n_heads, LANES), lambda b, j, pt: (b, 0, 0))])
    return pl.pallas_call(
        _moba_sample_score_kernel,
        grid_spec=grid_spec,
        out_shape=[jax.ShapeDtypeStruct((n, n_heads, n_pages * page), F32),
                   jax.ShapeDtypeStruct((n, n_heads, LANES), F32)],
        compiler_params=_cparams(2),
    )(page_table, q_cols, *([cache_t] * pps))


def _moba_sample_select_kernel(s_ref, gate_ref, bias_ref, q_ref, kn_ref, b0_ref,
                               p_ref, pown_ref, sel_ref):
    n_heads, n_keys = s_ref.shape[1:]
    blk_shift = MOBA_BLOCK.bit_length() - 1
    n_blk = n_keys // MOBA_BLOCK
    scale = HEAD_DIM ** -0.5
    s = s_ref[0]
    member = jnp.where(lax.broadcasted_iota(jnp.int32, (LANES, n_keys), 1) >> blk_shift
                       == lax.broadcasted_iota(jnp.int32, (LANES, n_keys), 0), 1.0, 0.0).astype(BF16)
    lane = lax.broadcasted_iota(jnp.int32, (n_heads, LANES), 1)
    gate = jnp.where(lane < n_blk, gate_ref[0], NEG)
    sel = jnp.zeros((n_heads, LANES), F32)
    for _ in range(MOBA_TOPK):
        _, pick = _first_argmax(gate, lane)
        hit = lane == pick
        sel = jnp.where(hit, 1.0, sel)
        gate = jnp.where(hit, NEG, gate)
    sel_ref[0] = sel
    key_sel = _mm(sel, member, False)
    sc = jnp.where(key_sel > 0.5, s * scale + bias_ref[...], NEG)
    s_own = jnp.sum(q_ref[0] * kn_ref[0], axis=1, keepdims=True) * scale + b0_ref[...]
    m = jnp.maximum(jnp.max(sc, axis=1, keepdims=True), s_own)
    p = jnp.exp(sc - m)
    p_own = jnp.exp(s_own - m)
    inv = 1.0 / (jnp.sum(p, axis=1, keepdims=True) + p_own)
    p_ref[0] = p * inv
    pown_ref[0] = jnp.broadcast_to(p_own * inv, (n_heads, LANES))


def _moba_sample_select(scores, gate, bias_keys, q3, kn3, bias0):
    n, n_heads, n_keys = scores.shape
    hd = q3.shape[2]
    tok = pl.BlockSpec((1, n_heads, hd), lambda b: (b, 0, 0))
    keys = pl.BlockSpec((1, n_heads, n_keys), lambda b: (b, 0, 0))
    lane_blk = pl.BlockSpec((1, n_heads, LANES), lambda b: (b, 0, 0))
    return pl.pallas_call(
        _moba_sample_select_kernel,
        grid=(n,),
        in_specs=[keys, lane_blk, pl.BlockSpec((n_heads, n_keys), lambda b: (0, 0)), tok, tok,
                  pl.BlockSpec((n_heads, 1), lambda b: (0, 0))],
        out_specs=[keys, lane_blk, lane_blk],
        out_shape=[jax.ShapeDtypeStruct((n, n_heads, n_keys), F32),
                   jax.ShapeDtypeStruct((n, n_heads, LANES), F32),
                   jax.ShapeDtypeStruct((n, n_heads, LANES), F32)],
        compiler_params=_cparams(1),
    )(scores, gate, bias_keys, q3, kn3, bias0)


def _pages_to_fetch(page_table, sel, pages_per_blk):
    n, n_pages = page_table.shape
    pps = PAGES_PER_STEP
    steps = n * n_pages // pps
    n_blk = n_pages // pages_per_blk
    blk_needed = jnp.max(sel[:, :, :n_blk], axis=1) > 0.0
    needed = jnp.repeat(blk_needed, pages_per_blk, axis=1)
    step_id = jnp.arange(steps, dtype=jnp.int32)[:, None]
    last_needed = lax.cummax(jnp.where(needed.reshape(steps, pps), step_id, 0), axis=0)
    pages = jnp.take_along_axis(page_table.reshape(steps, pps), last_needed, axis=0)
    return pages.reshape(n, n_pages), blk_needed.astype(jnp.int32)


def _moba_sample_pv_kernel(pt_ref, need_ref, p_ref, pown_ref, vn_ref, *refs):
    pps = PAGES_PER_STEP
    v_refs, o_ref, acc_s = refs[:pps], refs[pps], refs[pps + 1]
    n_heads, hd, page = v_refs[0].shape[1:]
    pages_per_blk = MOBA_BLOCK // page
    blks_per_step = pps // pages_per_blk
    b = pl.program_id(0)
    j = pl.program_id(1)

    @pl.when(j == 0)
    def _():
        acc_s[...] = jnp.zeros(acc_s.shape, F32)

    for blk_i in range(blks_per_step):
        @pl.when(need_ref[b, j * blks_per_step + blk_i] > 0)
        def _(blk_i=blk_i):
            acc = acc_s[...]
            for half in range(pages_per_blk):
                i = blk_i * pages_per_blk + half
                w = p_ref[0, :, i * page:(i + 1) * page][:, None, :]
                acc = acc + jnp.where(w > 0.0, v_refs[i][0] * w, 0.0)
            acc_s[...] = acc

    @pl.when(j == pl.num_programs(1) - 1)
    def _():
        eye = (lax.broadcasted_iota(jnp.int32, (hd, hd), 0)
               == lax.broadcasted_iota(jnp.int32, (hd, hd), 1))
        for h in range(n_heads):
            col = jnp.sum(acc_s[h], axis=1, keepdims=True)
            row = jnp.sum(jnp.where(eye, col, 0.0), axis=0, keepdims=True)
            o_ref[0, h:h + 1, :] = row + pown_ref[0, h:h + 1, 0:hd] * vn_ref[0, h:h + 1, :]


def _moba_sample_pv(pages, blk_needed, probs, p_own, vn3, cache_t):
    n, n_pages = pages.shape
    n_heads, hd, page = cache_t.shape[1:]
    pps = PAGES_PER_STEP
    v_spec = lambda i: pl.BlockSpec((1, n_heads, hd, page),
                                    lambda b, j, pt, need, i=i: (pt[b, j * pps + i], 0, 0, 0))
    tok = pl.BlockSpec((1, n_heads, hd), lambda b, j, pt, need: (b, 0, 0))
    grid_spec = pltpu.PrefetchScalarGridSpec(
        num_scalar_prefetch=2,
        grid=(n, n_pages // pps),
        in_specs=[pl.BlockSpec((1, n_heads, pps * page), lambda b, j, pt, need: (b, 0, j)),
                  pl.BlockSpec((1, n_heads, LANES), lambda b, j, pt, need: (b, 0, 0)), tok]
                 + [v_spec(i) for i in range(pps)],
        out_specs=tok,
        scratch_shapes=[pltpu.VMEM((n_heads, hd, page), F32)])
    return pl.pallas_call(
        _moba_sample_pv_kernel,
        grid_spec=grid_spec,
        out_shape=jax.ShapeDtypeStruct((n, n_heads, hd), F32),
        compiler_params=_cparams(2),
    )(pages, blk_needed, probs, p_own, vn3, *([cache_t] * pps))


def _sample_key_bias(tab, past_len):
    n_heads, n_dist = tab.shape
    assert past_len >= n_dist
    tab_x = jnp.concatenate([tab, tab[:, -1:]], axis=1)
    near = jnp.flip(tab_x[:, 1:n_dist + 1], axis=1)
    far = jnp.broadcast_to(tab[:, -1:], (n_heads, past_len - n_dist))
    return jnp.concatenate([far, near], axis=1)


def _moba_sample(q, k_new, v_new, cache_k_l, cache_v_l, page_table, tab):
    n, width = q.shape
    n_pool, page, n_heads, hd = cache_k_l.shape
    n_pages = page_table.shape[1]
    past_len = n_pages * page
    assert hd == HEAD_DIM and n_heads * hd == width and page == LANES
    assert past_len % MOBA_BLOCK == 0 and past_len // MOBA_BLOCK >= MOBA_TOPK
    assert n_pages % PAGES_PER_STEP == 0
    heads = lambda a: a.reshape(n, n_heads, hd)
    q3, kn3, vn3 = heads(q), heads(k_new), heads(v_new)
    ck = jnp.transpose(cache_k_l, (0, 2, 3, 1))
    cv = jnp.transpose(cache_v_l, (0, 2, 3, 1))
    q_cols = jnp.broadcast_to(q3[..., None], (n, n_heads, hd, page))
    assert PAGES_PER_STEP % (MOBA_BLOCK // page) == 0
    scores, gate = _moba_sample_scores(page_table, q_cols, ck)
    probs, p_own, sel = _moba_sample_select(scores, gate, _sample_key_bias(tab, past_len), q3, kn3,
                                            tab[:, 0:1])
    v_pages, blk_needed = _pages_to_fetch(page_table, sel, MOBA_BLOCK // page)
    return _moba_sample_pv(v_pages, blk_needed, probs, p_own, vn3, cv).reshape(n, width)


def _layer_weights(l, precise, norm1_g, w_in, shift_mu, w0, w2, a0, a2, g2, k_k, k_a, r_k,
                   lnx_g, lnx_b, w_out, norm2_g, w_rg, b_rg, w_re, b_re, w_eg, w_eu, w_ed,
                   w_ple, w_pleg):
    wd = F32 if precise else BF16
    row = lambda x: x.reshape(1, -1).astype(F32)
    wa3 = 3 * (w_out.shape[1] - w0.shape[1])
    rw_cols = w_in.shape[2] - wa3
    z_pad = -rw_cols % LANES
    w2p, a2p, g2p = _lora_padded(w2[l], a2[l], g2[l])
    n_rt = N_GROUPS + N_GROUPS * EXPERTS_PER_GROUP
    w_router = jnp.pad(jnp.concatenate([w_rg[l], w_re[l]], axis=1), ((0, 0), (0, LANES - n_rt)))
    b_router = jnp.pad(jnp.concatenate([b_rg[l], b_re[l]]), (0, LANES - n_rt))
    return dict(
        norm1=row(norm1_g[l]),
        wqkv=w_in[l][:, :wa3].astype(wd),
        wz=jnp.pad(w_in[l][:, wa3:], ((0, 0), (0, z_pad))).astype(wd),
        mu=jnp.pad(shift_mu[l], (0, z_pad))[None],
        w0=row(w0[l]), w2p=w2p, a0=row(a0[l]), a2p=a2p, g2p=g2p,
        kk=row(k_k[l]), ka=row(k_a[l]), rk=row(r_k[l]), lg=row(lnx_g[l]), lb=row(lnx_b[l]),
        wo=w_out[l].astype(wd), norm2=row(norm2_g[l]), w_router=w_router, b_router=b_router[None],
        weg=w_eg[l].astype(wd), weu=w_eu[l].astype(wd), wed=w_ed[l].astype(wd),
        wple=w_ple[l].astype(wd), wgate=w_pleg[l].astype(wd),
        rw_cols=rw_cols, wa=wa3 // 3)


def _post_attention(x2d, att, rw, p2d, w, fg, *, tm, precise, final):
    x1, h2, comb = _outproj_router(x2d, att, rw, w["wo"], w["norm2"], w["w_router"], w["b_router"],
                                   tm=tm, precise=precise)
    tm_moe = min(x2d.shape[0], MOE_TOKEN_TILE)
    x2 = _moe(h2, comb, x1, w["weg"], w["weu"], w["wed"], tm=tm_moe, precise=precise)
    return _ple(x2, p2d, w["wple"], w["wgate"], fg, tm=tm, precise=precise, final=final)


def kernel(x_prompt, x_sample, p_prompt, p_sample, cache_k, cache_v, page_table, state_wkv, state_shift, norm1_g, w_in, rel_bias, shift_mu, w0, w2, a0, a2, g2, k_k, k_a, r_k, lnx_g, lnx_b, w_out, norm2_g, w_rg, b_rg, w_re, b_re, w_eg, w_eu, w_ed, w_ple, w_pleg, final_g):
    depth = norm1_g.shape[0]
    b, s, d = x_prompt.shape
    n = x_sample.shape[0]
    assert x_sample.shape[1] == 1
    n_heads = state_wkv.shape[2]
    layer_params = (norm1_g, w_in, shift_mu, w0, w2, a0, a2, g2, k_k, k_a, r_k, lnx_g, lnx_b,
                    w_out, norm2_g, w_rg, b_rg, w_re, b_re, w_eg, w_eu, w_ed, w_ple, w_pleg)
    fg = final_g.reshape(1, -1)
    tab = _rel_bias_table(rel_bias, 2 * MOBA_BLOCK)
    bias_tiles = _bias_tiles(tab * LOG2_E)
    xp = x_prompt.reshape(b * s, d)
    xs = x_sample.reshape(n, d)
    outs = [[] for _ in range(8)]
    for l in range(depth):
        final = l == depth - 1
        w = _layer_weights(l, False, *layer_params)
        wa, rw_cols = w["wa"], w["rw_cols"]
        heads = lambda a, lead: a.reshape(lead + (wa // HEAD_DIM, HEAD_DIM))
        qk, k_t, v_t, z = _inproj_seq(xp, w["norm1"], w["wqkv"], w["wz"], b=b, tm=512)
        att = _moba_prompt(qk.reshape(b, s, 2 * wa), v_t, bias_tiles).reshape(b * s, wa)
        seq_major = lambda a: jnp.transpose(a.reshape(b, wa // HEAD_DIM, HEAD_DIM, s), (0, 3, 1, 2))
        z3 = z.reshape(b, s, -1)
        rw, st = _rwkv_prompt(z3, w["mu"], w["w0"], w["w2p"], w["a0"], w["a2p"], w["g2p"], w["kk"],
                              w["ka"], w["rk"], w["lg"], w["lb"], tc=RWKV_TOKEN_TILE)
        xp = _post_attention(xp, att, rw.reshape(b * s, -1), p_prompt[l].reshape(b * s, -1), w, fg,
                             tm=512, precise=False, final=final)
        outs[0].append(seq_major(k_t))
        outs[1].append(seq_major(v_t))
        outs[4].append(_state_from_tiles(st))
        outs[6].append(z3[:, -1, :rw_cols])
        w = _layer_weights(l, True, *layer_params)
        qkv, z = _inproj(xs, w["norm1"], w["wqkv"], w["wz"], tm=n, precise=True)
        q_s, k_s, v_s = qkv[:, :wa], qkv[:, wa:2 * wa], qkv[:, 2 * wa:]
        att = _moba_sample(q_s, k_s, v_s, cache_k[l], cache_v[l], page_table, tab)
        z_prev = jnp.pad(state_shift[l], ((0, 0), (0, z.shape[1] - rw_cols)))
        per_tok = _rwkv_sample_prep(z, z_prev, w["mu"], w["w0"], w["w2p"], w["a0"], w["a2p"], w["g2p"],
                                    w["kk"], w["ka"])
        per_head = [a.reshape(n, n_heads, HEAD_DIM) for a in per_tok]
        hp = lambda a: a.reshape(n_heads, HEAD_DIM)
        st_s, rw_s = _rwkv_sample_step(state_wkv[l], per_head, hp(w["rk"]), hp(w["lg"]), hp(w["lb"]))
        xs = _post_attention(xs, att, rw_s.reshape(n, -1), p_sample[l].reshape(n, -1), w, fg,
                             tm=n, precise=True, final=final)
        outs[2].append(heads(k_s, (n, 1)))
        outs[3].append(heads(v_s, (n, 1)))
        outs[5].append(st_s)
        outs[7].append(z[:, :rw_cols])
    kp, vp, ks, vs, wp, ws, sp, ss = [jnp.stack(o) for o in outs]
    return (xp.reshape(b, s, d), xs.reshape(n, 1, d), kp, vp, ks, vs, wp, ws, sp, ss)
```

```python
import functools
import math

import jax
import jax.numpy as jnp
from jax import lax
from jax.experimental import pallas as pl
from jax.experimental.pallas import tpu as pltpu

F32 = jnp.float32
BF16 = jnp.bfloat16
HIGHEST = lax.Precision.HIGHEST

HEAD_DIM = 64
MOBA_BLOCK = 256
MOBA_TOPK = 3
N_BUCKETS = 32
MAX_DISTANCE = 128
LORA_W = 64
LORA_A = 64
LORA_G = 160
N_GROUPS = 4
EXPERTS_PER_GROUP = 4
RMS_EPS = 1e-6
GN_EPS = 64e-5
NEG = -1e30
LOG2_E = math.log2(math.e)

LANES = 128
MXU_DIM = 256
VMEM_LIMIT_BYTES = 56 * 1024 * 1024

MOE_TOKEN_TILE = 1024
MOBA_HEAD_GROUP = 8
RWKV_CHUNK = 64
RWKV_TOKEN_TILE = 512
HEADS_PER_TILE = MXU_DIM // HEAD_DIM


def _cparams(n_axes):
    return pltpu.CompilerParams(dimension_semantics=("arbitrary",) * n_axes,
                                vmem_limit_bytes=VMEM_LIMIT_BYTES)


def _mm(a, b, precise):
    if precise:
        return jnp.dot(a.astype(F32), b.astype(F32), precision=HIGHEST, preferred_element_type=F32)
    return jnp.dot(a.astype(BF16), b.astype(BF16), preferred_element_type=F32)


def _mm_nt(a, b):
    return lax.dot_general(a.astype(BF16), b.astype(BF16), (((1,), (1,)), ((), ())),
                           preferred_element_type=F32)


def _mm_tn(a, b):
    return lax.dot_general(a.astype(BF16), b.astype(BF16), (((0,), (0,)), ((), ())),
                           preferred_element_type=F32)


def _mm_split(a_exact, x, terms):
    a16 = a_exact.astype(BF16)
    acc = None
    rem = x
    for _ in range(terms):
        piece = rem.astype(BF16)
        part = jnp.dot(a16, piece, preferred_element_type=F32)
        acc = part if acc is None else acc + part
        rem = rem - piece.astype(F32)
    return acc


def _mm_split_rhs(x, b_exact, terms):
    b16 = b_exact.astype(BF16)
    acc = None
    rem = x
    for _ in range(terms):
        piece = rem.astype(BF16)
        part = jnp.dot(piece, b16, preferred_element_type=F32)
        acc = part if acc is None else acc + part
        rem = rem - piece.astype(F32)
    return acc


def _sigmoid(x):
    return 1.0 / (1.0 + jnp.exp(-x))


def _softplus(x):
    return jnp.maximum(x, 0.0) + jnp.log(1.0 + jnp.exp(-jnp.abs(x)))


def _rms_norm(x, g):
    return x * lax.rsqrt(jnp.mean(x * x, axis=-1, keepdims=True) + RMS_EPS) * g


def _inproj_kernel(x_ref, g_ref, wqkv_ref, wz_ref, qkv_ref, z_ref, *, precise):
    h = _rms_norm(x_ref[...], g_ref[...])
    qkv_ref[...] = _mm(h, wqkv_ref[...], precise)
    z_ref[...] = _mm(h, wz_ref[...], precise)


def _inproj(x2d, g, wqkv, wz, *, tm, precise):
    t, d = x2d.shape
    nq, nz = wqkv.shape[1], wz.shape[1]
    return pl.pallas_call(
        functools.partial(_inproj_kernel, precise=precise),
        grid=(t // tm,),
        in_specs=[pl.BlockSpec((tm, d), lambda i: (i, 0)),
                  pl.BlockSpec((1, d), lambda i: (0, 0)),
                  pl.BlockSpec((d, nq), lambda i: (0, 0)),
                  pl.BlockSpec((d, nz), lambda i: (0, 0))],
        out_specs=[pl.BlockSpec((tm, nq), lambda i: (i, 0)),
                   pl.BlockSpec((tm, nz), lambda i: (i, 0))],
        out_shape=[jax.ShapeDtypeStruct((t, nq), F32), jax.ShapeDtypeStruct((t, nz), F32)],
        compiler_params=_cparams(1),
    )(x2d, g, wqkv, wz)


def _inproj_seq_kernel(x_ref, g_ref, wqkv_ref, wz_ref, qk_ref, kt_ref, vt_ref, z_ref):
    wa = kt_ref.shape[1]
    h = _rms_norm(x_ref[...], g_ref[...])
    qkv = _mm(h, wqkv_ref[...], False)
    qk_ref[...] = qkv[:, :2 * wa]
    kt_ref[0] = qkv[:, wa:2 * wa].T
    vt_ref[0] = qkv[:, 2 * wa:].T
    z_ref[...] = _mm(h, wz_ref[...], False)


def _inproj_seq(x2d, g, wqkv, wz, *, b, tm):
    t, d = x2d.shape
    s = t // b
    nq, nz = wqkv.shape[1], wz.shape[1]
    wa = nq // 3
    per_seq = s // tm
    feat_major = pl.BlockSpec((1, wa, tm), lambda i: (i // per_seq, 0, i % per_seq))
    return pl.pallas_call(
        _inproj_seq_kernel,
        grid=(t // tm,),
        in_specs=[pl.BlockSpec((tm, d), lambda i: (i, 0)),
                  pl.BlockSpec((1, d), lambda i: (0, 0)),
                  pl.BlockSpec((d, nq), lambda i: (0, 0)),
                  pl.BlockSpec((d, nz), lambda i: (0, 0))],
        out_specs=[pl.BlockSpec((tm, 2 * wa), lambda i: (i, 0)), feat_major, feat_major,
                   pl.BlockSpec((tm, nz), lambda i: (i, 0))],
        out_shape=[jax.ShapeDtypeStruct((t, 2 * wa), F32), jax.ShapeDtypeStruct((b, wa, s), F32),
                   jax.ShapeDtypeStruct((b, wa, s), F32), jax.ShapeDtypeStruct((t, nz), F32)],
        compiler_params=_cparams(1),
    )(x2d, g, wqkv, wz)


def _moba_prompt_kernel(q_ref, k_ref, vt_ref, tt_ref, o_ref,
                        kbf_s, vt_s, kmean_s, m_s, l_s, acc_s, sel_s, qbf_s):
    blk = MOBA_BLOCK
    i = pl.program_id(1)
    nb = kbf_s.shape[0]
    n_heads = tt_ref.shape[0]
    scale = HEAD_DIM ** -0.5 * LOG2_E

    @pl.when(i == 0)
    def _():
        for n in range(nb):
            kb = k_ref[0, n * blk:(n + 1) * blk, :]
            kbf_s[n] = kb.astype(BF16)
            kmean_s[n:n + 1, :] = jnp.mean(kb, axis=0, keepdims=True)
            vt_s[n] = vt_ref[0, :, n * blk:(n + 1) * blk].astype(BF16)

    q_t = q_ref[0].T
    blk_row = lax.broadcasted_iota(jnp.int32, (nb, blk), 0)
    key_idx = lax.broadcasted_iota(jnp.int32, (blk, blk), 0)
    qry_idx = lax.broadcasted_iota(jnp.int32, (blk, blk), 1)
    causal = key_idx <= qry_idx
    upper_half = lax.broadcasted_iota(jnp.int32, (LANES, blk), 0) >= HEAD_DIM

    for h in range(n_heads):
        lanes = slice((h // 2) * LANES, (h // 2 + 1) * LANES)
        q_h = jnp.where(upper_half == (h % 2 == 1), q_t[lanes, :], 0.0)
        gate = jnp.dot(kmean_s[:, lanes], q_h, precision=HIGHEST, preferred_element_type=F32)
        gate = jnp.where(blk_row < i, gate, NEG)
        rank = jnp.zeros((nb, blk), jnp.int32)
        for m in range(nb):
            gm = gate[m:m + 1, :]
            beats = (gm > gate) | ((gm == gate) & (m < blk_row))
            rank = rank + beats.astype(jnp.int32)
        sel_s[h] = jnp.where((rank < MOBA_TOPK) & (blk_row < i), 1.0, 0.0)
        qbf_s[h] = (q_h * scale).astype(BF16)
    m_s[...] = jnp.full(m_s.shape, NEG, F32)
    l_s[...] = jnp.zeros(l_s.shape, F32)
    acc_s[...] = jnp.zeros(acc_s.shape, F32)

    def attend(group, n, bias_of, mask_of, uniform_bias=False):
        scores = [jnp.dot(kbf_s[n, :, (h // 2) * LANES:(h // 2 + 1) * LANES], qbf_s[h],
                          preferred_element_type=F32) for h in group]
        probs, alphas = [], []
        for h, s in zip(group, scores):
            m_old = m_s[h:h + 1, :]
            if uniform_bias:
                bias = bias_of(h)
                s = jnp.where(mask_of(h), s, NEG)
                m_new = jnp.maximum(m_old, jnp.max(s, axis=0, keepdims=True) + bias)
                p = jnp.exp2(s - (m_new - bias))
            else:
                s = jnp.where(mask_of(h), s + bias_of(h), NEG)
                m_new = jnp.maximum(m_old, jnp.max(s, axis=0, keepdims=True))
                p = jnp.exp2(s - m_new)
            alpha = jnp.exp2(m_old - m_new)
            l_s[h:h + 1, :] = alpha * l_s[h:h + 1, :] + jnp.sum(p, axis=0, keepdims=True)
            m_s[h:h + 1, :] = m_new
            probs.append(p.astype(BF16))
            alphas.append(alpha)
        pvs = [jnp.dot(vt_s[n, h * HEAD_DIM:(h + 1) * HEAD_DIM, :], p, preferred_element_type=F32)
               for h, p in zip(group, probs)]
        for h, alpha, pv in zip(group, alphas, pvs):
            acc_s[h] = alpha * acc_s[h] + pv

    for g0 in range(0, n_heads, MOBA_HEAD_GROUP):
        group = range(g0, g0 + MOBA_HEAD_GROUP)
        attend(group, i, lambda h: tt_ref[h, :, blk:2 * blk], lambda h: causal)

        @pl.when(i >= 1)
        def _(group=group):
            n = i - 1
            attend(group, n, lambda h: tt_ref[h, :, 0:blk], lambda h: sel_s[h, pl.ds(n, 1), :] > 0.5)

        def far_body(n, carry, group=group):
            attend(group, n, lambda h: tt_ref[h, 0:1, blk - 1:blk],
                   lambda h: sel_s[h, pl.ds(n, 1), :] > 0.5, uniform_bias=True)
            return carry

        lax.fori_loop(0, jnp.maximum(i - 1, 0), far_body, 0)

    for pair in range(n_heads // 2):
        outs = [acc_s[h] / l_s[h:h + 1, :] for h in (2 * pair, 2 * pair + 1)]
        o_ref[0, :, pair * LANES:(pair + 1) * LANES] = jnp.concatenate(outs, axis=0).T


def _moba_prompt(qk3, vt3, tt):
    b, s, w2 = qk3.shape
    w = w2 // 2
    blk = MOBA_BLOCK
    nb = s // blk
    n_heads = w // HEAD_DIM
    return pl.pallas_call(
        _moba_prompt_kernel,
        grid=(b, nb),
        in_specs=[pl.BlockSpec((1, blk, w), lambda bi, i: (bi, i, 0)),
                  pl.BlockSpec((1, s, w), lambda bi, i: (bi, 0, 1)),
                  pl.BlockSpec((1, w, s), lambda bi, i: (bi, 0, 0)),
                  pl.BlockSpec((n_heads, blk, 2 * blk), lambda bi, i: (0, 0, 0))],
        out_specs=pl.BlockSpec((1, blk, w), lambda bi, i: (bi, i, 0)),
        out_shape=jax.ShapeDtypeStruct((b, s, w), F32),
        scratch_shapes=[pltpu.VMEM((nb, blk, w), BF16),
                        pltpu.VMEM((nb, w, blk), BF16),
                        pltpu.VMEM((nb, w), F32),
                        pltpu.VMEM((n_heads, blk), F32),
                        pltpu.VMEM((n_heads, blk), F32),
                        pltpu.VMEM((n_heads, HEAD_DIM, blk), F32),
                        pltpu.VMEM((n_heads, nb, blk), F32),
                        pltpu.VMEM((n_heads, LANES, blk), BF16)],
        compiler_params=_cparams(2),
    )(qk3, qk3, vt3, tt)


def _rel_bias_table(rel_bias, n_dist):
    n = jnp.arange(n_dist, dtype=jnp.int32)
    max_exact = N_BUCKETS // 2
    nf = jnp.maximum(n, 1).astype(F32)
    large = max_exact + (jnp.log(nf / max_exact) / math.log(MAX_DISTANCE / max_exact)
                         * (N_BUCKETS - max_exact)).astype(jnp.int32)
    large = jnp.minimum(large, N_BUCKETS - 1)
    bucket = jnp.where(n < max_exact, n, large)
    return rel_bias.astype(F32).T[:, bucket]


def _bias_tiles(tab):
    blk = MOBA_BLOCK
    period = 2 * blk
    n_heads = tab.shape[0]
    base = jnp.concatenate([tab[:, blk:period], tab[:, 0:blk]], axis=1)
    stride = 2 * period - 1
    reps = -(-(blk * stride) // period)
    seq = jnp.tile(base, (1, reps))[:, :blk * stride]
    return seq.reshape(n_heads, blk, stride)[:, :, :period]


def _mm3(a, b):
    a_hi = a.astype(BF16)
    b_hi = b.astype(BF16)
    a_lo = (a - a_hi.astype(F32)).astype(BF16)
    b_lo = (b - b_hi.astype(F32)).astype(BF16)
    return (jnp.dot(a_hi, b_hi, preferred_element_type=F32)
            + jnp.dot(a_hi, b_lo, preferred_element_type=F32)
            + jnp.dot(a_lo, b_hi, preferred_element_type=F32))


def _rwkv_prompt_kernel(zr_ref, zk_ref, zv_ref, zl_ref,
                        mur_ref, muk_ref, muv_ref, mul_ref,
                        w0_ref, w2_ref, a0_ref, a2_ref, g2_ref,
                        kk_ref, ka_ref, rk_ref, lg_ref, lb_ref,
                        rw_ref, st_ref,
                        p_s, last_r, last_k, last_v, last_l):
    t = pl.program_id(2)
    tc = zr_ref.shape[1]
    w = MXU_DIM
    c_len = RWKV_CHUNK
    hd_shift = HEAD_DIM.bit_length() - 1

    @pl.when(t == 0)
    def _():
        p_s[...] = jnp.zeros(p_s.shape, F32)
        last_r[...] = jnp.zeros(last_r.shape, F32)
        last_k[...] = jnp.zeros(last_k.shape, F32)
        last_v[...] = jnp.zeros(last_v.shape, F32)
        last_l[...] = jnp.zeros(last_l.shape, F32)

    def token_shift_mix(z_ref, last_ref, mu_ref):
        z = z_ref[0]
        row = lax.broadcasted_iota(jnp.int32, z.shape, 0)
        prev = jnp.where(row == 0, last_ref[...], pltpu.roll(z, 1, 0))
        last_ref[...] = z[tc - 1:tc, :]
        return z + (prev - z) * mu_ref[...]

    ri = lax.broadcasted_iota(jnp.int32, (w, w), 0)
    ci = lax.broadcasted_iota(jnp.int32, (w, w), 1)
    same_head = (ri >> hd_shift) == (ci >> hd_shift)
    tok_r = ri & (HEAD_DIM - 1)
    tok_c = ci & (HEAD_DIM - 1)
    strict = same_head & (tok_r > tok_c)
    incl = same_head & (tok_r >= tok_c)
    eye = ri == ci
    head_ones = jnp.where(same_head, 1.0, 0.0).astype(BF16)
    tri = jnp.where(lax.broadcasted_iota(jnp.int32, (c_len, c_len), 0)
                    >= lax.broadcasted_iota(jnp.int32, (c_len, c_len), 1), 1.0, 0.0).astype(BF16)

    def head_sum(x):
        return _mm_split_rhs(x, head_ones, 2)

    def stack(x):
        return jnp.where(same_head, jnp.concatenate([x] * HEADS_PER_TILE, axis=0), 0.0)

    def unstack(x):
        out = x[0:c_len]
        for e in range(1, HEADS_PER_TILE):
            out = out + x[e * c_len:(e + 1) * c_len]
        return out

    r = token_shift_mix(zr_ref, last_r, mur_ref)
    k = token_shift_mix(zk_ref, last_k, muk_ref)
    v = token_shift_mix(zv_ref, last_v, muv_ref)
    zl = token_shift_mix(zl_ref, last_l, mul_ref)

    n_wa = LORA_W + LORA_A
    zl_wa = zl[:, :n_wa]
    w_pre = w0_ref[...] + _mm(jnp.tanh(zl_wa), w2_ref[0:n_wa, :], False)
    logd = -jnp.exp(-_softplus(-w_pre) - 0.5)
    eta = _sigmoid(a0_ref[...] + _mm(zl_wa, a2_ref[0:n_wa, :], False))
    gate = _mm(_sigmoid(zl[:, n_wa:]), g2_ref[n_wa:, :], False)
    kk = k * kk_ref[...]
    kk = kk * lax.rsqrt(jnp.maximum(head_sum(kk * kk), 1e-24))
    kh = k * (1.0 + (eta - 1.0) * ka_ref[...])
    bonus = head_sum(r * kh * rk_ref[...]) * v
    kb = kk * eta

    chunks = range(tc // c_len)
    each = lambda f, *lists: [f(*(x[c] for x in lists)) for c in chunks]
    rows = [slice(c * c_len, (c + 1) * c_len) for c in chunks]
    ld = [logd[rw] for rw in rows]
    cum = each(lambda x: _mm_split(tri, x, 3), ld)
    cum_end = each(lambda x: x[c_len - 1:c_len, :], cum)
    e_inv = each(lambda x: jnp.exp(-x), cum)
    e_rem = each(lambda x, xe: jnp.exp(xe - x), cum, cum_end)
    a_t = [stack(-kk[rows[c]] * jnp.exp(cum[c] - ld[c])) for c in chunks]
    b_t = [stack(kb[rows[c]] * e_inv[c]) for c in chunks]
    k_t = [stack(kh[rows[c]] * e_inv[c]) for c in chunks]
    r_t = [stack(r[rows[c]] * jnp.exp(cum[c])) for c in chunks]
    b_h = [stack(kb[rows[c]] * e_rem[c]) for c in chunks]
    k_h = [stack(kh[rows[c]] * e_rem[c]) for c in chunks]
    v_s = [stack(v[rows[c]]) for c in chunks]

    prod = each(lambda a, rr, bb, kt: _mm_nt(jnp.concatenate([a, rr], axis=0),
                                             jnp.concatenate([bb, kt], axis=0)), a_t, r_t, b_t, k_t)
    n_mat = each(lambda x: jnp.where(strict, x[0:w, 0:w], 0.0), prod)
    a_k = each(lambda x: jnp.where(strict, x[0:w, w:2 * w], 0.0), prod)
    m_rb = each(lambda x: jnp.where(incl, x[w:2 * w, 0:w], 0.0), prod)
    m_rk = each(lambda x: jnp.where(incl, x[w:2 * w, w:2 * w], 0.0), prod)

    t_inv = each(lambda x: jnp.where(eye, 1.0, 0.0) + x, n_mat)
    n_pow = n_mat
    for _ in range(c_len.bit_length() - 2):
        n_pow = each(lambda x: _mm(x, x, False), n_pow)
        t_inv = each(lambda x, y: x + _mm(x, y, False), t_inv, n_pow)

    a_p = each(lambda x, y: _mm(x, y, False), t_inv, a_t)
    a_v = each(lambda x, y: _mm(x, y, False), a_k, v_s)
    v_p = each(lambda x, y: _mm(x, y, False), t_inv, a_v)
    r_p = each(lambda rr, m, ap: unstack(rr + _mm(m, ap, False)), r_t, m_rb, a_p)
    y_0 = each(lambda mb, vp, mk, vs: unstack(_mm(mb, vp, False) + _mm(mk, vs, False)),
               m_rb, v_p, m_rk, v_s)
    decay = each(jnp.exp, cum_end)
    mix = each(_mm_tn, a_p, b_h)
    add = each(lambda vp, bh, vs, kh_: _mm_tn(vp, bh) + _mm_tn(vs, kh_), v_p, b_h, v_s, k_h)

    st = p_s[...]
    ys = []
    for c in chunks:
        ys.append(_mm_nt(r_p[c], st) + y_0[c])
        st = st * decay[c] + _mm(st, mix[c], False) + add[c]
    p_s[...] = st
    st_ref[0, 0] = st

    y = jnp.concatenate(ys, axis=0)
    mean = head_sum(y) * (1.0 / HEAD_DIM)
    yc = y - mean
    var = head_sum(yc * yc) * (1.0 / HEAD_DIM)
    yn = yc * lax.rsqrt(var + GN_EPS) * lg_ref[...] + lb_ref[...]
    rw_ref[0] = (yn + bonus) * gate


def _rwkv_prompt(z3, mu, w0, w2p, a0, a2p, g2p, kk, ka, rk, lg, lb, *, tc):
    b, s, _ = z3.shape
    w = MXU_DIM
    wr = w0.shape[1]
    nt = wr // w
    lw = w2p.shape[0]
    lora_blk = (3 * wr) // lw
    vec = lambda off: pl.BlockSpec((1, w), lambda bi, q, t, off=off: (0, off + q))
    zblk = lambda off: pl.BlockSpec((1, tc, w), lambda bi, q, t, off=off: (bi, t, off + q))
    lora_w = pl.BlockSpec((lw, w), lambda bi, q, t: (0, q))
    return pl.pallas_call(
        _rwkv_prompt_kernel,
        grid=(b, nt, s // tc),
        in_specs=[zblk(0), zblk(nt), zblk(2 * nt),
                  pl.BlockSpec((1, tc, lw), lambda bi, q, t: (bi, t, lora_blk)),
                  vec(0), vec(nt), vec(2 * nt),
                  pl.BlockSpec((1, lw), lambda bi, q, t: (0, lora_blk)),
                  vec(0), lora_w, vec(0), lora_w, lora_w,
                  vec(0), vec(0), vec(0), vec(0), vec(0)],
        out_specs=[pl.BlockSpec((1, tc, w), lambda bi, q, t: (bi, t, q)),
                   pl.BlockSpec((1, 1, w, w), lambda bi, q, t: (bi, q, 0, 0))],
        out_shape=[jax.ShapeDtypeStruct((b, s, wr), F32),
                   jax.ShapeDtypeStruct((b, nt, w, w), F32)],
        scratch_shapes=[pltpu.VMEM((w, w), F32),
                        pltpu.VMEM((1, w), F32), pltpu.VMEM((1, w), F32),
                        pltpu.VMEM((1, w), F32), pltpu.VMEM((1, lw), F32)],
        compiler_params=_cparams(3),
    )(z3, z3, z3, z3, mu, mu, mu, mu, w0, w2p, a0, a2p, g2p, kk, ka, rk, lg, lb)


def _state_from_tiles(st):
    b, nt = st.shape[:2]
    heads = []
    for q in range(nt):
        for e in range(HEADS_PER_TILE):
            sl = slice(e * HEAD_DIM, (e + 1) * HEAD_DIM)
            heads.append(st[:, q, sl, sl])
    return jnp.stack(heads, axis=1)


def _lora_padded(w2, a2, g2):
    lw = 3 * LANES
    n = w2.shape[1]
    o_a = LORA_W
    o_g = LORA_W + LORA_A
    w2p = jnp.zeros((lw, n), F32).at[0:o_a].set(w2)
    a2p = jnp.zeros((lw, n), F32).at[o_a:o_g].set(a2)
    g2p = jnp.zeros((lw, n), F32).at[o_g:o_g + LORA_G].set(g2)
    return w2p, a2p, g2p


ROUTER_LANE0 = N_GROUPS


def _first_argmax(x, lane):
    mx = jnp.max(x, axis=1, keepdims=True)
    idx = jnp.min(jnp.where(x == mx, lane, LANES), axis=1, keepdims=True)
    return mx, idx


def _outproj_router_kernel(x_ref, att_ref, rw_ref, wo_ref, g_ref, wr_ref, br_ref,
                           x1_ref, h2_ref, comb_ref, *, precise):
    wa = att_ref.shape[1]
    x1 = (x_ref[...] + _mm(att_ref[...], wo_ref[0:wa, :], precise)
          + _mm(rw_ref[...], wo_ref[wa:, :], precise))
    x1_ref[...] = x1
    h2 = _rms_norm(x1, g_ref[...])
    h2_ref[...] = h2.astype(h2_ref.dtype)
    if precise:
        logits = _mm(h2, wr_ref[...], True) + br_ref[...]
    else:
        logits = _mm3(h2, wr_ref[...]) + br_ref[...]
    n_exp = N_GROUPS * EXPERTS_PER_GROUP
    lane = lax.broadcasted_iota(jnp.int32, logits.shape, 1)
    is_grp = lane < N_GROUPS
    gl = jnp.where(is_grp, logits, NEG)
    g_max, grp = _first_argmax(gl, lane)
    p_grp = 1.0 / jnp.sum(jnp.where(is_grp, jnp.exp(gl - g_max), 0.0), axis=1, keepdims=True)
    e_lane = lane - ROUTER_LANE0
    grp_shift = EXPERTS_PER_GROUP.bit_length() - 1
    in_grp = (e_lane >= 0) & (e_lane < n_exp) & ((e_lane >> grp_shift) == grp)
    el = jnp.where(in_grp, logits, NEG)
    m1, i1 = _first_argmax(el, lane)
    el2 = jnp.where(lane == i1, NEG, el)
    m2, i2 = _first_argmax(el2, lane)
    ratio = jnp.exp(m2 - m1)
    p1 = 1.0 / (1.0 + ratio)
    p2 = ratio / (1.0 + ratio)
    comb_ref[...] = p_grp * (jnp.where(lane == i1, p1, 0.0) + jnp.where(lane == i2, p2, 0.0))


def _outproj_router(x2d, att, rw, wo, g, wr, br, *, tm, precise):
    t, d = x2d.shape
    wa, wr_cols = att.shape[1], rw.shape[1]
    h_dtype = F32 if precise else BF16
    tok = lambda n: pl.BlockSpec((tm, n), lambda i: (i, 0))
    full = lambda a: pl.BlockSpec(a.shape, lambda i: (0,) * a.ndim)
    return pl.pallas_call(
        functools.partial(_outproj_router_kernel, precise=precise),
        grid=(t // tm,),
        in_specs=[tok(d), tok(wa), tok(wr_cols), full(wo), full(g), full(wr), full(br)],
        out_specs=[tok(d), tok(d), tok(LANES)],
        out_shape=[jax.ShapeDtypeStruct((t, d), F32), jax.ShapeDtypeStruct((t, d), h_dtype),
                   jax.ShapeDtypeStruct((t, LANES), F32)],
        compiler_params=_cparams(1),
    )(x2d, att, rw, wo, g, wr, br)


def _moe_kernel(h2_ref, comb_ref, x1_ref, wg_ref, wu_ref, wd_ref, x2_ref, *, precise):
    g = pl.program_id(1)

    @pl.when(g == 0)
    def _():
        x2_ref[...] = x1_ref[...]

    h2 = h2_ref[...]
    comb = comb_ref[...]
    f = wd_ref.shape[1] // EXPERTS_PER_GROUP
    lane = lax.broadcasted_iota(jnp.int32, comb.shape, 1)
    hg = _mm(h2, wg_ref[0], precise)
    hu = _mm(h2, wu_ref[0], precise)
    acts = []
    for j in range(EXPERTS_PER_GROUP):
        e_lane = ROUTER_LANE0 + g * EXPERTS_PER_GROUP + j
        weight = jnp.sum(jnp.where(lane == e_lane, comb, 0.0), axis=1, keepdims=True)
        a = hg[:, j * f:(j + 1) * f]
        acts.append(a * _sigmoid(a) * hu[:, j * f:(j + 1) * f] * weight)
    x2_ref[...] += _mm(jnp.concatenate(acts, axis=1), wd_ref[0], precise)


def _moe(h2, comb, x1, wg, wu, wd, *, tm, precise):
    t, d = x1.shape
    n_grp, _, gf = wg.shape
    return pl.pallas_call(
        functools.partial(_moe_kernel, precise=precise),
        grid=(t // tm, n_grp),
        in_specs=[pl.BlockSpec((tm, d), lambda i, g: (i, 0)),
                  pl.BlockSpec((tm, LANES), lambda i, g: (i, 0)),
                  pl.BlockSpec((tm, d), lambda i, g: (i, 0)),
                  pl.BlockSpec((1, d, gf), lambda i, g: (g, 0, 0)),
                  pl.BlockSpec((1, d, gf), lambda i, g: (g, 0, 0)),
                  pl.BlockSpec((1, gf, d), lambda i, g: (g, 0, 0))],
        out_specs=pl.BlockSpec((tm, d), lambda i, g: (i, 0)),
        out_shape=jax.ShapeDtypeStruct((t, d), F32),
        compiler_params=_cparams(2),
    )(h2, comb, x1, wg, wu, wd)


def _ple_kernel(x_ref, p_ref, wple_ref, wgate_ref, fg_ref, y_ref, *, precise, final):
    x = x_ref[...]
    emb = _mm(p_ref[...], wple_ref[...], precise)
    x = x + emb * _sigmoid(_mm(x, wgate_ref[...], precise))
    y_ref[...] = _rms_norm(x, fg_ref[...]) if final else x


def _ple(x2d, p2d, wple, wgate, fg, *, tm, precise, final):
    t, d = x2d.shape
    pd = p2d.shape[1]
    full = lambda a: pl.BlockSpec(a.shape, lambda i: (0,) * a.ndim)
    return pl.pallas_call(
        functools.partial(_ple_kernel, precise=precise, final=final),
        grid=(t // tm,),
        in_specs=[pl.BlockSpec((tm, d), lambda i: (i, 0)), pl.BlockSpec((tm, pd), lambda i: (i, 0)),
                  full(wple), full(wgate), full(fg)],
        out_specs=pl.BlockSpec((tm, d), lambda i: (i, 0)),
        out_shape=jax.ShapeDtypeStruct((t, d), F32),
        compiler_params=_cparams(1),
    )(x2d, p2d, wple, wgate, fg)


def _rwkv_sample_prep_kernel(z_ref, zp_ref, mu_ref, w0_ref, w2_ref, a0_ref, a2_ref, g2_ref,
                             kk_ref, ka_ref,
                             r_ref, d_ref, kkn_ref, kb_ref, kh_ref, v_ref, g_ref):
    wr = w0_ref.shape[1]
    hd_shift = HEAD_DIM.bit_length() - 1
    z = z_ref[...]
    zs = z + (zp_ref[...] - z) * mu_ref[...]
    r = zs[:, 0:wr]
    k = zs[:, wr:2 * wr]
    v = zs[:, 2 * wr:3 * wr]
    zl = zs[:, 3 * wr:]
    w_pre = w0_ref[...] + _mm(jnp.tanh(zl), w2_ref[...], True)
    decay = jnp.exp(-jnp.exp(-_softplus(-w_pre) - 0.5))
    eta = _sigmoid(a0_ref[...] + _mm(zl, a2_ref[...], True))
    ri = lax.broadcasted_iota(jnp.int32, (wr, wr), 0)
    ci = lax.broadcasted_iota(jnp.int32, (wr, wr), 1)
    head_ones = jnp.where((ri >> hd_shift) == (ci >> hd_shift), 1.0, 0.0)
    kk = k * kk_ref[...]
    kk = kk * lax.rsqrt(jnp.maximum(_mm(kk * kk, head_ones, True), 1e-24))
    r_ref[...] = r
    d_ref[...] = decay
    kkn_ref[...] = kk
    kb_ref[...] = kk * eta
    kh_ref[...] = k * (1.0 + (eta - 1.0) * ka_ref[...])
    v_ref[...] = v
    g_ref[...] = _mm(_sigmoid(zl), g2_ref[...], True)


def _rwkv_sample_prep(z, zp, mu, w0, w2p, a0, a2p, g2p, kk, ka):
    n = z.shape[0]
    wr = w0.shape[1]
    args = (z, zp, mu, w0, w2p, a0, a2p, g2p, kk, ka)
    full = lambda a: pl.BlockSpec(a.shape, lambda i: (0,) * a.ndim)
    return pl.pallas_call(
        _rwkv_sample_prep_kernel,
        grid=(1,),
        in_specs=[full(a) for a in args],
        out_specs=[pl.BlockSpec((n, wr), lambda i: (0, 0))] * 7,
        out_shape=[jax.ShapeDtypeStruct((n, wr), F32)] * 7,
        compiler_params=_cparams(1),
    )(*args)


def _rwkv_sample_step_kernel(s_ref, r_ref, d_ref, kkn_ref, kb_ref, kh_ref, v_ref, g_ref,
                             rk_ref, lg_ref, lb_ref, so_ref, rw_ref):
    n_heads = s_ref.shape[1]
    eye = (lax.broadcasted_iota(jnp.int32, (HEAD_DIM, HEAD_DIM), 0)
           == lax.broadcasted_iota(jnp.int32, (HEAD_DIM, HEAD_DIM), 1))
    for h in range(n_heads):
        hrow = lambda ref, lead=True: ref[0, h:h + 1, :] if lead else ref[h:h + 1, :]
        st = s_ref[0, h]
        r, v, kh = hrow(r_ref), hrow(v_ref), hrow(kh_ref)
        sa = jnp.sum(st * (-hrow(kkn_ref)), axis=1, keepdims=True)
        v_col = jnp.sum(jnp.where(eye, v, 0.0), axis=1, keepdims=True)
        st = st * hrow(d_ref) + sa * hrow(kb_ref) + v_col * kh
        so_ref[0, h] = st
        y_col = jnp.sum(st * r, axis=1, keepdims=True)
        y = jnp.sum(jnp.where(eye, y_col, 0.0), axis=0, keepdims=True)
        mean = jnp.mean(y, axis=1, keepdims=True)
        yc = y - mean
        var = jnp.mean(yc * yc, axis=1, keepdims=True)
        yn = yc * lax.rsqrt(var + GN_EPS) * hrow(lg_ref, False) + hrow(lb_ref, False)
        bonus = jnp.sum(r * kh * hrow(rk_ref, False), axis=1, keepdims=True) * v
        rw_ref[0, h:h + 1, :] = (yn + bonus) * hrow(g_ref)


def _rwkv_sample_step(state, per_head, rk, lg, lb):
    n, nh = state.shape[:2]
    tok = pl.BlockSpec((1, nh, HEAD_DIM), lambda i: (i, 0, 0))
    par = pl.BlockSpec((nh, HEAD_DIM), lambda i: (0, 0))
    st = pl.BlockSpec((1, nh, HEAD_DIM, HEAD_DIM), lambda i: (i, 0, 0, 0))
    return pl.pallas_call(
        _rwkv_sample_step_kernel,
        grid=(n,),
        in_specs=[st] + [tok] * 7 + [par] * 3,
        out_specs=[st, tok],
        out_shape=[jax.ShapeDtypeStruct(state.shape, F32),
                   jax.ShapeDtypeStruct((n, nh, HEAD_DIM), F32)],
        compiler_params=_cparams(1),
    )(state, *per_head, rk, lg, lb)


PAGES_PER_STEP = 32


def _moba_sample_score_kernel(pt_ref, qc_ref, *refs):
    pps = PAGES_PER_STEP
    k_refs, s_ref, gate_ref = refs[:pps], refs[pps], refs[pps + 1]
    n_heads, _, page = k_refs[0].shape[1:]
    pages_per_blk = MOBA_BLOCK // page
    blks_per_step = pps // pages_per_blk
    j = pl.program_id(1)
    qc = qc_ref[0]
    lane = lax.broadcasted_iota(jnp.int32, (n_heads, LANES), 1)

    @pl.when(j == 0)
    def _():
        gate_ref[0] = jnp.zeros(gate_ref.shape[1:], F32)

    gate = gate_ref[0]
    for blk_i in range(blks_per_step):
        blk_sum = jnp.zeros((n_heads, page), F32)
        for half in range(pages_per_blk):
            i = blk_i * pages_per_blk + half
            s = jnp.sum(k_refs[i][0] * qc, axis=1)
            s_ref[0, :, i * page:(i + 1) * page] = s
            blk_sum = blk_sum + s
        g_col = jnp.sum(blk_sum, axis=1, keepdims=True) * (1.0 / MOBA_BLOCK)
        gate = jnp.where(lane == j * blks_per_step + blk_i, g_col, gate)
    gate_ref[0] = gate


def _moba_sample_scores(page_table, q_cols, cache_t):
    n, n_pages = page_table.shape
    n_heads, hd, page = cache_t.shape[1:]
    pps = PAGES_PER_STEP
    k_spec = lambda i: pl.BlockSpec((1, n_heads, hd, page),
                                    lambda b, j, pt, i=i: (pt[b, j * pps + i], 0, 0, 0))
    grid_spec = pltpu.PrefetchScalarGridSpec(
        num_scalar_prefetch=1,
        grid=(n, n_pages // pps),
        in_specs=[pl.BlockSpec((1, n_heads, hd, page), lambda b, j, pt: (b, 0, 0, 0))]
                 + [k_spec(i) for i in range(pps)],
        out_specs=[pl.BlockSpec((1, n_heads, pps * page), lambda b, j, pt: (b, 0, j)),
                   pl.BlockSpec((1, n_heads, LANES), lambda b, j, pt: (b, 0, 0))])
    return pl.pallas_call(
        _moba_sample_score_kernel,
        grid_spec=grid_spec,
        out_shape=[jax.ShapeDtypeStruct((n, n_heads, n_pages * page), F32),
                   jax.ShapeDtypeStruct((n, n_heads, LANES), F32)],
        compiler_params=_cparams(2),
    )(page_table, q_cols, *([cache_t] * pps))


def _moba_sample_select_kernel(s_ref, gate_ref, bias_ref, q_ref, kn_ref, b0_ref,
                               p_ref, pown_ref, sel_ref, member_s):
    n_heads, n_keys = s_ref.shape[1:]
    blk_shift = MOBA_BLOCK.bit_length() - 1
    n_blk = n_keys // MOBA_BLOCK
    scale = HEAD_DIM ** -0.5
    s = s_ref[0]

    @pl.when(pl.program_id(0) == 0)
    def _():
        member_s[...] = jnp.where(
            lax.broadcasted_iota(jnp.int32, (LANES, n_keys), 1) >> blk_shift
            == lax.broadcasted_iota(jnp.int32, (LANES, n_keys), 0), 1.0, 0.0).astype(BF16)

    lane = lax.broadcasted_iota(jnp.int32, (n_heads, LANES), 1)
    gate = jnp.where(lane < n_blk, gate_ref[0], NEG)
    sel = jnp.zeros((n_heads, LANES), F32)
    for _ in range(MOBA_TOPK):
        _, pick = _first_argmax(gate, lane)
        hit = lane == pick
        sel = jnp.where(hit, 1.0, sel)
        gate = jnp.where(hit, NEG, gate)
    sel_ref[0] = sel
    key_sel = _mm(sel, member_s[...], False)
    sc = jnp.where(key_sel > 0.5, s * scale + bias_ref[...], NEG)
    s_own = jnp.sum(q_ref[0] * kn_ref[0], axis=1, keepdims=True) * scale + b0_ref[...]
    m = jnp.maximum(jnp.max(sc, axis=1, keepdims=True), s_own)
    p = jnp.exp(sc - m)
    p_own = jnp.exp(s_own - m)
    inv = 1.0 / (jnp.sum(p, axis=1, keepdims=True) + p_own)
    p_ref[0] = p * inv
    pown_ref[0] = jnp.broadcast_to(p_own * inv, (n_heads, LANES))


def _moba_sample_select(scores, gate, bias_keys, q3, kn3, bias0):
    n, n_heads, n_keys = scores.shape
    hd = q3.shape[2]
    tok = pl.BlockSpec((1, n_heads, hd), lambda b: (b, 0, 0))
    keys = pl.BlockSpec((1, n_heads, n_keys), lambda b: (b, 0, 0))
    lane_blk = pl.BlockSpec((1, n_heads, LANES), lambda b: (b, 0, 0))
    return pl.pallas_call(
        _moba_sample_select_kernel,
        grid=(n,),
        in_specs=[keys, lane_blk, pl.BlockSpec((n_heads, n_keys), lambda b: (0, 0)), tok, tok,
                  pl.BlockSpec((n_heads, 1), lambda b: (0, 0))],
        out_specs=[keys, lane_blk, lane_blk],
        out_shape=[jax.ShapeDtypeStruct((n, n_heads, n_keys), F32),
                   jax.ShapeDtypeStruct((n, n_heads, LANES), F32),
                   jax.ShapeDtypeStruct((n, n_heads, LANES), F32)],
        scratch_shapes=[pltpu.VMEM((LANES, n_keys), BF16)],
        compiler_params=_cparams(1),
    )(scores, gate, bias_keys, q3, kn3, bias0)


def _pages_to_fetch(page_table, sel, pages_per_blk):
    n, n_pages = page_table.shape
    pps = PAGES_PER_STEP
    steps = n * n_pages // pps
    n_blk = n_pages // pages_per_blk
    blk_needed = jnp.max(sel[:, :, :n_blk], axis=1) > 0.0
    needed = jnp.repeat(blk_needed, pages_per_blk, axis=1)
    step_id = jnp.arange(steps, dtype=jnp.int32)[:, None]
    last_needed = lax.cummax(jnp.where(needed.reshape(steps, pps), step_id, 0), axis=0)
    pages = jnp.take_along_axis(page_table.reshape(steps, pps), last_needed, axis=0)
    return pages.reshape(n, n_pages), blk_needed.astype(jnp.int32)


def _moba_sample_pv_kernel(pt_ref, need_ref, p_ref, pown_ref, vn_ref, *refs):
    pps = PAGES_PER_STEP
    v_refs, o_ref, acc_s = refs[:pps], refs[pps], refs[pps + 1]
    n_heads, hd, page = v_refs[0].shape[1:]
    pages_per_blk = MOBA_BLOCK // page
    blks_per_step = pps // pages_per_blk
    b = pl.program_id(0)
    j = pl.program_id(1)

    @pl.when(j == 0)
    def _():
        acc_s[...] = jnp.zeros(acc_s.shape, F32)

    for blk_i in range(blks_per_step):
        @pl.when(need_ref[b, j * blks_per_step + blk_i] > 0)
        def _(blk_i=blk_i):
            acc = acc_s[...]
            for half in range(pages_per_blk):
                i = blk_i * pages_per_blk + half
                w = p_ref[0, :, i * page:(i + 1) * page][:, None, :]
                acc = acc + jnp.where(w > 0.0, v_refs[i][0] * w, 0.0)
            acc_s[...] = acc

    @pl.when(j == pl.num_programs(1) - 1)
    def _():
        eye = (lax.broadcasted_iota(jnp.int32, (hd, hd), 0)
               == lax.broadcasted_iota(jnp.int32, (hd, hd), 1))
        for h in range(n_heads):
            col = jnp.sum(acc_s[h], axis=1, keepdims=True)
            row = jnp.sum(jnp.where(eye, col, 0.0), axis=0, keepdims=True)
            o_ref[0, h:h + 1, :] = row + pown_ref[0, h:h + 1, 0:hd] * vn_ref[0, h:h + 1, :]


def _moba_sample_pv(pages, blk_needed, probs, p_own, vn3, cache_t):
    n, n_pages = pages.shape
    n_heads, hd, page = cache_t.shape[1:]
    pps = PAGES_PER_STEP
    v_spec = lambda i: pl.BlockSpec((1, n_heads, hd, page),
                                    lambda b, j, pt, need, i=i: (pt[b, j * pps + i], 0, 0, 0))
    tok = pl.BlockSpec((1, n_heads, hd), lambda b, j, pt, need: (b, 0, 0))
    grid_spec = pltpu.PrefetchScalarGridSpec(
        num_scalar_prefetch=2,
        grid=(n, n_pages // pps),
        in_specs=[pl.BlockSpec((1, n_heads, pps * page), lambda b, j, pt, need: (b, 0, j)),
                  pl.BlockSpec((1, n_heads, LANES), lambda b, j, pt, need: (b, 0, 0)), tok]
                 + [v_spec(i) for i in range(pps)],
        out_specs=tok,
        scratch_shapes=[pltpu.VMEM((n_heads, hd, page), F32)])
    return pl.pallas_call(
        _moba_sample_pv_kernel,
        grid_spec=grid_spec,
        out_shape=jax.ShapeDtypeStruct((n, n_heads, hd), F32),
        compiler_params=_cparams(2),
    )(pages, blk_needed, probs, p_own, vn3, *([cache_t] * pps))


def _sample_key_bias(tab, past_len):
    n_heads, n_dist = tab.shape
    assert past_len >= n_dist
    tab_x = jnp.concatenate([tab, tab[:, -1:]], axis=1)
    near = jnp.flip(tab_x[:, 1:n_dist + 1], axis=1)
    far = jnp.broadcast_to(tab[:, -1:], (n_heads, past_len - n_dist))
    return jnp.concatenate([far, near], axis=1)


def _moba_sample(q, k_new, v_new, cache_k_l, cache_v_l, page_table, tab):
    n, width = q.shape
    n_pool, page, n_heads, hd = cache_k_l.shape
    n_pages = page_table.shape[1]
    past_len = n_pages * page
    assert hd == HEAD_DIM and n_heads * hd == width and page == LANES
    assert past_len % MOBA_BLOCK == 0 and past_len // MOBA_BLOCK >= MOBA_TOPK
    assert n_pages % PAGES_PER_STEP == 0
    heads = lambda a: a.reshape(n, n_heads, hd)
    q3, kn3, vn3 = heads(q), heads(k_new), heads(v_new)
    ck = jnp.transpose(cache_k_l, (0, 2, 3, 1))
    cv = jnp.transpose(cache_v_l, (0, 2, 3, 1))
    q_cols = jnp.broadcast_to(q3[..., None], (n, n_heads, hd, page))
    assert PAGES_PER_STEP % (MOBA_BLOCK // page) == 0
    scores, gate = _moba_sample_scores(page_table, q_cols, ck)
    probs, p_own, sel = _moba_sample_select(scores, gate, _sample_key_bias(tab, past_len), q3, kn3,
                                            tab[:, 0:1])
    v_pages, blk_needed = _pages_to_fetch(page_table, sel, MOBA_BLOCK // page)
    return _moba_sample_pv(v_pages, blk_needed, probs, p_own, vn3, cv).reshape(n, width)


def _layer_weights(l, precise, norm1_g, w_in, shift_mu, w0, w2, a0, a2, g2, k_k, k_a, r_k,
                   lnx_g, lnx_b, w_out, norm2_g, w_rg, b_rg, w_re, b_re, w_eg, w_eu, w_ed,
                   w_ple, w_pleg):
    wd = F32 if precise else BF16
    row = lambda x: x.reshape(1, -1).astype(F32)
    wa3 = 3 * (w_out.shape[1] - w0.shape[1])
    rw_cols = w_in.shape[2] - wa3
    z_pad = -rw_cols % LANES
    w2p, a2p, g2p = _lora_padded(w2[l], a2[l], g2[l])
    n_rt = N_GROUPS + N_GROUPS * EXPERTS_PER_GROUP
    w_router = jnp.pad(jnp.concatenate([w_rg[l], w_re[l]], axis=1), ((0, 0), (0, LANES - n_rt)))
    b_router = jnp.pad(jnp.concatenate([b_rg[l], b_re[l]]), (0, LANES - n_rt))

    def by_group(w_e):
        _, d_in, f = w_e.shape
        w_g = w_e.reshape(N_GROUPS, EXPERTS_PER_GROUP, d_in, f)
        return jnp.transpose(w_g, (0, 2, 1, 3)).reshape(N_GROUPS, d_in, EXPERTS_PER_GROUP * f)

    return dict(
        norm1=row(norm1_g[l]),
        wqkv=w_in[l][:, :wa3].astype(wd),
        wz=jnp.pad(w_in[l][:, wa3:], ((0, 0), (0, z_pad))).astype(wd),
        mu=jnp.pad(shift_mu[l], (0, z_pad))[None],
        w0=row(w0[l]), w2p=w2p, a0=row(a0[l]), a2p=a2p, g2p=g2p,
        kk=row(k_k[l]), ka=row(k_a[l]), rk=row(r_k[l]), lg=row(lnx_g[l]), lb=row(lnx_b[l]),
        wo=w_out[l].astype(wd), norm2=row(norm2_g[l]), w_router=w_router, b_router=b_router[None],
        weg=by_group(w_eg[l]).astype(wd), weu=by_group(w_eu[l]).astype(wd),
        wed=w_ed[l].reshape(N_GROUPS, -1, w_ed.shape[3]).astype(wd),
        wple=w_ple[l].astype(wd), wgate=w_pleg[l].astype(wd),
        rw_cols=rw_cols, wa=wa3 // 3)


def _post_attention(x2d, att, rw, p2d, w, fg, *, precise, final):
    tm = min(x2d.shape[0], MOE_TOKEN_TILE)
    x1, h2, comb = _outproj_router(x2d, att, rw, w["wo"], w["norm2"], w["w_router"], w["b_router"],
                                   tm=tm, precise=precise)
    x2 = _moe(h2, comb, x1, w["weg"], w["weu"], w["wed"], tm=tm, precise=precise)
    return _ple(x2, p2d, w["wple"], w["wgate"], fg, tm=tm, precise=precise, final=final)


def kernel(x_prompt, x_sample, p_prompt, p_sample, cache_k, cache_v, page_table, state_wkv, state_shift, norm1_g, w_in, rel_bias, shift_mu, w0, w2, a0, a2, g2, k_k, k_a, r_k, lnx_g, lnx_b, w_out, norm2_g, w_rg, b_rg, w_re, b_re, w_eg, w_eu, w_ed, w_ple, w_pleg, final_g):
    depth = norm1_g.shape[0]
    b, s, d = x_prompt.shape
    n = x_sample.shape[0]
    assert x_sample.shape[1] == 1
    n_heads = state_wkv.shape[2]
    layer_params = (norm1_g, w_in, shift_mu, w0, w2, a0, a2, g2, k_k, k_a, r_k, lnx_g, lnx_b,
                    w_out, norm2_g, w_rg, b_rg, w_re, b_re, w_eg, w_eu, w_ed, w_ple, w_pleg)
    fg = final_g.reshape(1, -1)
    tab = _rel_bias_table(rel_bias, 2 * MOBA_BLOCK)
    bias_tiles = _bias_tiles(tab * LOG2_E)
    xp = x_prompt.reshape(b * s, d)
    xs = x_sample.reshape(n, d)
    outs = [[] for _ in range(8)]
    for l in range(depth):
        final = l == depth - 1
        w = _layer_weights(l, False, *layer_params)
        wa, rw_cols = w["wa"], w["rw_cols"]
        heads = lambda a, lead: a.reshape(lead + (wa // HEAD_DIM, HEAD_DIM))
        qk, k_t, v_t, z = _inproj_seq(xp, w["norm1"], w["wqkv"], w["wz"], b=b, tm=512)
        att = _moba_prompt(qk.reshape(b, s, 2 * wa), v_t, bias_tiles).reshape(b * s, wa)
        seq_major = lambda a: jnp.transpose(a.reshape(b, wa // HEAD_DIM, HEAD_DIM, s), (0, 3, 1, 2))
        z3 = z.reshape(b, s, -1)
        rw, st = _rwkv_prompt(z3, w["mu"], w["w0"], w["w2p"], w["a0"], w["a2p"], w["g2p"], w["kk"],
                              w["ka"], w["rk"], w["lg"], w["lb"], tc=RWKV_TOKEN_TILE)
        xp = _post_attention(xp, att, rw.reshape(b * s, -1), p_prompt[l].reshape(b * s, -1), w, fg,
                             precise=False, final=final)
        outs[0].append(seq_major(k_t))
        outs[1].append(seq_major(v_t))
        outs[4].append(_state_from_tiles(st))
        outs[6].append(z3[:, -1, :rw_cols])
        w = _layer_weights(l, True, *layer_params)
        qkv, z = _inproj(xs, w["norm1"], w["wqkv"], w["wz"], tm=n, precise=True)
        q_s, k_s, v_s = qkv[:, :wa], qkv[:, wa:2 * wa], qkv[:, 2 * wa:]
        att = _moba_sample(q_s, k_s, v_s, cache_k[l], cache_v[l], page_table, tab)
        z_prev = jnp.pad(state_shift[l], ((0, 0), (0, z.shape[1] - rw_cols)))
        per_tok = _rwkv_sample_prep(z, z_prev, w["mu"], w["w0"], w["w2p"], w["a0"], w["a2p"], w["g2p"],
                                    w["kk"], w["ka"])
        per_head = [a.reshape(n, n_heads, HEAD_DIM) for a in per_tok]
        hp = lambda a: a.reshape(n_heads, HEAD_DIM)
        st_s, rw_s = _rwkv_sample_step(state_wkv[l], per_head, hp(w["rk"]), hp(w["lg"]), hp(w["lb"]))
        xs = _post_attention(xs, att, rw_s.reshape(n, -1), p_sample[l].reshape(n, -1), w, fg,
                             precise=True, final=final)
        outs[2].append(heads(k_s, (n, 1)))
        outs[3].append(heads(v_s, (n, 1)))
        outs[5].append(st_s)
        outs[7].append(z[:, :rw_cols])
    kp, vp, ks, vs, wp, ws, sp, ss = [jnp.stack(o) for o in outs]
    return (xp.reshape(b, s, d), xs.reshape(n, 1, d), kp, vp, ks, vs, wp, ws, sp, ss)
```

```python
import functools
import math

import jax
import jax.numpy as jnp
from jax import lax
from jax.experimental import pallas as pl
from jax.experimental.pallas import tpu as pltpu

F32 = jnp.float32
BF16 = jnp.bfloat16
HIGHEST = lax.Precision.HIGHEST

HEAD_DIM = 64
MOBA_BLOCK = 256
MOBA_TOPK = 3
N_BUCKETS = 32
MAX_DISTANCE = 128
LORA_W = 64
LORA_A = 64
LORA_G = 160
N_GROUPS = 4
EXPERTS_PER_GROUP = 4
RMS_EPS = 1e-6
GN_EPS = 64e-5
NEG = -1e30
LOG2_E = math.log2(math.e)

LANES = 128
MXU_DIM = 256
VMEM_LIMIT_BYTES = 56 * 1024 * 1024

MOE_TOKEN_TILE = 1024
MOBA_HEAD_GROUP = 8
RWKV_CHUNK = 64
RWKV_TOKEN_TILE = 512
HEADS_PER_TILE = MXU_DIM // HEAD_DIM


def _cparams(n_axes):
    return pltpu.CompilerParams(dimension_semantics=("arbitrary",) * n_axes,
                                vmem_limit_bytes=VMEM_LIMIT_BYTES)


def _mm(a, b, precise):
    if precise:
        return jnp.dot(a.astype(F32), b.astype(F32), precision=HIGHEST, preferred_element_type=F32)
    return jnp.dot(a.astype(BF16), b.astype(BF16), preferred_element_type=F32)


def _mm_nt(a, b):
    return lax.dot_general(a.astype(BF16), b.astype(BF16), (((1,), (1,)), ((), ())),
                           preferred_element_type=F32)


def _mm_tn(a, b):
    return lax.dot_general(a.astype(BF16), b.astype(BF16), (((0,), (0,)), ((), ())),
                           preferred_element_type=F32)


def _mm_split(a_exact, x, terms):
    a16 = a_exact.astype(BF16)
    acc = None
    rem = x
    for _ in range(terms):
        piece = rem.astype(BF16)
        part = jnp.dot(a16, piece, preferred_element_type=F32)
        acc = part if acc is None else acc + part
        rem = rem - piece.astype(F32)
    return acc


def _mm_split_rhs(x, b_exact, terms):
    b16 = b_exact.astype(BF16)
    acc = None
    rem = x
    for _ in range(terms):
        piece = rem.astype(BF16)
        part = jnp.dot(piece, b16, preferred_element_type=F32)
        acc = part if acc is None else acc + part
        rem = rem - piece.astype(F32)
    return acc


def _sigmoid(x):
    return 1.0 / (1.0 + jnp.exp(-x))


def _softplus(x):
    return jnp.maximum(x, 0.0) + jnp.log(1.0 + jnp.exp(-jnp.abs(x)))


def _rms_norm(x, g):
    return x * lax.rsqrt(jnp.mean(x * x, axis=-1, keepdims=True) + RMS_EPS) * g


def _inproj_kernel(x_ref, g_ref, wqkv_ref, wz_ref, qkv_ref, z_ref, *, precise):
    h = _rms_norm(x_ref[...], g_ref[...])
    qkv_ref[...] = _mm(h, wqkv_ref[...], precise)
    z_ref[...] = _mm(h, wz_ref[...], precise)


def _inproj(x2d, g, wqkv, wz, *, tm, precise):
    t, d = x2d.shape
    nq, nz = wqkv.shape[1], wz.shape[1]
    return pl.pallas_call(
        functools.partial(_inproj_kernel, precise=precise),
        grid=(t // tm,),
        in_specs=[pl.BlockSpec((tm, d), lambda i: (i, 0)),
                  pl.BlockSpec((1, d), lambda i: (0, 0)),
                  pl.BlockSpec((d, nq), lambda i: (0, 0)),
                  pl.BlockSpec((d, nz), lambda i: (0, 0))],
        out_specs=[pl.BlockSpec((tm, nq), lambda i: (i, 0)),
                   pl.BlockSpec((tm, nz), lambda i: (i, 0))],
        out_shape=[jax.ShapeDtypeStruct((t, nq), F32), jax.ShapeDtypeStruct((t, nz), F32)],
        compiler_params=_cparams(1),
    )(x2d, g, wqkv, wz)


def _inproj_seq_kernel(x_ref, g_ref, wqkv_ref, wz_ref, qk_ref, kt_ref, vt_ref, z_ref):
    wa = kt_ref.shape[1]
    h = _rms_norm(x_ref[...], g_ref[...])
    qkv = _mm(h, wqkv_ref[...], False)
    qk_ref[...] = qkv[:, :2 * wa]
    kt_ref[0] = qkv[:, wa:2 * wa].T
    vt_ref[0] = qkv[:, 2 * wa:].T
    z_ref[...] = _mm(h, wz_ref[...], False)


def _inproj_seq(x2d, g, wqkv, wz, *, b, tm):
    t, d = x2d.shape
    s = t // b
    nq, nz = wqkv.shape[1], wz.shape[1]
    wa = nq // 3
    per_seq = s // tm
    feat_major = pl.BlockSpec((1, wa, tm), lambda i: (i // per_seq, 0, i % per_seq))
    return pl.pallas_call(
        _inproj_seq_kernel,
        grid=(t // tm,),
        in_specs=[pl.BlockSpec((tm, d), lambda i: (i, 0)),
                  pl.BlockSpec((1, d), lambda i: (0, 0)),
                  pl.BlockSpec((d, nq), lambda i: (0, 0)),
                  pl.BlockSpec((d, nz), lambda i: (0, 0))],
        out_specs=[pl.BlockSpec((tm, 2 * wa), lambda i: (i, 0)), feat_major, feat_major,
                   pl.BlockSpec((tm, nz), lambda i: (i, 0))],
        out_shape=[jax.ShapeDtypeStruct((t, 2 * wa), F32), jax.ShapeDtypeStruct((b, wa, s), F32),
                   jax.ShapeDtypeStruct((b, wa, s), F32), jax.ShapeDtypeStruct((t, nz), F32)],
        compiler_params=_cparams(1),
    )(x2d, g, wqkv, wz)


def _moba_prompt_kernel(q_ref, k_ref, vt_ref, tt_ref, o_ref,
                        kbf_s, vt_s, kmean_s, m_s, l_s, acc_s, sel_s, qbf_s):
    blk = MOBA_BLOCK
    i = pl.program_id(1)
    nb = kbf_s.shape[0]
    n_heads = tt_ref.shape[0]
    scale = HEAD_DIM ** -0.5 * LOG2_E

    @pl.when(i == 0)
    def _():
        for n in range(nb):
            kb = k_ref[0, n * blk:(n + 1) * blk, :]
            kbf_s[n] = kb.astype(BF16)
            kmean_s[n:n + 1, :] = jnp.mean(kb, axis=0, keepdims=True)
            vt_s[n] = vt_ref[0, :, n * blk:(n + 1) * blk].astype(BF16)

    q_t = q_ref[0].T
    blk_row = lax.broadcasted_iota(jnp.int32, (nb, blk), 0)
    key_idx = lax.broadcasted_iota(jnp.int32, (blk, blk), 0)
    qry_idx = lax.broadcasted_iota(jnp.int32, (blk, blk), 1)
    causal = key_idx <= qry_idx
    upper_half = lax.broadcasted_iota(jnp.int32, (LANES, blk), 0) >= HEAD_DIM

    for h in range(n_heads):
        lanes = slice((h // 2) * LANES, (h // 2 + 1) * LANES)
        q_h = jnp.where(upper_half == (h % 2 == 1), q_t[lanes, :], 0.0)
        gate = _mm(kmean_s[:, lanes], q_h, False)
        gate = jnp.where(blk_row < i, gate, NEG)
        rank = jnp.zeros((nb, blk), jnp.int32)
        for m in range(nb):
            gm = gate[m:m + 1, :]
            beats = (gm > gate) | ((gm == gate) & (m < blk_row))
            rank = rank + beats.astype(jnp.int32)
        sel_s[h] = jnp.where((rank < MOBA_TOPK) & (blk_row < i), 1.0, 0.0)
        qbf_s[h] = (q_h * scale).astype(BF16)
    m_s[...] = jnp.full(m_s.shape, NEG, F32)
    l_s[...] = jnp.zeros(l_s.shape, F32)
    acc_s[...] = jnp.zeros(acc_s.shape, F32)

    def attend(group, n, bias_of, mask_of, uniform_bias=False):
        scores = [jnp.dot(kbf_s[n, :, (h // 2) * LANES:(h // 2 + 1) * LANES], qbf_s[h],
                          preferred_element_type=F32) for h in group]
        probs, alphas = [], []
        for h, s in zip(group, scores):
            m_old = m_s[h:h + 1, :]
            if uniform_bias:
                bias = bias_of(h)
                s = jnp.where(mask_of(h), s, NEG)
                m_new = jnp.maximum(m_old, jnp.max(s, axis=0, keepdims=True) + bias)
                p = jnp.exp2(s - (m_new - bias))
            else:
                s = jnp.where(mask_of(h), s + bias_of(h), NEG)
                m_new = jnp.maximum(m_old, jnp.max(s, axis=0, keepdims=True))
                p = jnp.exp2(s - m_new)
            alpha = jnp.exp2(m_old - m_new)
            l_s[h:h + 1, :] = alpha * l_s[h:h + 1, :] + jnp.sum(p, axis=0, keepdims=True)
            m_s[h:h + 1, :] = m_new
            probs.append(p.astype(BF16))
            alphas.append(alpha)
        pvs = [jnp.dot(vt_s[n, h * HEAD_DIM:(h + 1) * HEAD_DIM, :], p, preferred_element_type=F32)
               for h, p in zip(group, probs)]
        for h, alpha, pv in zip(group, alphas, pvs):
            acc_s[h] = alpha * acc_s[h] + pv

    for g0 in range(0, n_heads, MOBA_HEAD_GROUP):
        group = range(g0, g0 + MOBA_HEAD_GROUP)
        attend(group, i, lambda h: tt_ref[h, :, blk:2 * blk], lambda h: causal)

        @pl.when(i >= 1)
        def _(group=group):
            n = i - 1
            attend(group, n, lambda h: tt_ref[h, :, 0:blk], lambda h: sel_s[h, pl.ds(n, 1), :] > 0.5)

        def far_body(n, carry, group=group):
            attend(group, n, lambda h: tt_ref[h, 0:1, blk - 1:blk],
                   lambda h: sel_s[h, pl.ds(n, 1), :] > 0.5, uniform_bias=True)
            return carry

        lax.fori_loop(0, jnp.maximum(i - 1, 0), far_body, 0)

    for pair in range(n_heads // 2):
        outs = [acc_s[h] / l_s[h:h + 1, :] for h in (2 * pair, 2 * pair + 1)]
        o_ref[0, :, pair * LANES:(pair + 1) * LANES] = jnp.concatenate(outs, axis=0).T


def _moba_prompt(qk3, vt3, tt):
    b, s, w2 = qk3.shape
    w = w2 // 2
    blk = MOBA_BLOCK
    nb = s // blk
    n_heads = w // HEAD_DIM
    return pl.pallas_call(
        _moba_prompt_kernel,
        grid=(b, nb),
        in_specs=[pl.BlockSpec((1, blk, w), lambda bi, i: (bi, i, 0)),
                  pl.BlockSpec((1, s, w), lambda bi, i: (bi, 0, 1)),
                  pl.BlockSpec((1, w, s), lambda bi, i: (bi, 0, 0)),
                  pl.BlockSpec((n_heads, blk, 2 * blk), lambda bi, i: (0, 0, 0))],
        out_specs=pl.BlockSpec((1, blk, w), lambda bi, i: (bi, i, 0)),
        out_shape=jax.ShapeDtypeStruct((b, s, w), F32),
        scratch_shapes=[pltpu.VMEM((nb, blk, w), BF16),
                        pltpu.VMEM((nb, w, blk), BF16),
                        pltpu.VMEM((nb, w), F32),
                        pltpu.VMEM((n_heads, blk), F32),
                        pltpu.VMEM((n_heads, blk), F32),
                        pltpu.VMEM((n_heads, HEAD_DIM, blk), F32),
                        pltpu.VMEM((n_heads, nb, blk), F32),
                        pltpu.VMEM((n_heads, LANES, blk), BF16)],
        compiler_params=_cparams(2),
    )(qk3, qk3, vt3, tt)


def _rel_bias_table(rel_bias, n_dist):
    n = jnp.arange(n_dist, dtype=jnp.int32)
    max_exact = N_BUCKETS // 2
    nf = jnp.maximum(n, 1).astype(F32)
    large = max_exact + (jnp.log(nf / max_exact) / math.log(MAX_DISTANCE / max_exact)
                         * (N_BUCKETS - max_exact)).astype(jnp.int32)
    large = jnp.minimum(large, N_BUCKETS - 1)
    bucket = jnp.where(n < max_exact, n, large)
    return rel_bias.astype(F32).T[:, bucket]


def _bias_tiles(tab):
    blk = MOBA_BLOCK
    period = 2 * blk
    n_heads = tab.shape[0]
    base = jnp.concatenate([tab[:, blk:period], tab[:, 0:blk]], axis=1)
    stride = 2 * period - 1
    reps = -(-(blk * stride) // period)
    seq = jnp.tile(base, (1, reps))[:, :blk * stride]
    return seq.reshape(n_heads, blk, stride)[:, :, :period]


def _mm3(a, b):
    a_hi = a.astype(BF16)
    b_hi = b.astype(BF16)
    a_lo = (a - a_hi.astype(F32)).astype(BF16)
    b_lo = (b - b_hi.astype(F32)).astype(BF16)
    return (jnp.dot(a_hi, b_hi, preferred_element_type=F32)
            + jnp.dot(a_hi, b_lo, preferred_element_type=F32)
            + jnp.dot(a_lo, b_hi, preferred_element_type=F32))


def _rwkv_prompt_kernel(zr_ref, zk_ref, zv_ref, zl_ref,
                        mur_ref, muk_ref, muv_ref, mul_ref,
                        w0_ref, w2_ref, a0_ref, a2_ref, g2_ref,
                        kk_ref, ka_ref, rk_ref, lg_ref, lb_ref,
                        rw_ref, st_ref,
                        p_s, last_r, last_k, last_v, last_l):
    t = pl.program_id(2)
    tc = zr_ref.shape[1]
    w = MXU_DIM
    c_len = RWKV_CHUNK
    hd_shift = HEAD_DIM.bit_length() - 1

    @pl.when(t == 0)
    def _():
        p_s[...] = jnp.zeros(p_s.shape, F32)
        last_r[...] = jnp.zeros(last_r.shape, F32)
        last_k[...] = jnp.zeros(last_k.shape, F32)
        last_v[...] = jnp.zeros(last_v.shape, F32)
        last_l[...] = jnp.zeros(last_l.shape, F32)

    def token_shift_mix(z_ref, last_ref, mu_ref):
        z = z_ref[0]
        row = lax.broadcasted_iota(jnp.int32, z.shape, 0)
        prev = jnp.where(row == 0, last_ref[...], pltpu.roll(z, 1, 0))
        last_ref[...] = z[tc - 1:tc, :]
        return z + (prev - z) * mu_ref[...]

    ri = lax.broadcasted_iota(jnp.int32, (w, w), 0)
    ci = lax.broadcasted_iota(jnp.int32, (w, w), 1)
    same_head = (ri >> hd_shift) == (ci >> hd_shift)
    tok_r = ri & (HEAD_DIM - 1)
    tok_c = ci & (HEAD_DIM - 1)
    strict = same_head & (tok_r > tok_c)
    incl = same_head & (tok_r >= tok_c)
    eye = ri == ci
    head_ones = jnp.where(same_head, 1.0, 0.0).astype(BF16)
    tri = jnp.where(lax.broadcasted_iota(jnp.int32, (c_len, c_len), 0)
                    >= lax.broadcasted_iota(jnp.int32, (c_len, c_len), 1), 1.0, 0.0).astype(BF16)

    def head_sum(x):
        return _mm_split_rhs(x, head_ones, 2)

    def stack(x):
        return jnp.where(same_head, jnp.concatenate([x] * HEADS_PER_TILE, axis=0), 0.0)

    def unstack(x):
        out = x[0:c_len]
        for e in range(1, HEADS_PER_TILE):
            out = out + x[e * c_len:(e + 1) * c_len]
        return out

    r = token_shift_mix(zr_ref, last_r, mur_ref)
    k = token_shift_mix(zk_ref, last_k, muk_ref)
    v = token_shift_mix(zv_ref, last_v, muv_ref)
    zl = token_shift_mix(zl_ref, last_l, mul_ref)

    n_wa = LORA_W + LORA_A
    zl_wa = zl[:, :n_wa]
    w_pre = w0_ref[...] + _mm(jnp.tanh(zl_wa), w2_ref[0:n_wa, :], False)
    logd = -jnp.exp(-_softplus(-w_pre) - 0.5)
    eta = _sigmoid(a0_ref[...] + _mm(zl_wa, a2_ref[0:n_wa, :], False))
    gate = _mm(_sigmoid(zl[:, n_wa:]), g2_ref[n_wa:, :], False)
    kk = k * kk_ref[...]
    kk = kk * lax.rsqrt(jnp.maximum(head_sum(kk * kk), 1e-24))
    kh = k * (1.0 + (eta - 1.0) * ka_ref[...])
    bonus = head_sum(r * kh * rk_ref[...]) * v
    kb = kk * eta

    chunks = range(tc // c_len)
    each = lambda f, *lists: [f(*(x[c] for x in lists)) for c in chunks]
    rows = [slice(c * c_len, (c + 1) * c_len) for c in chunks]
    ld = [logd[rw] for rw in rows]
    cum = each(lambda x: _mm_split(tri, x, 3), ld)
    cum_end = each(lambda x: x[c_len - 1:c_len, :], cum)
    e_inv = each(lambda x: jnp.exp(-x), cum)
    e_rem = each(lambda x, xe: jnp.exp(xe - x), cum, cum_end)
    a_t = [stack(-kk[rows[c]] * jnp.exp(cum[c] - ld[c])) for c in chunks]
    b_t = [stack(kb[rows[c]] * e_inv[c]) for c in chunks]
    k_t = [stack(kh[rows[c]] * e_inv[c]) for c in chunks]
    r_t = [stack(r[rows[c]] * jnp.exp(cum[c])) for c in chunks]
    b_h = [stack(kb[rows[c]] * e_rem[c]) for c in chunks]
    k_h = [stack(kh[rows[c]] * e_rem[c]) for c in chunks]
    v_s = [stack(v[rows[c]]) for c in chunks]

    prod = each(lambda a, rr, bb, kt: _mm_nt(jnp.concatenate([a, rr], axis=0),
                                             jnp.concatenate([bb, kt], axis=0)), a_t, r_t, b_t, k_t)
    n_mat = each(lambda x: jnp.where(strict, x[0:w, 0:w], 0.0), prod)
    a_k = each(lambda x: jnp.where(strict, x[0:w, w:2 * w], 0.0), prod)
    m_rb = each(lambda x: jnp.where(incl, x[w:2 * w, 0:w], 0.0), prod)
    m_rk = each(lambda x: jnp.where(incl, x[w:2 * w, w:2 * w], 0.0), prod)

    t_inv = each(lambda x: jnp.where(eye, 1.0, 0.0) + x, n_mat)
    n_pow = n_mat
    for _ in range(c_len.bit_length() - 2):
        n_pow = each(lambda x: _mm(x, x, False), n_pow)
        t_inv = each(lambda x, y: x + _mm(x, y, False), t_inv, n_pow)

    a_p = each(lambda x, y: _mm(x, y, False), t_inv, a_t)
    a_v = each(lambda x, y: _mm(x, y, False), a_k, v_s)
    v_p = each(lambda x, y: _mm(x, y, False), t_inv, a_v)
    r_p = each(lambda rr, m, ap: unstack(rr + _mm(m, ap, False)), r_t, m_rb, a_p)
    y_0 = each(lambda mb, vp, mk, vs: unstack(_mm(mb, vp, False) + _mm(mk, vs, False)),
               m_rb, v_p, m_rk, v_s)
    decay = each(jnp.exp, cum_end)
    mix = each(_mm_tn, a_p, b_h)
    add = each(lambda vp, bh, vs, kh_: _mm_tn(vp, bh) + _mm_tn(vs, kh_), v_p, b_h, v_s, k_h)

    st = p_s[...]
    ys = []
    for c in chunks:
        ys.append(_mm_nt(r_p[c], st) + y_0[c])
        st = st * decay[c] + _mm(st, mix[c], False) + add[c]
    p_s[...] = st
    st_ref[0, 0] = st

    y = jnp.concatenate(ys, axis=0)
    mean = head_sum(y) * (1.0 / HEAD_DIM)
    yc = y - mean
    var = head_sum(yc * yc) * (1.0 / HEAD_DIM)
    yn = yc * lax.rsqrt(var + GN_EPS) * lg_ref[...] + lb_ref[...]
    rw_ref[0] = (yn + bonus) * gate


def _rwkv_prompt(z3, mu, w0, w2p, a0, a2p, g2p, kk, ka, rk, lg, lb, *, tc):
    b, s, _ = z3.shape
    w = MXU_DIM
    wr = w0.shape[1]
    nt = wr // w
    lw = w2p.shape[0]
    lora_blk = (3 * wr) // lw
    vec = lambda off: pl.BlockSpec((1, w), lambda bi, q, t, off=off: (0, off + q))
    zblk = lambda off: pl.BlockSpec((1, tc, w), lambda bi, q, t, off=off: (bi, t, off + q))
    lora_w = pl.BlockSpec((lw, w), lambda bi, q, t: (0, q))
    return pl.pallas_call(
        _rwkv_prompt_kernel,
        grid=(b, nt, s // tc),
        in_specs=[zblk(0), zblk(nt), zblk(2 * nt),
                  pl.BlockSpec((1, tc, lw), lambda bi, q, t: (bi, t, lora_blk)),
                  vec(0), vec(nt), vec(2 * nt),
                  pl.BlockSpec((1, lw), lambda bi, q, t: (0, lora_blk)),
                  vec(0), lora_w, vec(0), lora_w, lora_w,
                  vec(0), vec(0), vec(0), vec(0), vec(0)],
        out_specs=[pl.BlockSpec((1, tc, w), lambda bi, q, t: (bi, t, q)),
                   pl.BlockSpec((1, 1, w, w), lambda bi, q, t: (bi, q, 0, 0))],
        out_shape=[jax.ShapeDtypeStruct((b, s, wr), F32),
                   jax.ShapeDtypeStruct((b, nt, w, w), F32)],
        scratch_shapes=[pltpu.VMEM((w, w), F32),
                        pltpu.VMEM((1, w), F32), pltpu.VMEM((1, w), F32),
                        pltpu.VMEM((1, w), F32), pltpu.VMEM((1, lw), F32)],
        compiler_params=_cparams(3),
    )(z3, z3, z3, z3, mu, mu, mu, mu, w0, w2p, a0, a2p, g2p, kk, ka, rk, lg, lb)


def _state_from_tiles(st):
    b, nt = st.shape[:2]
    heads = []
    for q in range(nt):
        for e in range(HEADS_PER_TILE):
            sl = slice(e * HEAD_DIM, (e + 1) * HEAD_DIM)
            heads.append(st[:, q, sl, sl])
    return jnp.stack(heads, axis=1)


def _lora_padded(w2, a2, g2):
    lw = 3 * LANES
    n = w2.shape[1]
    o_a = LORA_W
    o_g = LORA_W + LORA_A
    w2p = jnp.zeros((lw, n), F32).at[0:o_a].set(w2)
    a2p = jnp.zeros((lw, n), F32).at[o_a:o_g].set(a2)
    g2p = jnp.zeros((lw, n), F32).at[o_g:o_g + LORA_G].set(g2)
    return w2p, a2p, g2p


ROUTER_LANE0 = N_GROUPS


def _first_argmax(x, lane):
    mx = jnp.max(x, axis=1, keepdims=True)
    idx = jnp.min(jnp.where(x == mx, lane, LANES), axis=1, keepdims=True)
    return mx, idx


def _outproj_router_kernel(x_ref, att_ref, rw_ref, wo_ref, g_ref, wr_ref, br_ref,
                           x1_ref, h2_ref, comb_ref, *, precise):
    wa = att_ref.shape[1]
    x1 = (x_ref[...] + _mm(att_ref[...], wo_ref[0:wa, :], precise)
          + _mm(rw_ref[...], wo_ref[wa:, :], precise))
    x1_ref[...] = x1
    h2 = _rms_norm(x1, g_ref[...])
    h2_ref[...] = h2.astype(h2_ref.dtype)
    logits = _mm(h2, wr_ref[...], precise) + br_ref[...]
    n_exp = N_GROUPS * EXPERTS_PER_GROUP
    lane = lax.broadcasted_iota(jnp.int32, logits.shape, 1)
    is_grp = lane < N_GROUPS
    gl = jnp.where(is_grp, logits, NEG)
    g_max, grp = _first_argmax(gl, lane)
    p_grp = 1.0 / jnp.sum(jnp.where(is_grp, jnp.exp(gl - g_max), 0.0), axis=1, keepdims=True)
    e_lane = lane - ROUTER_LANE0
    grp_shift = EXPERTS_PER_GROUP.bit_length() - 1
    in_grp = (e_lane >= 0) & (e_lane < n_exp) & ((e_lane >> grp_shift) == grp)
    el = jnp.where(in_grp, logits, NEG)
    m1, i1 = _first_argmax(el, lane)
    el2 = jnp.where(lane == i1, NEG, el)
    m2, i2 = _first_argmax(el2, lane)
    ratio = jnp.exp(m2 - m1)
    p1 = 1.0 / (1.0 + ratio)
    p2 = ratio / (1.0 + ratio)
    comb_ref[...] = p_grp * (jnp.where(lane == i1, p1, 0.0) + jnp.where(lane == i2, p2, 0.0))


def _outproj_router(x2d, att, rw, wo, g, wr, br, *, tm, precise):
    t, d = x2d.shape
    wa, wr_cols = att.shape[1], rw.shape[1]
    h_dtype = F32 if precise else BF16
    tok = lambda n: pl.BlockSpec((tm, n), lambda i: (i, 0))
    full = lambda a: pl.BlockSpec(a.shape, lambda i: (0,) * a.ndim)
    return pl.pallas_call(
        functools.partial(_outproj_router_kernel, precise=precise),
        grid=(t // tm,),
        in_specs=[tok(d), tok(wa), tok(wr_cols), full(wo), full(g), full(wr), full(br)],
        out_specs=[tok(d), tok(d), tok(LANES)],
        out_shape=[jax.ShapeDtypeStruct((t, d), F32), jax.ShapeDtypeStruct((t, d), h_dtype),
                   jax.ShapeDtypeStruct((t, LANES), F32)],
        compiler_params=_cparams(1),
    )(x2d, att, rw, wo, g, wr, br)


def _moe_kernel(h2_ref, comb_ref, x1_ref, wg_ref, wu_ref, wd_ref, x2_ref, *, precise):
    g = pl.program_id(1)

    @pl.when(g == 0)
    def _():
        x2_ref[...] = x1_ref[...]

    h2 = h2_ref[...]
    comb = comb_ref[...]
    lane = lax.broadcasted_iota(jnp.int32, comb.shape, 1)
    gates = [_mm(h2, wg_ref[j], precise) for j in range(EXPERTS_PER_GROUP)]
    ups = [_mm(h2, wu_ref[j], precise) for j in range(EXPERTS_PER_GROUP)]
    acts = []
    for j in range(EXPERTS_PER_GROUP):
        e_lane = ROUTER_LANE0 + g * EXPERTS_PER_GROUP + j
        weight = jnp.sum(jnp.where(lane == e_lane, comb, 0.0), axis=1, keepdims=True)
        acts.append(gates[j] * _sigmoid(gates[j]) * ups[j] * weight)
    x2_ref[...] += _mm(jnp.concatenate(acts, axis=1), wd_ref[0], precise)


def _moe(h2, comb, x1, wg, wu, wd, *, tm, precise):
    t, d = x1.shape
    n_grp, gf, _ = wd.shape
    epg = EXPERTS_PER_GROUP
    f = gf // epg
    return pl.pallas_call(
        functools.partial(_moe_kernel, precise=precise),
        grid=(t // tm, n_grp),
        in_specs=[pl.BlockSpec((tm, d), lambda i, g: (i, 0)),
                  pl.BlockSpec((tm, LANES), lambda i, g: (i, 0)),
                  pl.BlockSpec((tm, d), lambda i, g: (i, 0)),
                  pl.BlockSpec((epg, d, f), lambda i, g: (g, 0, 0)),
                  pl.BlockSpec((epg, d, f), lambda i, g: (g, 0, 0)),
                  pl.BlockSpec((1, gf, d), lambda i, g: (g, 0, 0))],
        out_specs=pl.BlockSpec((tm, d), lambda i, g: (i, 0)),
        out_shape=jax.ShapeDtypeStruct((t, d), F32),
        compiler_params=_cparams(2),
    )(h2, comb, x1, wg, wu, wd)


def _ple_kernel(x_ref, p_ref, wple_ref, wgate_ref, fg_ref, y_ref, *, precise, final):
    x = x_ref[...]
    emb = _mm(p_ref[...], wple_ref[...], precise)
    x = x + emb * _sigmoid(_mm(x, wgate_ref[...], precise))
    y_ref[...] = _rms_norm(x, fg_ref[...]) if final else x


def _ple(x2d, p2d, wple, wgate, fg, *, tm, precise, final):
    t, d = x2d.shape
    pd = p2d.shape[1]
    full = lambda a: pl.BlockSpec(a.shape, lambda i: (0,) * a.ndim)
    return pl.pallas_call(
        functools.partial(_ple_kernel, precise=precise, final=final),
        grid=(t // tm,),
        in_specs=[pl.BlockSpec((tm, d), lambda i: (i, 0)), pl.BlockSpec((tm, pd), lambda i: (i, 0)),
                  full(wple), full(wgate), full(fg)],
        out_specs=pl.BlockSpec((tm, d), lambda i: (i, 0)),
        out_shape=jax.ShapeDtypeStruct((t, d), F32),
        compiler_params=_cparams(1),
    )(x2d, p2d, wple, wgate, fg)


def _rwkv_sample_prep_kernel(z_ref, zp_ref, mu_ref, w0_ref, w2_ref, a0_ref, a2_ref, g2_ref,
                             kk_ref, ka_ref,
                             r_ref, d_ref, kkn_ref, kb_ref, kh_ref, v_ref, g_ref):
    wr = w0_ref.shape[1]
    hd_shift = HEAD_DIM.bit_length() - 1
    z = z_ref[...]
    zs = z + (zp_ref[...] - z) * mu_ref[...]
    r = zs[:, 0:wr]
    k = zs[:, wr:2 * wr]
    v = zs[:, 2 * wr:3 * wr]
    zl = zs[:, 3 * wr:]
    w_pre = w0_ref[...] + _mm(jnp.tanh(zl), w2_ref[...], False)
    decay = jnp.exp(-jnp.exp(-_softplus(-w_pre) - 0.5))
    eta = _sigmoid(a0_ref[...] + _mm(zl, a2_ref[...], False))
    ri = lax.broadcasted_iota(jnp.int32, (wr, wr), 0)
    ci = lax.broadcasted_iota(jnp.int32, (wr, wr), 1)
    head_ones = jnp.where((ri >> hd_shift) == (ci >> hd_shift), 1.0, 0.0)
    kk = k * kk_ref[...]
    kk = kk * lax.rsqrt(jnp.maximum(_mm(kk * kk, head_ones, True), 1e-24))
    r_ref[...] = r
    d_ref[...] = decay
    kkn_ref[...] = kk
    kb_ref[...] = kk * eta
    kh_ref[...] = k * (1.0 + (eta - 1.0) * ka_ref[...])
    v_ref[...] = v
    g_ref[...] = _mm(_sigmoid(zl), g2_ref[...], False)


def _rwkv_sample_prep(z, zp, mu, w0, w2p, a0, a2p, g2p, kk, ka):
    n = z.shape[0]
    wr = w0.shape[1]
    args = (z, zp, mu, w0, w2p, a0, a2p, g2p, kk, ka)
    full = lambda a: pl.BlockSpec(a.shape, lambda i: (0,) * a.ndim)
    return pl.pallas_call(
        _rwkv_sample_prep_kernel,
        grid=(1,),
        in_specs=[full(a) for a in args],
        out_specs=[pl.BlockSpec((n, wr), lambda i: (0, 0))] * 7,
        out_shape=[jax.ShapeDtypeStruct((n, wr), F32)] * 7,
        compiler_params=_cparams(1),
    )(*args)


def _rwkv_sample_step_kernel(s_ref, r_ref, d_ref, kkn_ref, kb_ref, kh_ref, v_ref, g_ref,
                             rk_ref, lg_ref, lb_ref, so_ref, rw_ref):
    n_heads = s_ref.shape[1]
    eye = (lax.broadcasted_iota(jnp.int32, (HEAD_DIM, HEAD_DIM), 0)
           == lax.broadcasted_iota(jnp.int32, (HEAD_DIM, HEAD_DIM), 1))
    for h in range(n_heads):
        hrow = lambda ref, lead=True: ref[0, h:h + 1, :] if lead else ref[h:h + 1, :]
        st = s_ref[0, h]
        r, v, kh = hrow(r_ref), hrow(v_ref), hrow(kh_ref)
        sa = jnp.sum(st * (-hrow(kkn_ref)), axis=1, keepdims=True)
        v_col = jnp.sum(jnp.where(eye, v, 0.0), axis=1, keepdims=True)
        st = st * hrow(d_ref) + sa * hrow(kb_ref) + v_col * kh
        so_ref[0, h] = st
        y_col = jnp.sum(st * r, axis=1, keepdims=True)
        y = jnp.sum(jnp.where(eye, y_col, 0.0), axis=0, keepdims=True)
        mean = jnp.mean(y, axis=1, keepdims=True)
        yc = y - mean
        var = jnp.mean(yc * yc, axis=1, keepdims=True)
        yn = yc * lax.rsqrt(var + GN_EPS) * hrow(lg_ref, False) + hrow(lb_ref, False)
        bonus = jnp.sum(r * kh * hrow(rk_ref, False), axis=1, keepdims=True) * v
        rw_ref[0, h:h + 1, :] = (yn + bonus) * hrow(g_ref)


def _rwkv_sample_step(state, per_head, rk, lg, lb):
    n, nh = state.shape[:2]
    tok = pl.BlockSpec((1, nh, HEAD_DIM), lambda i: (i, 0, 0))
    par = pl.BlockSpec((nh, HEAD_DIM), lambda i: (0, 0))
    st = pl.BlockSpec((1, nh, HEAD_DIM, HEAD_DIM), lambda i: (i, 0, 0, 0))
    return pl.pallas_call(
        _rwkv_sample_step_kernel,
        grid=(n,),
        in_specs=[st] + [tok] * 7 + [par] * 3,
        out_specs=[st, tok],
        out_shape=[jax.ShapeDtypeStruct(state.shape, F32),
                   jax.ShapeDtypeStruct((n, nh, HEAD_DIM), F32)],
        compiler_params=_cparams(1),
    )(state, *per_head, rk, lg, lb)


SELECT_SEQS_PER_STEP = 8
PAGES_PER_STEP = 32


def _moba_sample_score_kernel(pt_ref, qc_ref, *refs):
    pps = PAGES_PER_STEP
    k_refs, s_ref, gate_ref = refs[:pps], refs[pps], refs[pps + 1]
    n_heads, _, page = k_refs[0].shape[1:]
    pages_per_blk = MOBA_BLOCK // page
    blks_per_step = pps // pages_per_blk
    j = pl.program_id(1)
    qc = qc_ref[0]
    lane = lax.broadcasted_iota(jnp.int32, (n_heads, LANES), 1)

    @pl.when(j == 0)
    def _():
        gate_ref[0] = jnp.zeros(gate_ref.shape[1:], F32)

    gate = gate_ref[0]
    for blk_i in range(blks_per_step):
        blk_sum = jnp.zeros((n_heads, page), F32)
        for half in range(pages_per_blk):
            i = blk_i * pages_per_blk + half
            s = jnp.sum(k_refs[i][0] * qc, axis=1)
            s_ref[0, :, i * page:(i + 1) * page] = s
            blk_sum = blk_sum + s
        g_col = jnp.sum(blk_sum, axis=1, keepdims=True) * (1.0 / MOBA_BLOCK)
        gate = jnp.where(lane == j * blks_per_step + blk_i, g_col, gate)
    gate_ref[0] = gate


def _moba_sample_scores(page_table, q_cols, cache_t):
    n, n_pages = page_table.shape
    n_heads, hd, page = cache_t.shape[1:]
    pps = PAGES_PER_STEP
    k_spec = lambda i: pl.BlockSpec((1, n_heads, hd, page),
                                    lambda b, j, pt, i=i: (pt[b, j * pps + i], 0, 0, 0))
    grid_spec = pltpu.PrefetchScalarGridSpec(
        num_scalar_prefetch=1,
        grid=(n, n_pages // pps),
        in_specs=[pl.BlockSpec((1, n_heads, hd, page), lambda b, j, pt: (b, 0, 0, 0))]
                 + [k_spec(i) for i in range(pps)],
        out_specs=[pl.BlockSpec((1, n_heads, pps * page), lambda b, j, pt: (b, 0, j)),
                   pl.BlockSpec((1, n_heads, LANES), lambda b, j, pt: (b, 0, 0))])
    return pl.pallas_call(
        _moba_sample_score_kernel,
        grid_spec=grid_spec,
        out_shape=[jax.ShapeDtypeStruct((n, n_heads, n_pages * page), F32),
                   jax.ShapeDtypeStruct((n, n_heads, LANES), F32)],
        compiler_params=_cparams(2),
    )(page_table, q_cols, *([cache_t] * pps))


def _moba_sample_select_kernel(s_ref, gate_ref, bias_ref, q_ref, kn_ref, b0_ref,
                               p_ref, pown_ref, sel_ref, member_s):
    n_seq, n_heads, n_keys = s_ref.shape
    rows = n_seq * n_heads
    blk_shift = MOBA_BLOCK.bit_length() - 1
    n_blk = n_keys // MOBA_BLOCK
    scale = HEAD_DIM ** -0.5
    s = s_ref[...].reshape(rows, n_keys)

    @pl.when(pl.program_id(0) == 0)
    def _():
        member_s[...] = jnp.where(
            lax.broadcasted_iota(jnp.int32, (LANES, n_keys), 1) >> blk_shift
            == lax.broadcasted_iota(jnp.int32, (LANES, n_keys), 0), 1.0, 0.0).astype(BF16)

    lane = lax.broadcasted_iota(jnp.int32, (rows, LANES), 1)
    gate = jnp.where(lane < n_blk, gate_ref[...].reshape(rows, LANES), NEG)
    sel = jnp.zeros((rows, LANES), F32)
    for _ in range(MOBA_TOPK):
        _, pick = _first_argmax(gate, lane)
        hit = lane == pick
        sel = jnp.where(hit, 1.0, sel)
        gate = jnp.where(hit, NEG, gate)
    sel_ref[...] = sel.reshape(n_seq, n_heads, LANES)
    key_sel = _mm(sel, member_s[...], False)
    sc = jnp.where(key_sel > 0.5, s * scale + bias_ref[...], NEG)
    q = q_ref[...].reshape(rows, HEAD_DIM)
    k_new = kn_ref[...].reshape(rows, HEAD_DIM)
    s_own = jnp.sum(q * k_new, axis=1, keepdims=True) * scale + b0_ref[...]
    m = jnp.maximum(jnp.max(sc, axis=1, keepdims=True), s_own)
    p = jnp.exp(sc - m)
    p_own = jnp.exp(s_own - m)
    inv = 1.0 / (jnp.sum(p, axis=1, keepdims=True) + p_own)
    p_ref[...] = (p * inv).reshape(n_seq, n_heads, n_keys)
    pown_ref[...] = jnp.broadcast_to(p_own * inv, (rows, LANES)).reshape(n_seq, n_heads, LANES)


def _moba_sample_select(scores, gate, bias_keys, q3, kn3, bias0):
    n, n_heads, n_keys = scores.shape
    hd = q3.shape[2]
    ns = math.gcd(n, SELECT_SEQS_PER_STEP)
    tok = pl.BlockSpec((ns, n_heads, hd), lambda b: (b, 0, 0))
    keys = pl.BlockSpec((ns, n_heads, n_keys), lambda b: (b, 0, 0))
    lane_blk = pl.BlockSpec((ns, n_heads, LANES), lambda b: (b, 0, 0))
    bias_keys = jnp.tile(bias_keys, (ns, 1))
    bias0 = jnp.tile(bias0, (ns, 1))
    return pl.pallas_call(
        _moba_sample_select_kernel,
        grid=(n // ns,),
        in_specs=[keys, lane_blk, pl.BlockSpec((ns * n_heads, n_keys), lambda b: (0, 0)), tok, tok,
                  pl.BlockSpec((ns * n_heads, 1), lambda b: (0, 0))],
        out_specs=[keys, lane_blk, lane_blk],
        out_shape=[jax.ShapeDtypeStruct((n, n_heads, n_keys), F32),
                   jax.ShapeDtypeStruct((n, n_heads, LANES), F32),
                   jax.ShapeDtypeStruct((n, n_heads, LANES), F32)],
        scratch_shapes=[pltpu.VMEM((LANES, n_keys), BF16)],
        compiler_params=_cparams(1),
    )(scores, gate, bias_keys, q3, kn3, bias0)


def _pages_to_fetch(page_table, sel, pages_per_blk):
    n, n_pages = page_table.shape
    pps = PAGES_PER_STEP
    steps = n * n_pages // pps
    n_blk = n_pages // pages_per_blk
    blk_needed = jnp.max(sel[:, :, :n_blk], axis=1) > 0.0
    needed = jnp.repeat(blk_needed, pages_per_blk, axis=1)
    step_id = jnp.arange(steps, dtype=jnp.int32)[:, None]
    last_needed = lax.cummax(jnp.where(needed.reshape(steps, pps), step_id, 0), axis=0)
    pages = jnp.take_along_axis(page_table.reshape(steps, pps), last_needed, axis=0)
    return pages.reshape(n, n_pages), blk_needed.astype(jnp.int32)


def _moba_sample_pv_kernel(pt_ref, need_ref, p_ref, pown_ref, vn_ref, *refs):
    pps = PAGES_PER_STEP
    v_refs, o_ref, acc_s = refs[:pps], refs[pps], refs[pps + 1]
    n_heads, hd, page = v_refs[0].shape[1:]
    pages_per_blk = MOBA_BLOCK // page
    blks_per_step = pps // pages_per_blk
    b = pl.program_id(0)
    j = pl.program_id(1)

    @pl.when(j == 0)
    def _():
        acc_s[...] = jnp.zeros(acc_s.shape, F32)

    for blk_i in range(blks_per_step):
        @pl.when(need_ref[b, j * blks_per_step + blk_i] > 0)
        def _(blk_i=blk_i):
            acc = acc_s[...]
            for half in range(pages_per_blk):
                i = blk_i * pages_per_blk + half
                w = p_ref[0, :, i * page:(i + 1) * page][:, None, :]
                acc = acc + jnp.where(w > 0.0, v_refs[i][0] * w, 0.0)
            acc_s[...] = acc

    @pl.when(j == pl.num_programs(1) - 1)
    def _():
        eye = (lax.broadcasted_iota(jnp.int32, (hd, hd), 0)
               == lax.broadcasted_iota(jnp.int32, (hd, hd), 1))
        for h in range(n_heads):
            col = jnp.sum(acc_s[h], axis=1, keepdims=True)
            row = jnp.sum(jnp.where(eye, col, 0.0), axis=0, keepdims=True)
            o_ref[0, h:h + 1, :] = row + pown_ref[0, h:h + 1, 0:hd] * vn_ref[0, h:h + 1, :]


def _moba_sample_pv(pages, blk_needed, probs, p_own, vn3, cache_t):
    n, n_pages = pages.shape
    n_heads, hd, page = cache_t.shape[1:]
    pps = PAGES_PER_STEP
    v_spec = lambda i: pl.BlockSpec((1, n_heads, hd, page),
                                    lambda b, j, pt, need, i=i: (pt[b, j * pps + i], 0, 0, 0))
    tok = pl.BlockSpec((1, n_heads, hd), lambda b, j, pt, need: (b, 0, 0))
    grid_spec = pltpu.PrefetchScalarGridSpec(
        num_scalar_prefetch=2,
        grid=(n, n_pages // pps),
        in_specs=[pl.BlockSpec((1, n_heads, pps * page), lambda b, j, pt, need: (b, 0, j)),
                  pl.BlockSpec((1, n_heads, LANES), lambda b, j, pt, need: (b, 0, 0)), tok]
                 + [v_spec(i) for i in range(pps)],
        out_specs=tok,
        scratch_shapes=[pltpu.VMEM((n_heads, hd, page), F32)])
    return pl.pallas_call(
        _moba_sample_pv_kernel,
        grid_spec=grid_spec,
        out_shape=jax.ShapeDtypeStruct((n, n_heads, hd), F32),
        compiler_params=_cparams(2),
    )(pages, blk_needed, probs, p_own, vn3, *([cache_t] * pps))


def _sample_key_bias(tab, past_len):
    n_heads, n_dist = tab.shape
    assert past_len >= n_dist
    tab_x = jnp.concatenate([tab, tab[:, -1:]], axis=1)
    near = jnp.flip(tab_x[:, 1:n_dist + 1], axis=1)
    far = jnp.broadcast_to(tab[:, -1:], (n_heads, past_len - n_dist))
    return jnp.concatenate([far, near], axis=1)


def _moba_sample(q, k_new, v_new, cache_k_l, cache_v_l, page_table, tab):
    n, width = q.shape
    n_pool, page, n_heads, hd = cache_k_l.shape
    n_pages = page_table.shape[1]
    past_len = n_pages * page
    assert hd == HEAD_DIM and n_heads * hd == width and page == LANES
    assert past_len % MOBA_BLOCK == 0 and past_len // MOBA_BLOCK >= MOBA_TOPK
    assert n_pages % PAGES_PER_STEP == 0
    heads = lambda a: a.reshape(n, n_heads, hd)
    q3, kn3, vn3 = heads(q), heads(k_new), heads(v_new)
    ck = jnp.transpose(cache_k_l, (0, 2, 3, 1))
    cv = jnp.transpose(cache_v_l, (0, 2, 3, 1))
    q_cols = jnp.broadcast_to(q3[..., None], (n, n_heads, hd, page))
    assert PAGES_PER_STEP % (MOBA_BLOCK // page) == 0
    scores, gate = _moba_sample_scores(page_table, q_cols, ck)
    probs, p_own, sel = _moba_sample_select(scores, gate, _sample_key_bias(tab, past_len), q3, kn3,
                                            tab[:, 0:1])
    v_pages, blk_needed = _pages_to_fetch(page_table, sel, MOBA_BLOCK // page)
    return _moba_sample_pv(v_pages, blk_needed, probs, p_own, vn3, cv).reshape(n, width)


def _layer_weights(l, precise, norm1_g, w_in, shift_mu, w0, w2, a0, a2, g2, k_k, k_a, r_k,
                   lnx_g, lnx_b, w_out, norm2_g, w_rg, b_rg, w_re, b_re, w_eg, w_eu, w_ed,
                   w_ple, w_pleg):
    wd = F32 if precise else BF16
    row = lambda x: x.reshape(1, -1).astype(F32)
    wa3 = 3 * (w_out.shape[1] - w0.shape[1])
    rw_cols = w_in.shape[2] - wa3
    z_pad = -rw_cols % LANES
    w2p, a2p, g2p = _lora_padded(w2[l], a2[l], g2[l])
    n_rt = N_GROUPS + N_GROUPS * EXPERTS_PER_GROUP
    w_router = jnp.pad(jnp.concatenate([w_rg[l], w_re[l]], axis=1), ((0, 0), (0, LANES - n_rt)))
    b_router = jnp.pad(jnp.concatenate([b_rg[l], b_re[l]]), (0, LANES - n_rt))
    return dict(
        norm1=row(norm1_g[l]),
        wqkv=w_in[l][:, :wa3].astype(wd),
        wz=jnp.pad(w_in[l][:, wa3:], ((0, 0), (0, z_pad))).astype(wd),
        mu=jnp.pad(shift_mu[l], (0, z_pad))[None],
        w0=row(w0[l]), w2p=w2p, a0=row(a0[l]), a2p=a2p, g2p=g2p,
        kk=row(k_k[l]), ka=row(k_a[l]), rk=row(r_k[l]), lg=row(lnx_g[l]), lb=row(lnx_b[l]),
        wo=w_out[l].astype(wd), norm2=row(norm2_g[l]), w_router=w_router, b_router=b_router[None],
        weg=w_eg[l].astype(wd), weu=w_eu[l].astype(wd),
        wed=w_ed[l].reshape(N_GROUPS, -1, w_ed.shape[3]).astype(wd),
        wple=w_ple[l].astype(wd), wgate=w_pleg[l].astype(wd),
        rw_cols=rw_cols, wa=wa3 // 3)


def _post_attention(x2d, att, rw, p2d, w, fg, *, precise, final):
    tm = min(x2d.shape[0], MOE_TOKEN_TILE)
    x1, h2, comb = _outproj_router(x2d, att, rw, w["wo"], w["norm2"], w["w_router"], w["b_router"],
                                   tm=tm, precise=precise)
    x2 = _moe(h2, comb, x1, w["weg"], w["weu"], w["wed"], tm=tm, precise=precise)
    return _ple(x2, p2d, w["wple"], w["wgate"], fg, tm=tm, precise=precise, final=final)


def kernel(x_prompt, x_sample, p_prompt, p_sample, cache_k, cache_v, page_table, state_wkv, state_shift, norm1_g, w_in, rel_bias, shift_mu, w0, w2, a0, a2, g2, k_k, k_a, r_k, lnx_g, lnx_b, w_out, norm2_g, w_rg, b_rg, w_re, b_re, w_eg, w_eu, w_ed, w_ple, w_pleg, final_g):
    depth = norm1_g.shape[0]
    b, s, d = x_prompt.shape
    n = x_sample.shape[0]
    assert x_sample.shape[1] == 1
    n_heads = state_wkv.shape[2]
    layer_params = (norm1_g, w_in, shift_mu, w0, w2, a0, a2, g2, k_k, k_a, r_k, lnx_g, lnx_b,
                    w_out, norm2_g, w_rg, b_rg, w_re, b_re, w_eg, w_eu, w_ed, w_ple, w_pleg)
    fg = final_g.reshape(1, -1)
    tab = _rel_bias_table(rel_bias, 2 * MOBA_BLOCK)
    bias_tiles = _bias_tiles(tab * LOG2_E)
    xp = x_prompt.reshape(b * s, d)
    xs = x_sample.reshape(n, d)
    outs = [[] for _ in range(8)]
    for l in range(depth):
        final = l == depth - 1
        w = _layer_weights(l, False, *layer_params)
        wa, rw_cols = w["wa"], w["rw_cols"]
        heads = lambda a, lead: a.reshape(lead + (wa // HEAD_DIM, HEAD_DIM))
        qk, k_t, v_t, z = _inproj_seq(xp, w["norm1"], w["wqkv"], w["wz"], b=b, tm=512)
        att = _moba_prompt(qk.reshape(b, s, 2 * wa), v_t, bias_tiles).reshape(b * s, wa)
        seq_major = lambda a: jnp.transpose(a.reshape(b, wa // HEAD_DIM, HEAD_DIM, s), (0, 3, 1, 2))
        z3 = z.reshape(b, s, -1)
        rw, st = _rwkv_prompt(z3, w["mu"], w["w0"], w["w2p"], w["a0"], w["a2p"], w["g2p"], w["kk"],
                              w["ka"], w["rk"], w["lg"], w["lb"], tc=RWKV_TOKEN_TILE)
        xp = _post_attention(xp, att, rw.reshape(b * s, -1), p_prompt[l].reshape(b * s, -1), w, fg,
                             precise=False, final=final)
        outs[0].append(seq_major(k_t))
        outs[1].append(seq_major(v_t))
        outs[4].append(_state_from_tiles(st))
        outs[6].append(z3[:, -1, :rw_cols])
        qkv, z = _inproj(xs, w["norm1"], w["wqkv"], w["wz"], tm=n, precise=False)
        q_s, k_s, v_s = qkv[:, :wa], qkv[:, wa:2 * wa], qkv[:, 2 * wa:]
        att = _moba_sample(q_s, k_s, v_s, cache_k[l], cache_v[l], page_table, tab)
        z_prev = jnp.pad(state_shift[l], ((0, 0), (0, z.shape[1] - rw_cols)))
        per_tok = _rwkv_sample_prep(z, z_prev, w["mu"], w["w0"], w["w2p"], w["a0"], w["a2p"], w["g2p"],
                                    w["kk"], w["ka"])
        per_head = [a.reshape(n, n_heads, HEAD_DIM) for a in per_tok]
        hp = lambda a: a.reshape(n_heads, HEAD_DIM)
        st_s, rw_s = _rwkv_sample_step(state_wkv[l], per_head, hp(w["rk"]), hp(w["lg"]), hp(w["lb"]))
        xs = _post_attention(xs, att, rw_s.reshape(n, -1), p_sample[l].reshape(n, -1), w, fg,
                             precise=False, final=final)
        outs[2].append(heads(k_s, (n, 1)))
        outs[3].append(heads(v_s, (n, 1)))
        outs[5].append(st_s)
        outs[7].append(z[:, :rw_cols])
    kp, vp, ks, vs, wp, ws, sp, ss = [jnp.stack(o) for o in outs]
    return (xp.reshape(b, s, d), xs.reshape(n, 1, d), kp, vp, ks, vs, wp, ws, sp, ss)
```

```python
import functools
import math

import jax
import jax.numpy as jnp
from jax import lax
from jax.experimental import pallas as pl
from jax.experimental.pallas import tpu as pltpu

F32 = jnp.float32
BF16 = jnp.bfloat16
HIGHEST = lax.Precision.HIGHEST

HEAD_DIM = 64
MOBA_BLOCK = 256
MOBA_TOPK = 3
N_BUCKETS = 32
MAX_DISTANCE = 128
LORA_W = 64
LORA_A = 64
LORA_G = 160
N_GROUPS = 4
EXPERTS_PER_GROUP = 4
RMS_EPS = 1e-6
GN_EPS = 64e-5
NEG = -1e30
LOG2_E = math.log2(math.e)

LANES = 128
MXU_DIM = 256
VMEM_LIMIT_BYTES = 56 * 1024 * 1024

MOE_TOKEN_TILE = 1024
MOBA_HEAD_GROUP = 8
RWKV_CHUNK = 64
RWKV_TOKEN_TILE = 512
HEADS_PER_TILE = MXU_DIM // HEAD_DIM


def _cparams(n_axes):
    return pltpu.CompilerParams(dimension_semantics=("arbitrary",) * n_axes,
                                vmem_limit_bytes=VMEM_LIMIT_BYTES)


def _mm(a, b, precise):
    if precise:
        return jnp.dot(a.astype(F32), b.astype(F32), precision=HIGHEST, preferred_element_type=F32)
    return jnp.dot(a.astype(BF16), b.astype(BF16), preferred_element_type=F32)


def _mm_nt(a, b):
    return lax.dot_general(a.astype(BF16), b.astype(BF16), (((1,), (1,)), ((), ())),
                           preferred_element_type=F32)


def _mm_tn(a, b):
    return lax.dot_general(a.astype(BF16), b.astype(BF16), (((0,), (0,)), ((), ())),
                           preferred_element_type=F32)


def _mm_split(a_exact, x, terms):
    a16 = a_exact.astype(BF16)
    acc = None
    rem = x
    for _ in range(terms):
        piece = rem.astype(BF16)
        part = jnp.dot(a16, piece, preferred_element_type=F32)
        acc = part if acc is None else acc + part
        rem = rem - piece.astype(F32)
    return acc


def _mm_split_rhs(x, b_exact, terms):
    b16 = b_exact.astype(BF16)
    acc = None
    rem = x
    for _ in range(terms):
        piece = rem.astype(BF16)
        part = jnp.dot(piece, b16, preferred_element_type=F32)
        acc = part if acc is None else acc + part
        rem = rem - piece.astype(F32)
    return acc


def _sigmoid(x):
    return 1.0 / (1.0 + jnp.exp(-x))


def _softplus(x):
    return jnp.maximum(x, 0.0) + jnp.log(1.0 + jnp.exp(-jnp.abs(x)))


def _rms_norm(x, g):
    return x * lax.rsqrt(jnp.mean(x * x, axis=-1, keepdims=True) + RMS_EPS) * g


def _inproj_kernel(x_ref, g_ref, wqkv_ref, wz_ref, qkv_ref, z_ref, *, precise):
    h = _rms_norm(x_ref[...], g_ref[...])
    qkv_ref[...] = _mm(h, wqkv_ref[...], precise)
    z_ref[...] = _mm(h, wz_ref[...], precise)


def _inproj(x2d, g, wqkv, wz, *, tm, precise):
    t, d = x2d.shape
    nq, nz = wqkv.shape[1], wz.shape[1]
    return pl.pallas_call(
        functools.partial(_inproj_kernel, precise=precise),
        grid=(t // tm,),
        in_specs=[pl.BlockSpec((tm, d), lambda i: (i, 0)),
                  pl.BlockSpec((1, d), lambda i: (0, 0)),
                  pl.BlockSpec((d, nq), lambda i: (0, 0)),
                  pl.BlockSpec((d, nz), lambda i: (0, 0))],
        out_specs=[pl.BlockSpec((tm, nq), lambda i: (i, 0)),
                   pl.BlockSpec((tm, nz), lambda i: (i, 0))],
        out_shape=[jax.ShapeDtypeStruct((t, nq), F32), jax.ShapeDtypeStruct((t, nz), F32)],
        compiler_params=_cparams(1),
    )(x2d, g, wqkv, wz)


def _inproj_seq_kernel(x_ref, g_ref, wqkv_ref, wz_ref, qk_ref, kt_ref, vt_ref, z_ref):
    wa = kt_ref.shape[1]
    h = _rms_norm(x_ref[...], g_ref[...])
    qkv = _mm(h, wqkv_ref[...], False)
    qk_ref[...] = qkv[:, :2 * wa]
    kt_ref[0] = qkv[:, wa:2 * wa].T
    vt_ref[0] = qkv[:, 2 * wa:].T
    z_ref[...] = _mm(h, wz_ref[...], False)


def _inproj_seq(x2d, g, wqkv, wz, *, b, tm):
    t, d = x2d.shape
    s = t // b
    nq, nz = wqkv.shape[1], wz.shape[1]
    wa = nq // 3
    per_seq = s // tm
    feat_major = pl.BlockSpec((1, wa, tm), lambda i: (i // per_seq, 0, i % per_seq))
    return pl.pallas_call(
        _inproj_seq_kernel,
        grid=(t // tm,),
        in_specs=[pl.BlockSpec((tm, d), lambda i: (i, 0)),
                  pl.BlockSpec((1, d), lambda i: (0, 0)),
                  pl.BlockSpec((d, nq), lambda i: (0, 0)),
                  pl.BlockSpec((d, nz), lambda i: (0, 0))],
        out_specs=[pl.BlockSpec((tm, 2 * wa), lambda i: (i, 0)), feat_major, feat_major,
                   pl.BlockSpec((tm, nz), lambda i: (i, 0))],
        out_shape=[jax.ShapeDtypeStruct((t, 2 * wa), F32), jax.ShapeDtypeStruct((b, wa, s), F32),
                   jax.ShapeDtypeStruct((b, wa, s), F32), jax.ShapeDtypeStruct((t, nz), F32)],
        compiler_params=_cparams(1),
    )(x2d, g, wqkv, wz)


def _moba_prompt_kernel(q_ref, k_ref, vt_ref, tt_ref, o_ref,
                        kbf_s, vt_s, kmean_s, m_s, l_s, acc_s, sel_s, qbf_s):
    blk = MOBA_BLOCK
    i = pl.program_id(1)
    nb = kbf_s.shape[0]
    n_heads = tt_ref.shape[0]
    scale = HEAD_DIM ** -0.5 * LOG2_E

    @pl.when(i == 0)
    def _():
        for n in range(nb):
            kb = k_ref[0, n * blk:(n + 1) * blk, :]
            kbf_s[n] = kb.astype(BF16)
            kmean_s[n:n + 1, :] = jnp.mean(kb, axis=0, keepdims=True)
            vt_s[n] = vt_ref[0, :, n * blk:(n + 1) * blk].astype(BF16)

    q_t = q_ref[0].T
    blk_row = lax.broadcasted_iota(jnp.int32, (nb, blk), 0)
    key_idx = lax.broadcasted_iota(jnp.int32, (blk, blk), 0)
    qry_idx = lax.broadcasted_iota(jnp.int32, (blk, blk), 1)
    causal = key_idx <= qry_idx
    upper_half = lax.broadcasted_iota(jnp.int32, (LANES, blk), 0) >= HEAD_DIM

    for h in range(n_heads):
        lanes = slice((h // 2) * LANES, (h // 2 + 1) * LANES)
        q_h = jnp.where(upper_half == (h % 2 == 1), q_t[lanes, :], 0.0)
        gate = _mm(kmean_s[:, lanes], q_h, False)
        gate = jnp.where(blk_row < i, gate, NEG)
        rank = jnp.zeros((nb, blk), jnp.int32)
        for m in range(nb):
            gm = gate[m:m + 1, :]
            beats = (gm > gate) | ((gm == gate) & (m < blk_row))
            rank = rank + beats.astype(jnp.int32)
        sel_s[h] = jnp.where((rank < MOBA_TOPK) & (blk_row < i), 1.0, 0.0)
        qbf_s[h] = (q_h * scale).astype(BF16)
    m_s[...] = jnp.full(m_s.shape, NEG, F32)
    l_s[...] = jnp.zeros(l_s.shape, F32)
    acc_s[...] = jnp.zeros(acc_s.shape, F32)

    def attend(group, n, bias_of, mask_of, uniform_bias=False):
        scores = [jnp.dot(kbf_s[n, :, (h // 2) * LANES:(h // 2 + 1) * LANES], qbf_s[h],
                          preferred_element_type=F32) for h in group]
        probs, alphas = [], []
        for h, s in zip(group, scores):
            m_old = m_s[h:h + 1, :]
            if uniform_bias:
                bias = bias_of(h)
                s = jnp.where(mask_of(h), s, NEG)
                m_new = jnp.maximum(m_old, jnp.max(s, axis=0, keepdims=True) + bias)
                p = jnp.exp2(s - (m_new - bias))
            else:
                s = jnp.where(mask_of(h), s + bias_of(h), NEG)
                m_new = jnp.maximum(m_old, jnp.max(s, axis=0, keepdims=True))
                p = jnp.exp2(s - m_new)
            alpha = jnp.exp2(m_old - m_new)
            l_s[h:h + 1, :] = alpha * l_s[h:h + 1, :] + jnp.sum(p, axis=0, keepdims=True)
            m_s[h:h + 1, :] = m_new
            probs.append(p.astype(BF16))
            alphas.append(alpha)
        pvs = [jnp.dot(vt_s[n, h * HEAD_DIM:(h + 1) * HEAD_DIM, :], p, preferred_element_type=F32)
               for h, p in zip(group, probs)]
        for h, alpha, pv in zip(group, alphas, pvs):
            acc_s[h] = alpha * acc_s[h] + pv

    for g0 in range(0, n_heads, MOBA_HEAD_GROUP):
        group = range(g0, g0 + MOBA_HEAD_GROUP)
        attend(group, i, lambda h: tt_ref[h, :, blk:2 * blk], lambda h: causal)

        @pl.when(i >= 1)
        def _(group=group):
            n = i - 1
            attend(group, n, lambda h: tt_ref[h, :, 0:blk], lambda h: sel_s[h, pl.ds(n, 1), :] > 0.5)

        def far_body(n, carry, group=group):
            attend(group, n, lambda h: tt_ref[h, 0:1, blk - 1:blk],
                   lambda h: sel_s[h, pl.ds(n, 1), :] > 0.5, uniform_bias=True)
            return carry

        lax.fori_loop(0, jnp.maximum(i - 1, 0), far_body, 0)

    for pair in range(n_heads // 2):
        outs = [acc_s[h] / l_s[h:h + 1, :] for h in (2 * pair, 2 * pair + 1)]
        o_ref[0, :, pair * LANES:(pair + 1) * LANES] = jnp.concatenate(outs, axis=0).T


def _moba_prompt(qk3, vt3, tt):
    b, s, w2 = qk3.shape
    w = w2 // 2
    blk = MOBA_BLOCK
    nb = s // blk
    n_heads = w // HEAD_DIM
    return pl.pallas_call(
        _moba_prompt_kernel,
        grid=(b, nb),
        in_specs=[pl.BlockSpec((1, blk, w), lambda bi, i: (bi, i, 0)),
                  pl.BlockSpec((1, s, w), lambda bi, i: (bi, 0, 1)),
                  pl.BlockSpec((1, w, s), lambda bi, i: (bi, 0, 0)),
                  pl.BlockSpec((n_heads, blk, 2 * blk), lambda bi, i: (0, 0, 0))],
        out_specs=pl.BlockSpec((1, blk, w), lambda bi, i: (bi, i, 0)),
        out_shape=jax.ShapeDtypeStruct((b, s, w), F32),
        scratch_shapes=[pltpu.VMEM((nb, blk, w), BF16),
                        pltpu.VMEM((nb, w, blk), BF16),
                        pltpu.VMEM((nb, w), F32),
                        pltpu.VMEM((n_heads, blk), F32),
                        pltpu.VMEM((n_heads, blk), F32),
                        pltpu.VMEM((n_heads, HEAD_DIM, blk), F32),
                        pltpu.VMEM((n_heads, nb, blk), F32),
                        pltpu.VMEM((n_heads, LANES, blk), BF16)],
        compiler_params=_cparams(2),
    )(qk3, qk3, vt3, tt)


def _rel_bias_table(rel_bias, n_dist):
    n = jnp.arange(n_dist, dtype=jnp.int32)
    max_exact = N_BUCKETS // 2
    nf = jnp.maximum(n, 1).astype(F32)
    large = max_exact + (jnp.log(nf / max_exact) / math.log(MAX_DISTANCE / max_exact)
                         * (N_BUCKETS - max_exact)).astype(jnp.int32)
    large = jnp.minimum(large, N_BUCKETS - 1)
    bucket = jnp.where(n < max_exact, n, large)
    return rel_bias.astype(F32).T[:, bucket]


def _bias_tiles(tab):
    blk = MOBA_BLOCK
    period = 2 * blk
    n_heads = tab.shape[0]
    base = jnp.concatenate([tab[:, blk:period], tab[:, 0:blk]], axis=1)
    stride = 2 * period - 1
    reps = -(-(blk * stride) // period)
    seq = jnp.tile(base, (1, reps))[:, :blk * stride]
    return seq.reshape(n_heads, blk, stride)[:, :, :period]


def _mm3(a, b):
    a_hi = a.astype(BF16)
    b_hi = b.astype(BF16)
    a_lo = (a - a_hi.astype(F32)).astype(BF16)
    b_lo = (b - b_hi.astype(F32)).astype(BF16)
    return (jnp.dot(a_hi, b_hi, preferred_element_type=F32)
            + jnp.dot(a_hi, b_lo, preferred_element_type=F32)
            + jnp.dot(a_lo, b_hi, preferred_element_type=F32))


def _rwkv_prompt_kernel(zr_ref, zk_ref, zv_ref, zl_ref,
                        mur_ref, muk_ref, muv_ref, mul_ref,
                        w0_ref, w2_ref, a0_ref, a2_ref, g2_ref,
                        kk_ref, ka_ref, rk_ref, lg_ref, lb_ref,
                        rw_ref, st_ref,
                        p_s, last_r, last_k, last_v, last_l, tick=lambda: None):
    t = pl.program_id(2)
    tc = zr_ref.shape[1]
    w = MXU_DIM
    c_len = RWKV_CHUNK
    hd_shift = HEAD_DIM.bit_length() - 1

    @pl.when(t == 0)
    def _():
        p_s[...] = jnp.zeros(p_s.shape, F32)
        last_r[...] = jnp.zeros(last_r.shape, F32)
        last_k[...] = jnp.zeros(last_k.shape, F32)
        last_v[...] = jnp.zeros(last_v.shape, F32)
        last_l[...] = jnp.zeros(last_l.shape, F32)

    def token_shift_mix(z_ref, last_ref, mu_ref):
        z = z_ref[0]
        row = lax.broadcasted_iota(jnp.int32, z.shape, 0)
        prev = jnp.where(row == 0, last_ref[...], pltpu.roll(z, 1, 0))
        last_ref[...] = z[tc - 1:tc, :]
        return z + (prev - z) * mu_ref[...]

    ri = lax.broadcasted_iota(jnp.int32, (w, w), 0)
    ci = lax.broadcasted_iota(jnp.int32, (w, w), 1)
    same_head = (ri >> hd_shift) == (ci >> hd_shift)
    tok_r = ri & (HEAD_DIM - 1)
    tok_c = ci & (HEAD_DIM - 1)
    strict = same_head & (tok_r > tok_c)
    incl = same_head & (tok_r >= tok_c)
    eye = ri == ci
    head_ones = jnp.where(same_head, 1.0, 0.0).astype(BF16)
    tri = jnp.where(lax.broadcasted_iota(jnp.int32, (c_len, c_len), 0)
                    >= lax.broadcasted_iota(jnp.int32, (c_len, c_len), 1), 1.0, 0.0).astype(BF16)

    def head_sum(x):
        return _mm_split_rhs(x, head_ones, 2)

    def stack(x):
        return jnp.where(same_head, jnp.concatenate([x] * HEADS_PER_TILE, axis=0), 0.0)

    def unstack(x):
        out = x[0:c_len]
        for e in range(1, HEADS_PER_TILE):
            out = out + x[e * c_len:(e + 1) * c_len]
        return out

    r = token_shift_mix(zr_ref, last_r, mur_ref)
    k = token_shift_mix(zk_ref, last_k, muk_ref)
    v = token_shift_mix(zv_ref, last_v, muv_ref)
    zl = token_shift_mix(zl_ref, last_l, mul_ref)

    n_wa = LORA_W + LORA_A
    zl_wa = zl[:, :n_wa]
    w_pre = w0_ref[...] + _mm(jnp.tanh(zl_wa), w2_ref[0:n_wa, :], False)
    logd = -jnp.exp(-_softplus(-w_pre) - 0.5)
    eta = _sigmoid(a0_ref[...] + _mm(zl_wa, a2_ref[0:n_wa, :], False))
    gate = _mm(_sigmoid(zl[:, n_wa:]), g2_ref[n_wa:, :], False)
    kk = k * kk_ref[...]
    kk = kk * lax.rsqrt(jnp.maximum(head_sum(kk * kk), 1e-24))
    kh = k * (1.0 + (eta - 1.0) * ka_ref[...])
    bonus = head_sum(r * kh * rk_ref[...]) * v
    kb = kk * eta
    tick()

    chunks = range(tc // c_len)
    each = lambda f, *lists: [f(*(x[c] for x in lists)) for c in chunks]
    rows = [slice(c * c_len, (c + 1) * c_len) for c in chunks]
    ld = [logd[rw] for rw in rows]
    cum = each(lambda x: _mm_split(tri, x, 3), ld)
    cum_end = each(lambda x: x[c_len - 1:c_len, :], cum)
    e_inv = each(lambda x: jnp.exp(-x), cum)
    e_rem = each(lambda x, xe: jnp.exp(xe - x), cum, cum_end)
    tick()
    a_t =[stack(-kk[rows[c]] * jnp.exp(cum[c] - ld[c])) for c in chunks]
    b_t = [stack(kb[rows[c]] * e_inv[c]) for c in chunks]
    k_t = [stack(kh[rows[c]] * e_inv[c]) for c in chunks]
    r_t = [stack(r[rows[c]] * jnp.exp(cum[c])) for c in chunks]
    b_h = [stack(kb[rows[c]] * e_rem[c]) for c in chunks]
    k_h = [stack(kh[rows[c]] * e_rem[c]) for c in chunks]
    v_s = [stack(v[rows[c]]) for c in chunks]
    tick()

    prod =each(lambda a, rr, bb, kt: _mm_nt(jnp.concatenate([a, rr], axis=0),
                                             jnp.concatenate([bb, kt], axis=0)), a_t, r_t, b_t, k_t)
    n_mat = each(lambda x: jnp.where(strict, x[0:w, 0:w], 0.0), prod)
    a_k = each(lambda x: jnp.where(strict, x[0:w, w:2 * w], 0.0), prod)
    m_rb = each(lambda x: jnp.where(incl, x[w:2 * w, 0:w], 0.0), prod)
    m_rk = each(lambda x: jnp.where(incl, x[w:2 * w, w:2 * w], 0.0), prod)
    tick()

    t_inv = each(lambda x: jnp.where(eye, 1.0, 0.0) + x, n_mat)
    n_pow = n_mat
    for _ in range(c_len.bit_length() - 2):
        n_pow = each(lambda x: _mm(x, x, False), n_pow)
        tick()
        t_inv = each(lambda x, y: x + _mm(x, y, False), t_inv, n_pow)
        tick()

    a_p = each(lambda x, y: _mm(x, y, False), t_inv, a_t)
    tick()
    a_v = each(lambda x, y: _mm(x, y, False), a_k, v_s)
    tick()
    v_p = each(lambda x, y: _mm(x, y, False), t_inv, a_v)
    tick()
    r_p = each(lambda rr, m, ap: unstack(rr + _mm(m, ap, False)), r_t, m_rb, a_p)
    tick()
    y_0 = each(lambda mb, vp, mk, vs: unstack(_mm(mb, vp, False) + _mm(mk, vs, False)),
               m_rb, v_p, m_rk, v_s)
    tick()
    decay = each(jnp.exp, cum_end)
    mix = each(_mm_tn, a_p, b_h)
    tick()
    add = each(lambda vp, bh, vs, kh_: _mm_tn(vp, bh) + _mm_tn(vs, kh_), v_p, b_h, v_s, k_h)
    tick()

    st = p_s[...]
    ys = []
    for c in chunks:
        ys.append(_mm_nt(r_p[c], st) + y_0[c])
        st = st * decay[c] + _mm(st, mix[c], False) + add[c]
    p_s[...] = st
    st_ref[0, 0] = st

    y = jnp.concatenate(ys, axis=0)
    mean = head_sum(y) * (1.0 / HEAD_DIM)
    yc = y - mean
    var = head_sum(yc * yc) * (1.0 / HEAD_DIM)
    yn = yc * lax.rsqrt(var + GN_EPS) * lg_ref[...] + lb_ref[...]
    rw_ref[0] = (yn + bonus) * gate


N_RWKV_INPUTS = 18


def _rwkv_scores_kernel(pt_ref, *refs, steps_per_seq):
    pps = PAGES_PER_STEP
    n_in = N_RWKV_INPUTS
    rwkv_in, qc_ref, k_refs = refs[:n_in], refs[n_in], refs[n_in + 1:n_in + 1 + pps]
    rw_ref, st_ref, s_ref, gate_ref = refs[n_in + 1 + pps:n_in + 5 + pps]
    scratch = refs[n_in + 5 + pps:]
    step = ((pl.program_id(0) * pl.num_programs(1) + pl.program_id(1)) * pl.num_programs(2)
            + pl.program_id(2))
    work = iter(_key_pass_items(qc_ref, k_refs, s_ref, gate_ref, step % steps_per_seq))

    def tick():
        item = next(work, None)
        if item is not None:
            item()

    _rwkv_prompt_kernel(*rwkv_in, rw_ref, st_ref, *scratch, tick=tick)
    for item in work:
        item()


def _rwkv_prompt(z3, mu, w0, w2p, a0, a2p, g2p, kk, ka, rk, lg, lb, *, tc, key_pass=None):
    b, s, _ = z3.shape
    w = MXU_DIM
    wr = w0.shape[1]
    nt = wr // w
    n_t = s // tc
    lw = w2p.shape[0]
    lora_blk = (3 * wr) // lw
    vec = lambda off: pl.BlockSpec((1, w), lambda bi, q, t, *_, off=off: (0, off + q))
    zblk = lambda off: pl.BlockSpec((1, tc, w), lambda bi, q, t, *_, off=off: (bi, t, off + q))
    lora_w = pl.BlockSpec((lw, w), lambda bi, q, t, *_: (0, q))
    in_specs = [zblk(0), zblk(nt), zblk(2 * nt),
                pl.BlockSpec((1, tc, lw), lambda bi, q, t, *_: (bi, t, lora_blk)),
                vec(0), vec(nt), vec(2 * nt),
                pl.BlockSpec((1, lw), lambda bi, q, t, *_: (0, lora_blk)),
                vec(0), lora_w, vec(0), lora_w, lora_w,
                vec(0), vec(0), vec(0), vec(0), vec(0)]
    out_specs = [pl.BlockSpec((1, tc, w), lambda bi, q, t, *_: (bi, t, q)),
                 pl.BlockSpec((1, 1, w, w), lambda bi, q, t, *_: (bi, q, 0, 0))]
    out_shape = [jax.ShapeDtypeStruct((b, s, wr), F32), jax.ShapeDtypeStruct((b, nt, w, w), F32)]
    scratch_shapes = [pltpu.VMEM((w, w), F32), pltpu.VMEM((1, w), F32), pltpu.VMEM((1, w), F32),
                      pltpu.VMEM((1, w), F32), pltpu.VMEM((1, lw), F32)]
    args = (z3, z3, z3, z3, mu, mu, mu, mu, w0, w2p, a0, a2p, g2p, kk, ka, rk, lg, lb)
    assert len(args) == N_RWKV_INPUTS
    if key_pass is None:
        return pl.pallas_call(
            _rwkv_prompt_kernel, grid=(b, nt, n_t), in_specs=in_specs, out_specs=out_specs,
            out_shape=out_shape, scratch_shapes=scratch_shapes, compiler_params=_cparams(3),
        )(*args)

    page_table, q_cols, cache_t = key_pass
    n, n_pages = page_table.shape
    n_heads, hd, page = cache_t.shape[1:]
    pps = PAGES_PER_STEP
    steps_per_seq = n_pages // pps
    assert b * nt * n_t == n * steps_per_seq
    step = lambda bi, q, t: (bi * nt + q) * n_t + t
    seq = lambda bi, q, t: step(bi, q, t) // steps_per_seq
    part = lambda bi, q, t: step(bi, q, t) % steps_per_seq
    k_spec = lambda i: pl.BlockSpec(
        (1, n_heads, hd, page),
        lambda bi, q, t, pt, i=i: (pt[seq(bi, q, t), part(bi, q, t) * pps + i], 0, 0, 0))
    grid_spec = pltpu.PrefetchScalarGridSpec(
        num_scalar_prefetch=1,
        grid=(b, nt, n_t),
        in_specs=in_specs
                 + [pl.BlockSpec((1, n_heads, hd, page), lambda bi, q, t, pt: (seq(bi, q, t), 0, 0, 0))]
                 + [k_spec(i) for i in range(pps)],
        out_specs=out_specs
                  + [pl.BlockSpec((1, n_heads, pps * page),
                                  lambda bi, q, t, pt: (seq(bi, q, t), 0, part(bi, q, t))),
                     pl.BlockSpec((1, n_heads, LANES), lambda bi, q, t, pt: (seq(bi, q, t), 0, 0))],
        scratch_shapes=scratch_shapes)
    return pl.pallas_call(
        functools.partial(_rwkv_scores_kernel, steps_per_seq=steps_per_seq),
        grid_spec=grid_spec,
        out_shape=out_shape + [jax.ShapeDtypeStruct((n, n_heads, n_pages * page), F32),
                               jax.ShapeDtypeStruct((n, n_heads, LANES), F32)],
        compiler_params=_cparams(3),
    )(page_table, *args, q_cols, *([cache_t] * pps))


def _state_from_tiles(st):
    b, nt = st.shape[:2]
    heads = []
    for q in range(nt):
        for e in range(HEADS_PER_TILE):
            sl = slice(e * HEAD_DIM, (e + 1) * HEAD_DIM)
            heads.append(st[:, q, sl, sl])
    return jnp.stack(heads, axis=1)


def _lora_padded(w2, a2, g2):
    lw = 3 * LANES
    n = w2.shape[1]
    o_a = LORA_W
    o_g = LORA_W + LORA_A
    w2p = jnp.zeros((lw, n), F32).at[0:o_a].set(w2)
    a2p = jnp.zeros((lw, n), F32).at[o_a:o_g].set(a2)
    g2p = jnp.zeros((lw, n), F32).at[o_g:o_g + LORA_G].set(g2)
    return w2p, a2p, g2p


ROUTER_LANE0 = N_GROUPS


def _first_argmax(x, lane):
    mx = jnp.max(x, axis=1, keepdims=True)
    idx = jnp.min(jnp.where(x == mx, lane, LANES), axis=1, keepdims=True)
    return mx, idx


def _outproj_router_kernel(x_ref, att_ref, rw_ref, wo_ref, g_ref, wr_ref, br_ref,
                           x1_ref, h2_ref, comb_ref, *, precise):
    wa = att_ref.shape[1]
    x1 = (x_ref[...] + _mm(att_ref[...], wo_ref[0:wa, :], precise)
          + _mm(rw_ref[...], wo_ref[wa:, :], precise))
    x1_ref[...] = x1
    h2 = _rms_norm(x1, g_ref[...])
    h2_ref[...] = h2.astype(h2_ref.dtype)
    logits = _mm(h2, wr_ref[...], precise) + br_ref[...]
    n_exp = N_GROUPS * EXPERTS_PER_GROUP
    lane = lax.broadcasted_iota(jnp.int32, logits.shape, 1)
    is_grp = lane < N_GROUPS
    gl = jnp.where(is_grp, logits, NEG)
    g_max, grp = _first_argmax(gl, lane)
    p_grp = 1.0 / jnp.sum(jnp.where(is_grp, jnp.exp(gl - g_max), 0.0), axis=1, keepdims=True)
    e_lane = lane - ROUTER_LANE0
    grp_shift = EXPERTS_PER_GROUP.bit_length() - 1
    in_grp = (e_lane >= 0) & (e_lane < n_exp) & ((e_lane >> grp_shift) == grp)
    el = jnp.where(in_grp, logits, NEG)
    m1, i1 = _first_argmax(el, lane)
    el2 = jnp.where(lane == i1, NEG, el)
    m2, i2 = _first_argmax(el2, lane)
    ratio = jnp.exp(m2 - m1)
    p1 = 1.0 / (1.0 + ratio)
    p2 = ratio / (1.0 + ratio)
    comb_ref[...] = p_grp * (jnp.where(lane == i1, p1, 0.0) + jnp.where(lane == i2, p2, 0.0))


def _outproj_router(x2d, att, rw, wo, g, wr, br, *, tm, precise):
    t, d = x2d.shape
    wa, wr_cols = att.shape[1], rw.shape[1]
    h_dtype = F32 if precise else BF16
    tok = lambda n: pl.BlockSpec((tm, n), lambda i: (i, 0))
    full = lambda a: pl.BlockSpec(a.shape, lambda i: (0,) * a.ndim)
    return pl.pallas_call(
        functools.partial(_outproj_router_kernel, precise=precise),
        grid=(t // tm,),
        in_specs=[tok(d), tok(wa), tok(wr_cols), full(wo), full(g), full(wr), full(br)],
        out_specs=[tok(d), tok(d), tok(LANES)],
        out_shape=[jax.ShapeDtypeStruct((t, d), F32), jax.ShapeDtypeStruct((t, d), h_dtype),
                   jax.ShapeDtypeStruct((t, LANES), F32)],
        compiler_params=_cparams(1),
    )(x2d, att, rw, wo, g, wr, br)


def _moe_kernel(h2_ref, comb_ref, x1_ref, wg_ref, wu_ref, wd_ref, x2_ref, *, precise):
    g = pl.program_id(1)

    @pl.when(g == 0)
    def _():
        x2_ref[...] = x1_ref[...]

    h2 = h2_ref[...]
    comb = comb_ref[...]
    lane = lax.broadcasted_iota(jnp.int32, comb.shape, 1)
    gates = [_mm(h2, wg_ref[j], precise) for j in range(EXPERTS_PER_GROUP)]
    ups = [_mm(h2, wu_ref[j], precise) for j in range(EXPERTS_PER_GROUP)]
    acts = []
    for j in range(EXPERTS_PER_GROUP):
        e_lane = ROUTER_LANE0 + g * EXPERTS_PER_GROUP + j
        weight = jnp.sum(jnp.where(lane == e_lane, comb, 0.0), axis=1, keepdims=True)
        acts.append(gates[j] * _sigmoid(gates[j]) * ups[j] * weight)
    x2_ref[...] += _mm(jnp.concatenate(acts, axis=1), wd_ref[0], precise)


def _moe(h2, comb, x1, wg, wu, wd, *, tm, precise):
    t, d = x1.shape
    n_grp, gf, _ = wd.shape
    epg = EXPERTS_PER_GROUP
    f = gf // epg
    return pl.pallas_call(
        functools.partial(_moe_kernel, precise=precise),
        grid=(t // tm, n_grp),
        in_specs=[pl.BlockSpec((tm, d), lambda i, g: (i, 0)),
                  pl.BlockSpec((tm, LANES), lambda i, g: (i, 0)),
                  pl.BlockSpec((tm, d), lambda i, g: (i, 0)),
                  pl.BlockSpec((epg, d, f), lambda i, g: (g, 0, 0)),
                  pl.BlockSpec((epg, d, f), lambda i, g: (g, 0, 0)),
                  pl.BlockSpec((1, gf, d), lambda i, g: (g, 0, 0))],
        out_specs=pl.BlockSpec((tm, d), lambda i, g: (i, 0)),
        out_shape=jax.ShapeDtypeStruct((t, d), F32),
        compiler_params=_cparams(2),
    )(h2, comb, x1, wg, wu, wd)


def _ple_kernel(x_ref, p_ref, wple_ref, wgate_ref, fg_ref, y_ref, *, precise, final):
    x = x_ref[...]
    emb = _mm(p_ref[...], wple_ref[...], precise)
    x = x + emb * _sigmoid(_mm(x, wgate_ref[...], precise))
    y_ref[...] = _rms_norm(x, fg_ref[...]) if final else x


def _ple(x2d, p2d, wple, wgate, fg, *, tm, precise, final):
    t, d = x2d.shape
    pd = p2d.shape[1]
    full = lambda a: pl.BlockSpec(a.shape, lambda i: (0,) * a.ndim)
    return pl.pallas_call(
        functools.partial(_ple_kernel, precise=precise, final=final),
        grid=(t // tm,),
        in_specs=[pl.BlockSpec((tm, d), lambda i: (i, 0)), pl.BlockSpec((tm, pd), lambda i: (i, 0)),
                  full(wple), full(wgate), full(fg)],
        out_specs=pl.BlockSpec((tm, d), lambda i: (i, 0)),
        out_shape=jax.ShapeDtypeStruct((t, d), F32),
        compiler_params=_cparams(1),
    )(x2d, p2d, wple, wgate, fg)


def _rwkv_sample_prep_kernel(z_ref, zp_ref, mu_ref, w0_ref, w2_ref, a0_ref, a2_ref, g2_ref,
                             kk_ref, ka_ref,
                             r_ref, d_ref, kkn_ref, kb_ref, kh_ref, v_ref, g_ref):
    wr = w0_ref.shape[1]
    hd_shift = HEAD_DIM.bit_length() - 1
    z = z_ref[...]
    zs = z + (zp_ref[...] - z) * mu_ref[...]
    r = zs[:, 0:wr]
    k = zs[:, wr:2 * wr]
    v = zs[:, 2 * wr:3 * wr]
    zl = zs[:, 3 * wr:]
    w_pre = w0_ref[...] + _mm(jnp.tanh(zl), w2_ref[...], False)
    decay = jnp.exp(-jnp.exp(-_softplus(-w_pre) - 0.5))
    eta = _sigmoid(a0_ref[...] + _mm(zl, a2_ref[...], False))
    ri = lax.broadcasted_iota(jnp.int32, (wr, wr), 0)
    ci = lax.broadcasted_iota(jnp.int32, (wr, wr), 1)
    head_ones = jnp.where((ri >> hd_shift) == (ci >> hd_shift), 1.0, 0.0)
    kk = k * kk_ref[...]
    kk = kk * lax.rsqrt(jnp.maximum(_mm(kk * kk, head_ones, True), 1e-24))
    r_ref[...] = r
    d_ref[...] = decay
    kkn_ref[...] = kk
    kb_ref[...] = kk * eta
    kh_ref[...] = k * (1.0 + (eta - 1.0) * ka_ref[...])
    v_ref[...] = v
    g_ref[...] = _mm(_sigmoid(zl), g2_ref[...], False)


def _rwkv_sample_prep(z, zp, mu, w0, w2p, a0, a2p, g2p, kk, ka):
    n = z.shape[0]
    wr = w0.shape[1]
    args = (z, zp, mu, w0, w2p, a0, a2p, g2p, kk, ka)
    full = lambda a: pl.BlockSpec(a.shape, lambda i: (0,) * a.ndim)
    return pl.pallas_call(
        _rwkv_sample_prep_kernel,
        grid=(1,),
        in_specs=[full(a) for a in args],
        out_specs=[pl.BlockSpec((n, wr), lambda i: (0, 0))] * 7,
        out_shape=[jax.ShapeDtypeStruct((n, wr), F32)] * 7,
        compiler_params=_cparams(1),
    )(*args)


def _rwkv_sample_step_kernel(s_ref, r_ref, d_ref, kkn_ref, kb_ref, kh_ref, v_ref, g_ref,
                             rk_ref, lg_ref, lb_ref, so_ref, rw_ref):
    n_heads = s_ref.shape[1]
    eye = (lax.broadcasted_iota(jnp.int32, (HEAD_DIM, HEAD_DIM), 0)
           == lax.broadcasted_iota(jnp.int32, (HEAD_DIM, HEAD_DIM), 1))
    for h in range(n_heads):
        hrow = lambda ref, lead=True: ref[0, h:h + 1, :] if lead else ref[h:h + 1, :]
        st = s_ref[0, h]
        r, v, kh = hrow(r_ref), hrow(v_ref), hrow(kh_ref)
        sa = jnp.sum(st * (-hrow(kkn_ref)), axis=1, keepdims=True)
        v_col = jnp.sum(jnp.where(eye, v, 0.0), axis=1, keepdims=True)
        st = st * hrow(d_ref) + sa * hrow(kb_ref) + v_col * kh
        so_ref[0, h] = st
        y_col = jnp.sum(st * r, axis=1, keepdims=True)
        y = jnp.sum(jnp.where(eye, y_col, 0.0), axis=0, keepdims=True)
        mean = jnp.mean(y, axis=1, keepdims=True)
        yc = y - mean
        var = jnp.mean(yc * yc, axis=1, keepdims=True)
        yn = yc * lax.rsqrt(var + GN_EPS) * hrow(lg_ref, False) + hrow(lb_ref, False)
        bonus = jnp.sum(r * kh * hrow(rk_ref, False), axis=1, keepdims=True) * v
        rw_ref[0, h:h + 1, :] = (yn + bonus) * hrow(g_ref)


def _rwkv_sample_step(state, per_head, rk, lg, lb):
    n, nh = state.shape[:2]
    tok = pl.BlockSpec((1, nh, HEAD_DIM), lambda i: (i, 0, 0))
    par = pl.BlockSpec((nh, HEAD_DIM), lambda i: (0, 0))
    st = pl.BlockSpec((1, nh, HEAD_DIM, HEAD_DIM), lambda i: (i, 0, 0, 0))
    return pl.pallas_call(
        _rwkv_sample_step_kernel,
        grid=(n,),
        in_specs=[st] + [tok] * 7 + [par] * 3,
        out_specs=[st, tok],
        out_shape=[jax.ShapeDtypeStruct(state.shape, F32),
                   jax.ShapeDtypeStruct((n, nh, HEAD_DIM), F32)],
        compiler_params=_cparams(1),
    )(state, *per_head, rk, lg, lb)


SELECT_SEQS_PER_STEP = 8
PAGES_PER_STEP = 32


def _key_pass_items(qc_ref, k_refs, s_ref, gate_ref, j):
    n_heads, _, page = k_refs[0].shape[1:]
    pages_per_blk = MOBA_BLOCK // page
    blks_per_step = len(k_refs) // pages_per_blk
    lane = lax.broadcasted_iota(jnp.int32, (n_heads, LANES), 1)

    def init():
        @pl.when(j == 0)
        def _():
            gate_ref[0] = jnp.zeros(gate_ref.shape[1:], F32)

    def block_item(blk_i):
        def run():
            qc = qc_ref[0]
            blk_sum = jnp.zeros((n_heads, page), F32)
            for half in range(pages_per_blk):
                i = blk_i * pages_per_blk + half
                s = jnp.sum(k_refs[i][0] * qc, axis=1)
                s_ref[0, :, i * page:(i + 1) * page] = s
                blk_sum = blk_sum + s
            g_col = jnp.sum(blk_sum, axis=1, keepdims=True) * (1.0 / MOBA_BLOCK)
            gate_ref[0] = jnp.where(lane == j * blks_per_step + blk_i, g_col, gate_ref[0])
        return run

    return [init] + [block_item(blk_i) for blk_i in range(blks_per_step)]


def _moba_sample_score_kernel(pt_ref, qc_ref, *refs):
    pps = PAGES_PER_STEP
    k_refs, s_ref, gate_ref = refs[:pps], refs[pps], refs[pps + 1]
    for item in _key_pass_items(qc_ref, k_refs, s_ref, gate_ref, pl.program_id(1)):
        item()


def _moba_sample_scores(page_table, q_cols, cache_t):
    n, n_pages = page_table.shape
    n_heads, hd, page = cache_t.shape[1:]
    pps = PAGES_PER_STEP
    k_spec = lambda i: pl.BlockSpec((1, n_heads, hd, page),
                                    lambda b, j, pt, i=i: (pt[b, j * pps + i], 0, 0, 0))
    grid_spec = pltpu.PrefetchScalarGridSpec(
        num_scalar_prefetch=1,
        grid=(n, n_pages // pps),
        in_specs=[pl.BlockSpec((1, n_heads, hd, page), lambda b, j, pt: (b, 0, 0, 0))]
                 + [k_spec(i) for i in range(pps)],
        out_specs=[pl.BlockSpec((1, n_heads, pps * page), lambda b, j, pt: (b, 0, j)),
                   pl.BlockSpec((1, n_heads, LANES), lambda b, j, pt: (b, 0, 0))])
    return pl.pallas_call(
        _moba_sample_score_kernel,
        grid_spec=grid_spec,
        out_shape=[jax.ShapeDtypeStruct((n, n_heads, n_pages * page), F32),
                   jax.ShapeDtypeStruct((n, n_heads, LANES), F32)],
        compiler_params=_cparams(2),
    )(page_table, q_cols, *([cache_t] * pps))


def _moba_sample_select_kernel(s_ref, gate_ref, bias_ref, q_ref, kn_ref, b0_ref,
                               p_ref, pown_ref, sel_ref, member_s):
    n_seq, n_heads, n_keys = s_ref.shape
    rows = n_seq * n_heads
    blk_shift = MOBA_BLOCK.bit_length() - 1
    n_blk = n_keys // MOBA_BLOCK
    scale = HEAD_DIM ** -0.5
    s = s_ref[...].reshape(rows, n_keys)

    @pl.when(pl.program_id(0) == 0)
    def _():
        member_s[...] = jnp.where(
            lax.broadcasted_iota(jnp.int32, (LANES, n_keys), 1) >> blk_shift
            == lax.broadcasted_iota(jnp.int32, (LANES, n_keys), 0), 1.0, 0.0).astype(BF16)

    lane = lax.broadcasted_iota(jnp.int32, (rows, LANES), 1)
    gate = jnp.where(lane < n_blk, gate_ref[...].reshape(rows, LANES), NEG)
    sel = jnp.zeros((rows, LANES), F32)
    for _ in range(MOBA_TOPK):
        _, pick = _first_argmax(gate, lane)
        hit = lane == pick
        sel = jnp.where(hit, 1.0, sel)
        gate = jnp.where(hit, NEG, gate)
    sel_ref[...] = sel.reshape(n_seq, n_heads, LANES)
    key_sel = _mm(sel, member_s[...], False)
    sc = jnp.where(key_sel > 0.5, s * scale + bias_ref[...], NEG)
    q = q_ref[...].reshape(rows, HEAD_DIM)
    k_new = kn_ref[...].reshape(rows, HEAD_DIM)
    s_own = jnp.sum(q * k_new, axis=1, keepdims=True) * scale + b0_ref[...]
    m = jnp.maximum(jnp.max(sc, axis=1, keepdims=True), s_own)
    p = jnp.exp(sc - m)
    p_own = jnp.exp(s_own - m)
    inv = 1.0 / (jnp.sum(p, axis=1, keepdims=True) + p_own)
    p_ref[...] = (p * inv).reshape(n_seq, n_heads, n_keys)
    pown_ref[...] = jnp.broadcast_to(p_own * inv, (rows, LANES)).reshape(n_seq, n_heads, LANES)


def _moba_sample_select(scores, gate, bias_keys, q3, kn3, bias0):
    n, n_heads, n_keys = scores.shape
    hd = q3.shape[2]
    ns = math.gcd(n, SELECT_SEQS_PER_STEP)
    tok = pl.BlockSpec((ns, n_heads, hd), lambda b: (b, 0, 0))
    keys = pl.BlockSpec((ns, n_heads, n_keys), lambda b: (b, 0, 0))
    lane_blk = pl.BlockSpec((ns, n_heads, LANES), lambda b: (b, 0, 0))
    bias_keys = jnp.tile(bias_keys, (ns, 1))
    bias0 = jnp.tile(bias0, (ns, 1))
    return pl.pallas_call(
        _moba_sample_select_kernel,
        grid=(n // ns,),
        in_specs=[keys, lane_blk, pl.BlockSpec((ns * n_heads, n_keys), lambda b: (0, 0)), tok, tok,
                  pl.BlockSpec((ns * n_heads, 1), lambda b: (0, 0))],
        out_specs=[keys, lane_blk, lane_blk],
        out_shape=[jax.ShapeDtypeStruct((n, n_heads, n_keys), F32),
                   jax.ShapeDtypeStruct((n, n_heads, LANES), F32),
                   jax.ShapeDtypeStruct((n, n_heads, LANES), F32)],
        scratch_shapes=[pltpu.VMEM((LANES, n_keys), BF16)],
        compiler_params=_cparams(1),
    )(scores, gate, bias_keys, q3, kn3, bias0)


def _pages_to_fetch(page_table, sel, pages_per_blk):
    n, n_pages = page_table.shape
    pps = PAGES_PER_STEP
    steps = n * n_pages // pps
    n_blk = n_pages // pages_per_blk
    blk_needed = jnp.max(sel[:, :, :n_blk], axis=1) > 0.0
    needed = jnp.repeat(blk_needed, pages_per_blk, axis=1)
    step_id = jnp.arange(steps, dtype=jnp.int32)[:, None]
    last_needed = lax.cummax(jnp.where(needed.reshape(steps, pps), step_id, 0), axis=0)
    pages = jnp.take_along_axis(page_table.reshape(steps, pps), last_needed, axis=0)
    return pages.reshape(n, n_pages), blk_needed.astype(jnp.int32)


def _moba_sample_pv_kernel(pt_ref, need_ref, p_ref, pown_ref, vn_ref, *refs):
    pps = PAGES_PER_STEP
    v_refs, o_ref, acc_s = refs[:pps], refs[pps], refs[pps + 1]
    n_heads, hd, page = v_refs[0].shape[1:]
    pages_per_blk = MOBA_BLOCK // page
    blks_per_step = pps // pages_per_blk
    b = pl.program_id(0)
    j = pl.program_id(1)

    @pl.when(j == 0)
    def _():
        acc_s[...] = jnp.zeros(acc_s.shape, F32)

    for blk_i in range(blks_per_step):
        @pl.when(need_ref[b, j * blks_per_step + blk_i] > 0)
        def _(blk_i=blk_i):
            acc = acc_s[...]
            for half in range(pages_per_blk):
                i = blk_i * pages_per_blk + half
                w = p_ref[0, :, i * page:(i + 1) * page][:, None, :]
                acc = acc + jnp.where(w > 0.0, v_refs[i][0] * w, 0.0)
            acc_s[...] = acc

    @pl.when(j == pl.num_programs(1) - 1)
    def _():
        eye = (lax.broadcasted_iota(jnp.int32, (hd, hd), 0)
               == lax.broadcasted_iota(jnp.int32, (hd, hd), 1))
        for h in range(n_heads):
            col = jnp.sum(acc_s[h], axis=1, keepdims=True)
            row = jnp.sum(jnp.where(eye, col, 0.0), axis=0, keepdims=True)
            o_ref[0, h:h + 1, :] = row + pown_ref[0, h:h + 1, 0:hd] * vn_ref[0, h:h + 1, :]


def _moba_sample_pv(pages, blk_needed, probs, p_own, vn3, cache_t):
    n, n_pages = pages.shape
    n_heads, hd, page = cache_t.shape[1:]
    pps = PAGES_PER_STEP
    v_spec = lambda i: pl.BlockSpec((1, n_heads, hd, page),
                                    lambda b, j, pt, need, i=i: (pt[b, j * pps + i], 0, 0, 0))
    tok = pl.BlockSpec((1, n_heads, hd), lambda b, j, pt, need: (b, 0, 0))
    grid_spec = pltpu.PrefetchScalarGridSpec(
        num_scalar_prefetch=2,
        grid=(n, n_pages // pps),
        in_specs=[pl.BlockSpec((1, n_heads, pps * page), lambda b, j, pt, need: (b, 0, j)),
                  pl.BlockSpec((1, n_heads, LANES), lambda b, j, pt, need: (b, 0, 0)), tok]
                 + [v_spec(i) for i in range(pps)],
        out_specs=tok,
        scratch_shapes=[pltpu.VMEM((n_heads, hd, page), F32)])
    return pl.pallas_call(
        _moba_sample_pv_kernel,
        grid_spec=grid_spec,
        out_shape=jax.ShapeDtypeStruct((n, n_heads, hd), F32),
        compiler_params=_cparams(2),
    )(pages, blk_needed, probs, p_own, vn3, *([cache_t] * pps))


def _sample_key_bias(tab, past_len):
    n_heads, n_dist = tab.shape
    assert past_len >= n_dist
    tab_x = jnp.concatenate([tab, tab[:, -1:]], axis=1)
    near = jnp.flip(tab_x[:, 1:n_dist + 1], axis=1)
    far = jnp.broadcast_to(tab[:, -1:], (n_heads, past_len - n_dist))
    return jnp.concatenate([far, near], axis=1)


def _moba_sample_key_pass_args(q, cache_k_l, page_table):
    n, width = q.shape
    n_pool, page, n_heads, hd = cache_k_l.shape
    n_pages = page_table.shape[1]
    past_len = n_pages * page
    assert hd == HEAD_DIM and n_heads * hd == width and page == LANES
    assert past_len % MOBA_BLOCK == 0 and past_len // MOBA_BLOCK >= MOBA_TOPK
    assert n_pages % PAGES_PER_STEP == 0 and PAGES_PER_STEP % (MOBA_BLOCK // page) == 0
    ck = jnp.transpose(cache_k_l, (0, 2, 3, 1))
    q_cols = jnp.broadcast_to(q.reshape(n, n_heads, hd)[..., None], (n, n_heads, hd, page))
    return page_table, q_cols, ck


def _moba_sample_finish(scores, gate, q, k_new, v_new, cache_v_l, page_table, tab):
    n, width = q.shape
    n_pool, page, n_heads, hd = cache_v_l.shape
    past_len = page_table.shape[1] * page
    heads = lambda a: a.reshape(n, n_heads, hd)
    q3, kn3, vn3 = heads(q), heads(k_new), heads(v_new)
    cv = jnp.transpose(cache_v_l, (0, 2, 3, 1))
    probs, p_own, sel = _moba_sample_select(scores, gate, _sample_key_bias(tab, past_len), q3, kn3,
                                            tab[:, 0:1])
    v_pages, blk_needed = _pages_to_fetch(page_table, sel, MOBA_BLOCK // page)
    return _moba_sample_pv(v_pages, blk_needed, probs, p_own, vn3, cv).reshape(n, width)


def _moba_sample(q, k_new, v_new, cache_k_l, cache_v_l, page_table, tab):
    scores, gate = _moba_sample_scores(*_moba_sample_key_pass_args(q, cache_k_l, page_table))
    return _moba_sample_finish(scores, gate, q, k_new, v_new, cache_v_l, page_table, tab)


def _layer_weights(l, precise, norm1_g, w_in, shift_mu, w0, w2, a0, a2, g2, k_k, k_a, r_k,
                   lnx_g, lnx_b, w_out, norm2_g, w_rg, b_rg, w_re, b_re, w_eg, w_eu, w_ed,
                   w_ple, w_pleg):
    wd = F32 if precise else BF16
    row = lambda x: x.reshape(1, -1).astype(F32)
    wa3 = 3 * (w_out.shape[1] - w0.shape[1])
    rw_cols = w_in.shape[2] - wa3
    z_pad = -rw_cols % LANES
    w2p, a2p, g2p = _lora_padded(w2[l], a2[l], g2[l])
    n_rt = N_GROUPS + N_GROUPS * EXPERTS_PER_GROUP
    w_router = jnp.pad(jnp.concatenate([w_rg[l], w_re[l]], axis=1), ((0, 0), (0, LANES - n_rt)))
    b_router = jnp.pad(jnp.concatenate([b_rg[l], b_re[l]]), (0, LANES - n_rt))
    return dict(
        norm1=row(norm1_g[l]),
        wqkv=w_in[l][:, :wa3].astype(wd),
        wz=jnp.pad(w_in[l][:, wa3:], ((0, 0), (0, z_pad))).astype(wd),
        mu=jnp.pad(shift_mu[l], (0, z_pad))[None],
        w0=row(w0[l]), w2p=w2p, a0=row(a0[l]), a2p=a2p, g2p=g2p,
        kk=row(k_k[l]), ka=row(k_a[l]), rk=row(r_k[l]), lg=row(lnx_g[l]), lb=row(lnx_b[l]),
        wo=w_out[l].astype(wd), norm2=row(norm2_g[l]), w_router=w_router, b_router=b_router[None],
        weg=w_eg[l].astype(wd), weu=w_eu[l].astype(wd),
        wed=w_ed[l].reshape(N_GROUPS, -1, w_ed.shape[3]).astype(wd),
        wple=w_ple[l].astype(wd), wgate=w_pleg[l].astype(wd),
        rw_cols=rw_cols, wa=wa3 // 3)


def _post_attention(x2d, att, rw, p2d, w, fg, *, precise, final):
    tm = min(x2d.shape[0], MOE_TOKEN_TILE)
    x1, h2, comb = _outproj_router(x2d, att, rw, w["wo"], w["norm2"], w["w_router"], w["b_router"],
                                   tm=tm, precise=precise)
    x2 = _moe(h2, comb, x1, w["weg"], w["weu"], w["wed"], tm=tm, precise=precise)
    return _ple(x2, p2d, w["wple"], w["wgate"], fg, tm=tm, precise=precise, final=final)


def kernel(x_prompt, x_sample, p_prompt, p_sample, cache_k, cache_v, page_table, state_wkv, state_shift, norm1_g, w_in, rel_bias, shift_mu, w0, w2, a0, a2, g2, k_k, k_a, r_k, lnx_g, lnx_b, w_out, norm2_g, w_rg, b_rg, w_re, b_re, w_eg, w_eu, w_ed, w_ple, w_pleg, final_g):
    depth = norm1_g.shape[0]
    b, s, d = x_prompt.shape
    n = x_sample.shape[0]
    assert x_sample.shape[1] == 1
    n_heads = state_wkv.shape[2]
    layer_params = (norm1_g, w_in, shift_mu, w0, w2, a0, a2, g2, k_k, k_a, r_k, lnx_g, lnx_b,
                    w_out, norm2_g, w_rg, b_rg, w_re, b_re, w_eg, w_eu, w_ed, w_ple, w_pleg)
    fg = final_g.reshape(1, -1)
    tab = _rel_bias_table(rel_bias, 2 * MOBA_BLOCK)
    bias_tiles = _bias_tiles(tab * LOG2_E)
    xp = x_prompt.reshape(b * s, d)
    xs = x_sample.reshape(n, d)
    outs = [[] for _ in range(8)]
    for l in range(depth):
        final = l == depth - 1
        w = _layer_weights(l, False, *layer_params)
        wa, rw_cols = w["wa"], w["rw_cols"]
        heads = lambda a, lead: a.reshape(lead + (wa // HEAD_DIM, HEAD_DIM))
        qk, k_t, v_t, z = _inproj_seq(xp, w["norm1"], w["wqkv"], w["wz"], b=b, tm=512)
        qkv_s, z_s = _inproj(xs, w["norm1"], w["wqkv"], w["wz"], tm=n, precise=False)
        q_s, k_s, v_s = qkv_s[:, :wa], qkv_s[:, wa:2 * wa], qkv_s[:, 2 * wa:]
        att = _moba_prompt(qk.reshape(b, s, 2 * wa), v_t, bias_tiles).reshape(b * s, wa)
        seq_major = lambda a: jnp.transpose(a.reshape(b, wa // HEAD_DIM, HEAD_DIM, s), (0, 3, 1, 2))
        z3 = z.reshape(b, s, -1)
        key_pass = _moba_sample_key_pass_args(q_s, cache_k[l], page_table)
        rwkv_args = (z3, w["mu"], w["w0"], w["w2p"], w["a0"], w["a2p"], w["g2p"], w["kk"], w["ka"],
                     w["rk"], w["lg"], w["lb"])
        rwkv_steps = b * (w["w0"].shape[1] // MXU_DIM) * (s // RWKV_TOKEN_TILE)
        if rwkv_steps == n * (page_table.shape[1] // PAGES_PER_STEP):
            rw, st, scores, gate = _rwkv_prompt(*rwkv_args, tc=RWKV_TOKEN_TILE, key_pass=key_pass)
        else:
            rw, st = _rwkv_prompt(*rwkv_args, tc=RWKV_TOKEN_TILE)
            scores, gate = _moba_sample_scores(*key_pass)
        xp = _post_attention(xp, att, rw.reshape(b * s, -1), p_prompt[l].reshape(b * s, -1), w, fg,
                             precise=False, final=final)
        outs[0].append(seq_major(k_t))
        outs[1].append(seq_major(v_t))
        outs[4].append(_state_from_tiles(st))
        outs[6].append(z3[:, -1, :rw_cols])
        att = _moba_sample_finish(scores, gate, q_s, k_s, v_s, cache_v[l], page_table, tab)
        z_prev = jnp.pad(state_shift[l], ((0, 0), (0, z_s.shape[1] - rw_cols)))
        per_tok = _rwkv_sample_prep(z_s, z_prev, w["mu"], w["w0"], w["w2p"], w["a0"], w["a2p"], w["g2p"],
                                    w["kk"], w["ka"])
        per_head = [a.reshape(n, n_heads, HEAD_DIM) for a in per_tok]
        hp = lambda a: a.reshape(n_heads, HEAD_DIM)
        st_s, rw_s = _rwkv_sample_step(state_wkv[l], per_head, hp(w["rk"]), hp(w["lg"]), hp(w["lb"]))
        xs = _post_attention(xs, att, rw_s.reshape(n, -1), p_sample[l].reshape(n, -1), w, fg,
                             precise=False, final=final)
        outs[2].append(heads(k_s, (n, 1)))
        outs[3].append(heads(v_s, (n, 1)))
        outs[5].append(st_s)
        outs[7].append(z_s[:, :rw_cols])
    kp, vp, ks, vs, wp, ws, sp, ss = [jnp.stack(o) for o in outs]
    return (xp.reshape(b, s, d), xs.reshape(n, 1, d), kp, vp, ks, vs, wp, ws, sp, ss)
```

```python
import functools
import math

import jax
import jax.numpy as jnp
from jax import lax
from jax.experimental import pallas as pl
from jax.experimental.pallas import tpu as pltpu

F32 = jnp.float32
BF16 = jnp.bfloat16
HIGHEST = lax.Precision.HIGHEST

HEAD_DIM = 64
MOBA_BLOCK = 256
MOBA_TOPK = 3
N_BUCKETS = 32
MAX_DISTANCE = 128
LORA_W = 64
LORA_A = 64
LORA_G = 160
N_GROUPS = 4
EXPERTS_PER_GROUP = 4
RMS_EPS = 1e-6
GN_EPS = 64e-5
NEG = -1e30
LOG2_E = math.log2(math.e)

LANES = 128
MXU_DIM = 256
VMEM_LIMIT_BYTES = 56 * 1024 * 1024

MOE_TOKEN_TILE = 1024
MOBA_HEAD_GROUP = 8
RWKV_CHUNK = 64
RWKV_TOKEN_TILE = 512
HEADS_PER_TILE = MXU_DIM // HEAD_DIM


def _cparams(n_axes):
    return pltpu.CompilerParams(dimension_semantics=("arbitrary",) * n_axes,
                                vmem_limit_bytes=VMEM_LIMIT_BYTES)


def _mm(a, b, precise):
    if precise:
        return jnp.dot(a.astype(F32), b.astype(F32), precision=HIGHEST, preferred_element_type=F32)
    return jnp.dot(a.astype(BF16), b.astype(BF16), preferred_element_type=F32)


def _mm_nt(a, b):
    return lax.dot_general(a.astype(BF16), b.astype(BF16), (((1,), (1,)), ((), ())),
                           preferred_element_type=F32)


def _mm_tn(a, b):
    return lax.dot_general(a.astype(BF16), b.astype(BF16), (((0,), (0,)), ((), ())),
                           preferred_element_type=F32)


def _mm_split(a_exact, x, terms):
    a16 = a_exact.astype(BF16)
    acc = None
    rem = x
    for _ in range(terms):
        piece = rem.astype(BF16)
        part = jnp.dot(a16, piece, preferred_element_type=F32)
        acc = part if acc is None else acc + part
        rem = rem - piece.astype(F32)
    return acc


def _mm_split_rhs(x, b_exact, terms):
    b16 = b_exact.astype(BF16)
    acc = None
    rem = x
    for _ in range(terms):
        piece = rem.astype(BF16)
        part = jnp.dot(piece, b16, preferred_element_type=F32)
        acc = part if acc is None else acc + part
        rem = rem - piece.astype(F32)
    return acc


def _sigmoid(x):
    return 1.0 / (1.0 + jnp.exp(-x))


def _softplus(x):
    return jnp.maximum(x, 0.0) + jnp.log(1.0 + jnp.exp(-jnp.abs(x)))


def _rms_norm(x, g):
    return x * lax.rsqrt(jnp.mean(x * x, axis=-1, keepdims=True) + RMS_EPS) * g


def _inproj_kernel(x_ref, g_ref, wqkv_ref, wz_ref, qkv_ref, z_ref, *, precise):
    h = _rms_norm(x_ref[...], g_ref[...])
    qkv_ref[...] = _mm(h, wqkv_ref[...], precise)
    z_ref[...] = _mm(h, wz_ref[...], precise)


def _inproj(x2d, g, wqkv, wz, *, tm, precise):
    t, d = x2d.shape
    nq, nz = wqkv.shape[1], wz.shape[1]
    return pl.pallas_call(
        functools.partial(_inproj_kernel, precise=precise),
        grid=(t // tm,),
        in_specs=[pl.BlockSpec((tm, d), lambda i: (i, 0)),
                  pl.BlockSpec((1, d), lambda i: (0, 0)),
                  pl.BlockSpec((d, nq), lambda i: (0, 0)),
                  pl.BlockSpec((d, nz), lambda i: (0, 0))],
        out_specs=[pl.BlockSpec((tm, nq), lambda i: (i, 0)),
                   pl.BlockSpec((tm, nz), lambda i: (i, 0))],
        out_shape=[jax.ShapeDtypeStruct((t, nq), F32), jax.ShapeDtypeStruct((t, nz), F32)],
        compiler_params=_cparams(1),
    )(x2d, g, wqkv, wz)


def _inproj_seq_kernel(x_ref, g_ref, wqkv_ref, wz_ref, qk_ref, kt_ref, vt_ref, z_ref):
    wa = kt_ref.shape[1]
    h = _rms_norm(x_ref[...], g_ref[...])
    qkv = _mm(h, wqkv_ref[...], False)
    qk_ref[...] = qkv[:, :2 * wa]
    kt_ref[0] = qkv[:, wa:2 * wa].T
    vt_ref[0] = qkv[:, 2 * wa:].T
    z_ref[...] = _mm(h, wz_ref[...], False)


def _inproj_seq(x2d, g, wqkv, wz, *, b, tm):
    t, d = x2d.shape
    s = t // b
    nq, nz = wqkv.shape[1], wz.shape[1]
    wa = nq // 3
    per_seq = s // tm
    feat_major = pl.BlockSpec((1, wa, tm), lambda i: (i // per_seq, 0, i % per_seq))
    return pl.pallas_call(
        _inproj_seq_kernel,
        grid=(t // tm,),
        in_specs=[pl.BlockSpec((tm, d), lambda i: (i, 0)),
                  pl.BlockSpec((1, d), lambda i: (0, 0)),
                  pl.BlockSpec((d, nq), lambda i: (0, 0)),
                  pl.BlockSpec((d, nz), lambda i: (0, 0))],
        out_specs=[pl.BlockSpec((tm, 2 * wa), lambda i: (i, 0)), feat_major, feat_major,
                   pl.BlockSpec((tm, nz), lambda i: (i, 0))],
        out_shape=[jax.ShapeDtypeStruct((t, 2 * wa), F32), jax.ShapeDtypeStruct((b, wa, s), F32),
                   jax.ShapeDtypeStruct((b, wa, s), F32), jax.ShapeDtypeStruct((t, nz), F32)],
        compiler_params=_cparams(1),
    )(x2d, g, wqkv, wz)


def _moba_prompt_kernel(q_ref, k_ref, vt_ref, tt_ref, o_ref,
                        kbf_s, vt_s, kmean_s, m_s, l_s, acc_s, sel_s, qbf_s):
    blk = MOBA_BLOCK
    i = pl.program_id(1)
    nb = kbf_s.shape[0]
    n_heads = tt_ref.shape[0]
    scale = HEAD_DIM ** -0.5 * LOG2_E

    @pl.when(i == 0)
    def _():
        for n in range(nb):
            kb = k_ref[0, n * blk:(n + 1) * blk, :]
            kbf_s[n] = kb.astype(BF16)
            kmean_s[n:n + 1, :] = jnp.mean(kb, axis=0, keepdims=True)
            vt_s[n] = vt_ref[0, :, n * blk:(n + 1) * blk].astype(BF16)

    q_t = q_ref[0].T
    blk_row = lax.broadcasted_iota(jnp.int32, (nb, blk), 0)
    key_idx = lax.broadcasted_iota(jnp.int32, (blk, blk), 0)
    qry_idx = lax.broadcasted_iota(jnp.int32, (blk, blk), 1)
    causal = key_idx <= qry_idx
    upper_half = lax.broadcasted_iota(jnp.int32, (LANES, blk), 0) >= HEAD_DIM

    for h in range(n_heads):
        lanes = slice((h // 2) * LANES, (h // 2 + 1) * LANES)
        q_h = jnp.where(upper_half == (h % 2 == 1), q_t[lanes, :], 0.0)
        gate = _mm(kmean_s[:, lanes], q_h, False)
        gate = jnp.where(blk_row < i, gate, NEG)
        rank = jnp.zeros((nb, blk), jnp.int32)
        for m in range(nb):
            gm = gate[m:m + 1, :]
            beats = (gm > gate) | ((gm == gate) & (m < blk_row))
            rank = rank + beats.astype(jnp.int32)
        sel_s[h] = jnp.where((rank < MOBA_TOPK) & (blk_row < i), 1.0, 0.0)
        qbf_s[h] = (q_h * scale).astype(BF16)
    m_s[...] = jnp.full(m_s.shape, NEG, F32)
    l_s[...] = jnp.zeros(l_s.shape, F32)
    acc_s[...] = jnp.zeros(acc_s.shape, F32)

    def attend(group, n, bias_of, mask_of, uniform_bias=False):
        scores = [jnp.dot(kbf_s[n, :, (h // 2) * LANES:(h // 2 + 1) * LANES], qbf_s[h],
                          preferred_element_type=F32) for h in group]
        probs, alphas = [], []
        for h, s in zip(group, scores):
            m_old = m_s[h:h + 1, :]
            if uniform_bias:
                bias = bias_of(h)
                s = jnp.where(mask_of(h), s, NEG)
                m_new = jnp.maximum(m_old, jnp.max(s, axis=0, keepdims=True) + bias)
                p = jnp.exp2(s - (m_new - bias))
            else:
                s = jnp.where(mask_of(h), s + bias_of(h), NEG)
                m_new = jnp.maximum(m_old, jnp.max(s, axis=0, keepdims=True))
                p = jnp.exp2(s - m_new)
            alpha = jnp.exp2(m_old - m_new)
            l_s[h:h + 1, :] = alpha * l_s[h:h + 1, :] + jnp.sum(p, axis=0, keepdims=True)
            m_s[h:h + 1, :] = m_new
            probs.append(p.astype(BF16))
            alphas.append(alpha)
        pvs = [jnp.dot(vt_s[n, h * HEAD_DIM:(h + 1) * HEAD_DIM, :], p, preferred_element_type=F32)
               for h, p in zip(group, probs)]
        for h, alpha, pv in zip(group, alphas, pvs):
            acc_s[h] = alpha * acc_s[h] + pv

    for g0 in range(0, n_heads, MOBA_HEAD_GROUP):
        group = range(g0, g0 + MOBA_HEAD_GROUP)
        attend(group, i, lambda h: tt_ref[h, :, blk:2 * blk], lambda h: causal)

        @pl.when(i >= 1)
        def _(group=group):
            n = i - 1
            attend(group, n, lambda h: tt_ref[h, :, 0:blk], lambda h: sel_s[h, pl.ds(n, 1), :] > 0.5)

        def far_body(n, carry, group=group):
            attend(group, n, lambda h: tt_ref[h, 0:1, blk - 1:blk],
                   lambda h: sel_s[h, pl.ds(n, 1), :] > 0.5, uniform_bias=True)
            return carry

        lax.fori_loop(0, jnp.maximum(i - 1, 0), far_body, 0)

    for pair in range(n_heads // 2):
        outs = [acc_s[h] / l_s[h:h + 1, :] for h in (2 * pair, 2 * pair + 1)]
        o_ref[0, :, pair * LANES:(pair + 1) * LANES] = jnp.concatenate(outs, axis=0).T


def _moba_prompt(qk3, vt3, tt):
    b, s, w2 = qk3.shape
    w = w2 // 2
    blk = MOBA_BLOCK
    nb = s // blk
    n_heads = w // HEAD_DIM
    return pl.pallas_call(
        _moba_prompt_kernel,
        grid=(b, nb),
        in_specs=[pl.BlockSpec((1, blk, w), lambda bi, i: (bi, i, 0)),
                  pl.BlockSpec((1, s, w), lambda bi, i: (bi, 0, 1)),
                  pl.BlockSpec((1, w, s), lambda bi, i: (bi, 0, 0)),
                  pl.BlockSpec((n_heads, blk, 2 * blk), lambda bi, i: (0, 0, 0))],
        out_specs=pl.BlockSpec((1, blk, w), lambda bi, i: (bi, i, 0)),
        out_shape=jax.ShapeDtypeStruct((b, s, w), F32),
        scratch_shapes=[pltpu.VMEM((nb, blk, w), BF16),
                        pltpu.VMEM((nb, w, blk), BF16),
                        pltpu.VMEM((nb, w), F32),
                        pltpu.VMEM((n_heads, blk), F32),
                        pltpu.VMEM((n_heads, blk), F32),
                        pltpu.VMEM((n_heads, HEAD_DIM, blk), F32),
                        pltpu.VMEM((n_heads, nb, blk), F32),
                        pltpu.VMEM((n_heads, LANES, blk), BF16)],
        compiler_params=_cparams(2),
    )(qk3, qk3, vt3, tt)


def _rel_bias_table(rel_bias, n_dist):
    n = jnp.arange(n_dist, dtype=jnp.int32)
    max_exact = N_BUCKETS // 2
    nf = jnp.maximum(n, 1).astype(F32)
    large = max_exact + (jnp.log(nf / max_exact) / math.log(MAX_DISTANCE / max_exact)
                         * (N_BUCKETS - max_exact)).astype(jnp.int32)
    large = jnp.minimum(large, N_BUCKETS - 1)
    bucket = jnp.where(n < max_exact, n, large)
    return rel_bias.astype(F32).T[:, bucket]


def _bias_tiles(tab):
    blk = MOBA_BLOCK
    period = 2 * blk
    n_heads = tab.shape[0]
    base = jnp.concatenate([tab[:, blk:period], tab[:, 0:blk]], axis=1)
    stride = 2 * period - 1
    reps = -(-(blk * stride) // period)
    seq = jnp.tile(base, (1, reps))[:, :blk * stride]
    return seq.reshape(n_heads, blk, stride)[:, :, :period]


def _mm3(a, b):
    a_hi = a.astype(BF16)
    b_hi = b.astype(BF16)
    a_lo = (a - a_hi.astype(F32)).astype(BF16)
    b_lo = (b - b_hi.astype(F32)).astype(BF16)
    return (jnp.dot(a_hi, b_hi, preferred_element_type=F32)
            + jnp.dot(a_hi, b_lo, preferred_element_type=F32)
            + jnp.dot(a_lo, b_hi, preferred_element_type=F32))


def _rwkv_prompt_kernel(zr_ref, zk_ref, zv_ref, zl_ref,
                        mur_ref, muk_ref, muv_ref, mul_ref,
                        w0_ref, w2_ref, a0_ref, a2_ref, g2_ref,
                        kk_ref, ka_ref, rk_ref, lg_ref, lb_ref,
                        rw_ref, st_ref,
                        p_s, last_r, last_k, last_v, last_l, tick=lambda: None):
    t = pl.program_id(2)
    tc = zr_ref.shape[1]
    w = MXU_DIM
    c_len = RWKV_CHUNK
    hd_shift = HEAD_DIM.bit_length() - 1

    @pl.when(t == 0)
    def _():
        p_s[...] = jnp.zeros(p_s.shape, F32)
        last_r[...] = jnp.zeros(last_r.shape, F32)
        last_k[...] = jnp.zeros(last_k.shape, F32)
        last_v[...] = jnp.zeros(last_v.shape, F32)
        last_l[...] = jnp.zeros(last_l.shape, F32)

    def token_shift_mix(z_ref, last_ref, mu_ref):
        z = z_ref[0]
        row = lax.broadcasted_iota(jnp.int32, z.shape, 0)
        prev = jnp.where(row == 0, last_ref[...], pltpu.roll(z, 1, 0))
        last_ref[...] = z[tc - 1:tc, :]
        return z + (prev - z) * mu_ref[...]

    ri = lax.broadcasted_iota(jnp.int32, (w, w), 0)
    ci = lax.broadcasted_iota(jnp.int32, (w, w), 1)
    same_head = (ri >> hd_shift) == (ci >> hd_shift)
    tok_r = ri & (HEAD_DIM - 1)
    tok_c = ci & (HEAD_DIM - 1)
    strict = same_head & (tok_r > tok_c)
    incl = same_head & (tok_r >= tok_c)
    eye = ri == ci
    head_ones = jnp.where(same_head, 1.0, 0.0).astype(BF16)
    tri = jnp.where(lax.broadcasted_iota(jnp.int32, (c_len, c_len), 0)
                    >= lax.broadcasted_iota(jnp.int32, (c_len, c_len), 1), 1.0, 0.0).astype(BF16)

    def head_sum(x):
        return _mm_split_rhs(x, head_ones, 2)

    def stack(x):
        return jnp.where(same_head, jnp.concatenate([x] * HEADS_PER_TILE, axis=0), 0.0)

    def unstack(x):
        out = x[0:c_len]
        for e in range(1, HEADS_PER_TILE):
            out = out + x[e * c_len:(e + 1) * c_len]
        return out

    r = token_shift_mix(zr_ref, last_r, mur_ref)
    k = token_shift_mix(zk_ref, last_k, muk_ref)
    v = token_shift_mix(zv_ref, last_v, muv_ref)
    zl = token_shift_mix(zl_ref, last_l, mul_ref)

    n_wa = LORA_W + LORA_A
    zl_wa = zl[:, :n_wa]
    w_pre = w0_ref[...] + _mm(jnp.tanh(zl_wa), w2_ref[0:n_wa, :], False)
    logd = -jnp.exp(-_softplus(-w_pre) - 0.5)
    eta = _sigmoid(a0_ref[...] + _mm(zl_wa, a2_ref[0:n_wa, :], False))
    gate = _mm(_sigmoid(zl[:, n_wa:]), g2_ref[n_wa:, :], False)
    kk = k * kk_ref[...]
    kk = kk * lax.rsqrt(jnp.maximum(head_sum(kk * kk), 1e-24))
    kh = k * (1.0 + (eta - 1.0) * ka_ref[...])
    bonus = head_sum(r * kh * rk_ref[...]) * v
    kb = kk * eta
    tick()

    chunks = range(tc // c_len)
    each = lambda f, *lists: [f(*(x[c] for x in lists)) for c in chunks]
    rows = [slice(c * c_len, (c + 1) * c_len) for c in chunks]
    ld = [logd[rw] for rw in rows]
    cum = each(lambda x: _mm_split(tri, x, 3), ld)
    cum_end = each(lambda x: x[c_len - 1:c_len, :], cum)
    e_inv = each(lambda x: jnp.exp(-x), cum)
    e_rem = each(lambda x, xe: jnp.exp(xe - x), cum, cum_end)
    tick()
    a_t =[stack(-kk[rows[c]] * jnp.exp(cum[c] - ld[c])) for c in chunks]
    b_t = [stack(kb[rows[c]] * e_inv[c]) for c in chunks]
    k_t = [stack(kh[rows[c]] * e_inv[c]) for c in chunks]
    r_t = [stack(r[rows[c]] * jnp.exp(cum[c])) for c in chunks]
    b_h = [stack(kb[rows[c]] * e_rem[c]) for c in chunks]
    k_h = [stack(kh[rows[c]] * e_rem[c]) for c in chunks]
    v_s = [stack(v[rows[c]]) for c in chunks]
    tick()

    prod =each(lambda a, rr, bb, kt: _mm_nt(jnp.concatenate([a, rr], axis=0),
                                             jnp.concatenate([bb, kt], axis=0)), a_t, r_t, b_t, k_t)
    n_mat = each(lambda x: jnp.where(strict, x[0:w, 0:w], 0.0), prod)
    a_k = each(lambda x: jnp.where(strict, x[0:w, w:2 * w], 0.0), prod)
    m_rb = each(lambda x: jnp.where(incl, x[w:2 * w, 0:w], 0.0), prod)
    m_rk = each(lambda x: jnp.where(incl, x[w:2 * w, w:2 * w], 0.0), prod)
    tick()

    t_inv = each(lambda x: jnp.where(eye, 1.0, 0.0) + x, n_mat)
    n_pow = n_mat
    for _ in range(c_len.bit_length() - 2):
        n_pow = each(lambda x: _mm(x, x, False), n_pow)
        tick()
        t_inv = each(lambda x, y: x + _mm(x, y, False), t_inv, n_pow)
        tick()

    a_p = each(lambda x, y: _mm(x, y, False), t_inv, a_t)
    tick()
    a_v = each(lambda x, y: _mm(x, y, False), a_k, v_s)
    tick()
    v_p = each(lambda x, y: _mm(x, y, False), t_inv, a_v)
    tick()
    r_p = each(lambda rr, m, ap: unstack(rr + _mm(m, ap, False)), r_t, m_rb, a_p)
    tick()
    y_0 = each(lambda mb, vp, mk, vs: unstack(_mm(mb, vp, False) + _mm(mk, vs, False)),
               m_rb, v_p, m_rk, v_s)
    tick()
    decay = each(jnp.exp, cum_end)
    mix = each(_mm_tn, a_p, b_h)
    tick()
    add = each(lambda vp, bh, vs, kh_: _mm_tn(vp, bh) + _mm_tn(vs, kh_), v_p, b_h, v_s, k_h)
    tick()

    st = p_s[...]
    ys = []
    for c in chunks:
        ys.append(_mm_nt(r_p[c], st) + y_0[c])
        st = st * decay[c] + _mm(st, mix[c], False) + add[c]
    p_s[...] = st
    st_ref[0, 0] = st

    y = jnp.concatenate(ys, axis=0)
    mean = head_sum(y) * (1.0 / HEAD_DIM)
    yc = y - mean
    var = head_sum(yc * yc) * (1.0 / HEAD_DIM)
    yn = yc * lax.rsqrt(var + GN_EPS) * lg_ref[...] + lb_ref[...]
    rw_ref[0] = (yn + bonus) * gate


N_RWKV_INPUTS = 18


def _rwkv_scores_kernel(pt_ref, *refs, steps_per_seq):
    pps = PAGES_PER_STEP
    n_in = N_RWKV_INPUTS
    rwkv_in, qc_ref, k_refs = refs[:n_in], refs[n_in], refs[n_in + 1:n_in + 1 + pps]
    rw_ref, st_ref, s_ref, gate_ref = refs[n_in + 1 + pps:n_in + 5 + pps]
    scratch = refs[n_in + 5 + pps:]
    step = ((pl.program_id(0) * pl.num_programs(1) + pl.program_id(1)) * pl.num_programs(2)
            + pl.program_id(2))
    work = iter(_key_pass_items(qc_ref, k_refs, s_ref, gate_ref, step % steps_per_seq))

    def tick():
        item = next(work, None)
        if item is not None:
            item()

    _rwkv_prompt_kernel(*rwkv_in, rw_ref, st_ref, *scratch, tick=tick)
    for item in work:
        item()


def _rwkv_prompt(z3, mu, w0, w2p, a0, a2p, g2p, kk, ka, rk, lg, lb, *, tc, key_pass=None):
    b, s, _ = z3.shape
    w = MXU_DIM
    wr = w0.shape[1]
    nt = wr // w
    n_t = s // tc
    lw = w2p.shape[0]
    lora_blk = (3 * wr) // lw
    vec = lambda off: pl.BlockSpec((1, w), lambda bi, q, t, *_, off=off: (0, off + q))
    zblk = lambda off: pl.BlockSpec((1, tc, w), lambda bi, q, t, *_, off=off: (bi, t, off + q))
    lora_w = pl.BlockSpec((lw, w), lambda bi, q, t, *_: (0, q))
    in_specs = [zblk(0), zblk(nt), zblk(2 * nt),
                pl.BlockSpec((1, tc, lw), lambda bi, q, t, *_: (bi, t, lora_blk)),
                vec(0), vec(nt), vec(2 * nt),
                pl.BlockSpec((1, lw), lambda bi, q, t, *_: (0, lora_blk)),
                vec(0), lora_w, vec(0), lora_w, lora_w,
                vec(0), vec(0), vec(0), vec(0), vec(0)]
    out_specs = [pl.BlockSpec((1, tc, w), lambda bi, q, t, *_: (bi, t, q)),
                 pl.BlockSpec((1, 1, w, w), lambda bi, q, t, *_: (bi, q, 0, 0))]
    out_shape = [jax.ShapeDtypeStruct((b, s, wr), F32), jax.ShapeDtypeStruct((b, nt, w, w), F32)]
    scratch_shapes = [pltpu.VMEM((w, w), F32), pltpu.VMEM((1, w), F32), pltpu.VMEM((1, w), F32),
                      pltpu.VMEM((1, w), F32), pltpu.VMEM((1, lw), F32)]
    args = (z3, z3, z3, z3, mu, mu, mu, mu, w0, w2p, a0, a2p, g2p, kk, ka, rk, lg, lb)
    assert len(args) == N_RWKV_INPUTS
    if key_pass is None:
        return pl.pallas_call(
            _rwkv_prompt_kernel, grid=(b, nt, n_t), in_specs=in_specs, out_specs=out_specs,
            out_shape=out_shape, scratch_shapes=scratch_shapes, compiler_params=_cparams(3),
        )(*args)

    page_table, q_cols, cache_t = key_pass
    n, n_pages = page_table.shape
    n_heads, hd, page = cache_t.shape[1:]
    pps = PAGES_PER_STEP
    steps_per_seq = n_pages // pps
    assert b * nt * n_t == n * steps_per_seq
    step = lambda bi, q, t: (bi * nt + q) * n_t + t
    seq = lambda bi, q, t: step(bi, q, t) // steps_per_seq
    part = lambda bi, q, t: step(bi, q, t) % steps_per_seq
    k_spec = lambda i: pl.BlockSpec(
        (1, n_heads, hd, page),
        lambda bi, q, t, pt, i=i: (pt[seq(bi, q, t), part(bi, q, t) * pps + i], 0, 0, 0))
    grid_spec = pltpu.PrefetchScalarGridSpec(
        num_scalar_prefetch=1,
        grid=(b, nt, n_t),
        in_specs=in_specs
                 + [pl.BlockSpec((1, n_heads, hd, page), lambda bi, q, t, pt: (seq(bi, q, t), 0, 0, 0))]
                 + [k_spec(i) for i in range(pps)],
        out_specs=out_specs
                  + [pl.BlockSpec((1, n_heads, pps * page),
                                  lambda bi, q, t, pt: (seq(bi, q, t), 0, part(bi, q, t))),
                     pl.BlockSpec((1, n_heads, LANES), lambda bi, q, t, pt: (seq(bi, q, t), 0, 0))],
        scratch_shapes=scratch_shapes)
    return pl.pallas_call(
        functools.partial(_rwkv_scores_kernel, steps_per_seq=steps_per_seq),
        grid_spec=grid_spec,
        out_shape=out_shape + [jax.ShapeDtypeStruct((n, n_heads, n_pages * page), F32),
                               jax.ShapeDtypeStruct((n, n_heads, LANES), F32)],
        compiler_params=_cparams(3),
    )(page_table, *args, q_cols, *([cache_t] * pps))


def _state_from_tiles(st):
    b, nt = st.shape[:2]
    heads = []
    for q in range(nt):
        for e in range(HEADS_PER_TILE):
            sl = slice(e * HEAD_DIM, (e + 1) * HEAD_DIM)
            heads.append(st[:, q, sl, sl])
    return jnp.stack(heads, axis=1)


def _lora_padded(w2, a2, g2):
    lw = 3 * LANES
    n = w2.shape[1]
    o_a = LORA_W
    o_g = LORA_W + LORA_A
    w2p = jnp.zeros((lw, n), F32).at[0:o_a].set(w2)
    a2p = jnp.zeros((lw, n), F32).at[o_a:o_g].set(a2)
    g2p = jnp.zeros((lw, n), F32).at[o_g:o_g + LORA_G].set(g2)
    return w2p, a2p, g2p


ROUTER_LANE0 = N_GROUPS


def _first_argmax(x, lane):
    mx = jnp.max(x, axis=1, keepdims=True)
    idx = jnp.min(jnp.where(x == mx, lane, LANES), axis=1, keepdims=True)
    return mx, idx


def _outproj_router_kernel(x_ref, att_ref, rw_ref, wo_ref, g_ref, wr_ref, br_ref,
                           x1_ref, h2_ref, comb_ref, *, precise):
    wa = att_ref.shape[1]
    x1 = (x_ref[...] + _mm(att_ref[...], wo_ref[0:wa, :], precise)
          + _mm(rw_ref[...], wo_ref[wa:, :], precise))
    x1_ref[...] = x1
    h2 = _rms_norm(x1, g_ref[...])
    h2_ref[...] = h2.astype(h2_ref.dtype)
    logits = _mm(h2, wr_ref[...], precise) + br_ref[...]
    n_exp = N_GROUPS * EXPERTS_PER_GROUP
    lane = lax.broadcasted_iota(jnp.int32, logits.shape, 1)
    is_grp = lane < N_GROUPS
    gl = jnp.where(is_grp, logits, NEG)
    g_max, grp = _first_argmax(gl, lane)
    p_grp = 1.0 / jnp.sum(jnp.where(is_grp, jnp.exp(gl - g_max), 0.0), axis=1, keepdims=True)
    e_lane = lane - ROUTER_LANE0
    grp_shift = EXPERTS_PER_GROUP.bit_length() - 1
    in_grp = (e_lane >= 0) & (e_lane < n_exp) & ((e_lane >> grp_shift) == grp)
    el = jnp.where(in_grp, logits, NEG)
    m1, i1 = _first_argmax(el, lane)
    el2 = jnp.where(lane == i1, NEG, el)
    m2, i2 = _first_argmax(el2, lane)
    ratio = jnp.exp(m2 - m1)
    p1 = 1.0 / (1.0 + ratio)
    p2 = ratio / (1.0 + ratio)
    comb_ref[...] = p_grp * (jnp.where(lane == i1, p1, 0.0) + jnp.where(lane == i2, p2, 0.0))


def _outproj_router(x2d, att, rw, wo, g, wr, br, *, tm, precise):
    t, d = x2d.shape
    wa, wr_cols = att.shape[1], rw.shape[1]
    h_dtype = F32 if precise else BF16
    tok = lambda n: pl.BlockSpec((tm, n), lambda i: (i, 0))
    full = lambda a: pl.BlockSpec(a.shape, lambda i: (0,) * a.ndim)
    return pl.pallas_call(
        functools.partial(_outproj_router_kernel, precise=precise),
        grid=(t // tm,),
        in_specs=[tok(d), tok(wa), tok(wr_cols), full(wo), full(g), full(wr), full(br)],
        out_specs=[tok(d), tok(d), tok(LANES)],
        out_shape=[jax.ShapeDtypeStruct((t, d), F32), jax.ShapeDtypeStruct((t, d), h_dtype),
                   jax.ShapeDtypeStruct((t, LANES), F32)],
        compiler_params=_cparams(1),
    )(x2d, att, rw, wo, g, wr, br)


def _moe_ple_kernel(h2_ref, comb_ref, x1_ref, wg_ref, wu_ref, wd_ref, p_ref, wple_ref, wgate_ref,
                    fg_ref, y_ref, *, final):
    _moe_kernel(h2_ref, comb_ref, x1_ref, wg_ref, wu_ref, wd_ref, y_ref, precise=False)

    @pl.when(pl.program_id(1) == pl.num_programs(1) - 1)
    def _():
        _ple_kernel(y_ref, p_ref, wple_ref, wgate_ref, fg_ref, y_ref, precise=False, final=final)


def _moe_ple(h2, comb, x1, wg, wu, wd, p2d, wple, wgate, fg, *, tm, final):
    t, d = x1.shape
    n_grp, gf, _ = wd.shape
    epg = EXPERTS_PER_GROUP
    f = gf // epg
    pd = p2d.shape[1]
    tok = lambda n: pl.BlockSpec((tm, n), lambda i, g: (i, 0))
    full = lambda a: pl.BlockSpec(a.shape, lambda i, g: (0,) * a.ndim)
    return pl.pallas_call(
        functools.partial(_moe_ple_kernel, final=final),
        grid=(t // tm, n_grp),
        in_specs=[tok(d), tok(LANES), tok(d),
                  pl.BlockSpec((epg, d, f), lambda i, g: (g, 0, 0)),
                  pl.BlockSpec((epg, d, f), lambda i, g: (g, 0, 0)),
                  pl.BlockSpec((1, gf, d), lambda i, g: (g, 0, 0)),
                  tok(pd), full(wple), full(wgate), full(fg)],
        out_specs=tok(d),
        out_shape=jax.ShapeDtypeStruct((t, d), F32),
        compiler_params=_cparams(2),
    )(h2, comb, x1, wg, wu, wd, p2d, wple, wgate, fg)


def _moe_kernel(h2_ref, comb_ref, x1_ref, wg_ref, wu_ref, wd_ref, x2_ref, *, precise):
    g = pl.program_id(1)

    @pl.when(g == 0)
    def _():
        x2_ref[...] = x1_ref[...]

    h2 = h2_ref[...]
    comb = comb_ref[...]
    lane = lax.broadcasted_iota(jnp.int32, comb.shape, 1)
    gates = [_mm(h2, wg_ref[j], precise) for j in range(EXPERTS_PER_GROUP)]
    ups = [_mm(h2, wu_ref[j], precise) for j in range(EXPERTS_PER_GROUP)]
    acts = []
    for j in range(EXPERTS_PER_GROUP):
        e_lane = ROUTER_LANE0 + g * EXPERTS_PER_GROUP + j
        weight = jnp.sum(jnp.where(lane == e_lane, comb, 0.0), axis=1, keepdims=True)
        acts.append(gates[j] * _sigmoid(gates[j]) * ups[j] * weight)
    x2_ref[...] += _mm(jnp.concatenate(acts, axis=1), wd_ref[0], precise)


def _moe(h2, comb, x1, wg, wu, wd, *, tm, precise):
    t, d = x1.shape
    n_grp, gf, _ = wd.shape
    epg = EXPERTS_PER_GROUP
    f = gf // epg
    return pl.pallas_call(
        functools.partial(_moe_kernel, precise=precise),
        grid=(t // tm, n_grp),
        in_specs=[pl.BlockSpec((tm, d), lambda i, g: (i, 0)),
                  pl.BlockSpec((tm, LANES), lambda i, g: (i, 0)),
                  pl.BlockSpec((tm, d), lambda i, g: (i, 0)),
                  pl.BlockSpec((epg, d, f), lambda i, g: (g, 0, 0)),
                  pl.BlockSpec((epg, d, f), lambda i, g: (g, 0, 0)),
                  pl.BlockSpec((1, gf, d), lambda i, g: (g, 0, 0))],
        out_specs=pl.BlockSpec((tm, d), lambda i, g: (i, 0)),
        out_shape=jax.ShapeDtypeStruct((t, d), F32),
        compiler_params=_cparams(2),
    )(h2, comb, x1, wg, wu, wd)


def _ple_kernel(x_ref, p_ref, wple_ref, wgate_ref, fg_ref, y_ref, *, precise, final):
    x = x_ref[...]
    emb = _mm(p_ref[...], wple_ref[...], precise)
    x = x + emb * _sigmoid(_mm(x, wgate_ref[...], precise))
    y_ref[...] = _rms_norm(x, fg_ref[...]) if final else x


def _ple(x2d, p2d, wple, wgate, fg, *, tm, precise, final):
    t, d = x2d.shape
    pd = p2d.shape[1]
    full = lambda a: pl.BlockSpec(a.shape, lambda i: (0,) * a.ndim)
    return pl.pallas_call(
        functools.partial(_ple_kernel, precise=precise, final=final),
        grid=(t // tm,),
        in_specs=[pl.BlockSpec((tm, d), lambda i: (i, 0)), pl.BlockSpec((tm, pd), lambda i: (i, 0)),
                  full(wple), full(wgate), full(fg)],
        out_specs=pl.BlockSpec((tm, d), lambda i: (i, 0)),
        out_shape=jax.ShapeDtypeStruct((t, d), F32),
        compiler_params=_cparams(1),
    )(x2d, p2d, wple, wgate, fg)


def _rwkv_sample_prep_kernel(z_ref, zp_ref, mu_ref, w0_ref, w2_ref, a0_ref, a2_ref, g2_ref,
                             kk_ref, ka_ref,
                             r_ref, d_ref, kkn_ref, kb_ref, kh_ref, v_ref, g_ref):
    wr = w0_ref.shape[1]
    hd_shift = HEAD_DIM.bit_length() - 1
    z = z_ref[...]
    zs = z + (zp_ref[...] - z) * mu_ref[...]
    r = zs[:, 0:wr]
    k = zs[:, wr:2 * wr]
    v = zs[:, 2 * wr:3 * wr]
    zl = zs[:, 3 * wr:]
    w_pre = w0_ref[...] + _mm(jnp.tanh(zl), w2_ref[...], False)
    decay = jnp.exp(-jnp.exp(-_softplus(-w_pre) - 0.5))
    eta = _sigmoid(a0_ref[...] + _mm(zl, a2_ref[...], False))
    ri = lax.broadcasted_iota(jnp.int32, (wr, wr), 0)
    ci = lax.broadcasted_iota(jnp.int32, (wr, wr), 1)
    head_ones = jnp.where((ri >> hd_shift) == (ci >> hd_shift), 1.0, 0.0)
    kk = k * kk_ref[...]
    kk = kk * lax.rsqrt(jnp.maximum(_mm(kk * kk, head_ones, True), 1e-24))
    r_ref[...] = r
    d_ref[...] = decay
    kkn_ref[...] = kk
    kb_ref[...] = kk * eta
    kh_ref[...] = k * (1.0 + (eta - 1.0) * ka_ref[...])
    v_ref[...] = v
    g_ref[...] = _mm(_sigmoid(zl), g2_ref[...], False)


def _rwkv_sample_prep(z, zp, mu, w0, w2p, a0, a2p, g2p, kk, ka):
    n = z.shape[0]
    wr = w0.shape[1]
    args = (z, zp, mu, w0, w2p, a0, a2p, g2p, kk, ka)
    full = lambda a: pl.BlockSpec(a.shape, lambda i: (0,) * a.ndim)
    return pl.pallas_call(
        _rwkv_sample_prep_kernel,
        grid=(1,),
        in_specs=[full(a) for a in args],
        out_specs=[pl.BlockSpec((n, wr), lambda i: (0, 0))] * 7,
        out_shape=[jax.ShapeDtypeStruct((n, wr), F32)] * 7,
        compiler_params=_cparams(1),
    )(*args)


def _rwkv_sample_step_kernel(s_ref, r_ref, d_ref, kkn_ref, kb_ref, kh_ref, v_ref, g_ref,
                             rk_ref, lg_ref, lb_ref, so_ref, rw_ref):
    n_heads = s_ref.shape[1]
    eye = (lax.broadcasted_iota(jnp.int32, (HEAD_DIM, HEAD_DIM), 0)
           == lax.broadcasted_iota(jnp.int32, (HEAD_DIM, HEAD_DIM), 1))
    for h in range(n_heads):
        hrow = lambda ref, lead=True: ref[0, h:h + 1, :] if lead else ref[h:h + 1, :]
        st = s_ref[0, h]
        r, v, kh = hrow(r_ref), hrow(v_ref), hrow(kh_ref)
        sa = jnp.sum(st * (-hrow(kkn_ref)), axis=1, keepdims=True)
        v_col = jnp.sum(jnp.where(eye, v, 0.0), axis=1, keepdims=True)
        st = st * hrow(d_ref) + sa * hrow(kb_ref) + v_col * kh
        so_ref[0, h] = st
        y_col = jnp.sum(st * r, axis=1, keepdims=True)
        y = jnp.sum(jnp.where(eye, y_col, 0.0), axis=0, keepdims=True)
        mean = jnp.mean(y, axis=1, keepdims=True)
        yc = y - mean
        var = jnp.mean(yc * yc, axis=1, keepdims=True)
        yn = yc * lax.rsqrt(var + GN_EPS) * hrow(lg_ref, False) + hrow(lb_ref, False)
        bonus = jnp.sum(r * kh * hrow(rk_ref, False), axis=1, keepdims=True) * v
        rw_ref[0, h:h + 1, :] = (yn + bonus) * hrow(g_ref)


def _rwkv_sample_step(state, per_head, rk, lg, lb):
    n, nh = state.shape[:2]
    tok = pl.BlockSpec((1, nh, HEAD_DIM), lambda i: (i, 0, 0))
    par = pl.BlockSpec((nh, HEAD_DIM), lambda i: (0, 0))
    st = pl.BlockSpec((1, nh, HEAD_DIM, HEAD_DIM), lambda i: (i, 0, 0, 0))
    return pl.pallas_call(
        _rwkv_sample_step_kernel,
        grid=(n,),
        in_specs=[st] + [tok] * 7 + [par] * 3,
        out_specs=[st, tok],
        out_shape=[jax.ShapeDtypeStruct(state.shape, F32),
                   jax.ShapeDtypeStruct((n, nh, HEAD_DIM), F32)],
        compiler_params=_cparams(1),
    )(state, *per_head, rk, lg, lb)


SELECT_SEQS_PER_STEP = 8
PAGES_PER_STEP = 32


def _key_pass_items(qc_ref, k_refs, s_ref, gate_ref, j):
    n_heads, _, page = k_refs[0].shape[1:]
    pages_per_blk = MOBA_BLOCK // page
    blks_per_step = len(k_refs) // pages_per_blk
    lane = lax.broadcasted_iota(jnp.int32, (n_heads, LANES), 1)

    def init():
        @pl.when(j == 0)
        def _():
            gate_ref[0] = jnp.zeros(gate_ref.shape[1:], F32)

    def block_item(blk_i):
        def run():
            qc = qc_ref[0]
            blk_sum = jnp.zeros((n_heads, page), F32)
            for half in range(pages_per_blk):
                i = blk_i * pages_per_blk + half
                s = jnp.sum(k_refs[i][0] * qc, axis=1)
                s_ref[0, :, i * page:(i + 1) * page] = s
                blk_sum = blk_sum + s
            g_col = jnp.sum(blk_sum, axis=1, keepdims=True) * (1.0 / MOBA_BLOCK)
            gate_ref[0] = jnp.where(lane == j * blks_per_step + blk_i, g_col, gate_ref[0])
        return run

    return [init] + [block_item(blk_i) for blk_i in range(blks_per_step)]


def _moba_sample_score_kernel(pt_ref, qc_ref, *refs):
    pps = PAGES_PER_STEP
    k_refs, s_ref, gate_ref = refs[:pps], refs[pps], refs[pps + 1]
    for item in _key_pass_items(qc_ref, k_refs, s_ref, gate_ref, pl.program_id(1)):
        item()


def _moba_sample_scores(page_table, q_cols, cache_t):
    n, n_pages = page_table.shape
    n_heads, hd, page = cache_t.shape[1:]
    pps = PAGES_PER_STEP
    k_spec = lambda i: pl.BlockSpec((1, n_heads, hd, page),
                                    lambda b, j, pt, i=i: (pt[b, j * pps + i], 0, 0, 0))
    grid_spec = pltpu.PrefetchScalarGridSpec(
        num_scalar_prefetch=1,
        grid=(n, n_pages // pps),
        in_specs=[pl.BlockSpec((1, n_heads, hd, page), lambda b, j, pt: (b, 0, 0, 0))]
                 + [k_spec(i) for i in range(pps)],
        out_specs=[pl.BlockSpec((1, n_heads, pps * page), lambda b, j, pt: (b, 0, j)),
                   pl.BlockSpec((1, n_heads, LANES), lambda b, j, pt: (b, 0, 0))])
    return pl.pallas_call(
        _moba_sample_score_kernel,
        grid_spec=grid_spec,
        out_shape=[jax.ShapeDtypeStruct((n, n_heads, n_pages * page), F32),
                   jax.ShapeDtypeStruct((n, n_heads, LANES), F32)],
        compiler_params=_cparams(2),
    )(page_table, q_cols, *([cache_t] * pps))


def _moba_sample_select_kernel(s_ref, gate_ref, bias_ref, q_ref, kn_ref, b0_ref,
                               p_ref, pown_ref, sel_ref, member_s):
    n_seq, n_heads, n_keys = s_ref.shape
    rows = n_seq * n_heads
    blk_shift = MOBA_BLOCK.bit_length() - 1
    n_blk = n_keys // MOBA_BLOCK
    scale = HEAD_DIM ** -0.5
    s = s_ref[...].reshape(rows, n_keys)

    @pl.when(pl.program_id(0) == 0)
    def _():
        member_s[...] = jnp.where(
            lax.broadcasted_iota(jnp.int32, (LANES, n_keys), 1) >> blk_shift
            == lax.broadcasted_iota(jnp.int32, (LANES, n_keys), 0), 1.0, 0.0).astype(BF16)

    lane = lax.broadcasted_iota(jnp.int32, (rows, LANES), 1)
    gate = jnp.where(lane < n_blk, gate_ref[...].reshape(rows, LANES), NEG)
    sel = jnp.zeros((rows, LANES), F32)
    for _ in range(MOBA_TOPK):
        _, pick = _first_argmax(gate, lane)
        hit = lane == pick
        sel = jnp.where(hit, 1.0, sel)
        gate = jnp.where(hit, NEG, gate)
    sel_ref[...] = sel.reshape(n_seq, n_heads, LANES)
    key_sel = _mm(sel, member_s[...], False)
    sc = jnp.where(key_sel > 0.5, s * scale + bias_ref[...], NEG)
    q = q_ref[...].reshape(rows, HEAD_DIM)
    k_new = kn_ref[...].reshape(rows, HEAD_DIM)
    s_own = jnp.sum(q * k_new, axis=1, keepdims=True) * scale + b0_ref[...]
    m = jnp.maximum(jnp.max(sc, axis=1, keepdims=True), s_own)
    p = jnp.exp(sc - m)
    p_own = jnp.exp(s_own - m)
    inv = 1.0 / (jnp.sum(p, axis=1, keepdims=True) + p_own)
    p_ref[...] = (p * inv).reshape(n_seq, n_heads, n_keys)
    pown_ref[...] = jnp.broadcast_to(p_own * inv, (rows, LANES)).reshape(n_seq, n_heads, LANES)


def _moba_sample_select(scores, gate, bias_keys, q3, kn3, bias0):
    n, n_heads, n_keys = scores.shape
    hd = q3.shape[2]
    ns = math.gcd(n, SELECT_SEQS_PER_STEP)
    tok = pl.BlockSpec((ns, n_heads, hd), lambda b: (b, 0, 0))
    keys = pl.BlockSpec((ns, n_heads, n_keys), lambda b: (b, 0, 0))
    lane_blk = pl.BlockSpec((ns, n_heads, LANES), lambda b: (b, 0, 0))
    bias_keys = jnp.tile(bias_keys, (ns, 1))
    bias0 = jnp.tile(bias0, (ns, 1))
    return pl.pallas_call(
        _moba_sample_select_kernel,
        grid=(n // ns,),
        in_specs=[keys, lane_blk, pl.BlockSpec((ns * n_heads, n_keys), lambda b: (0, 0)), tok, tok,
                  pl.BlockSpec((ns * n_heads, 1), lambda b: (0, 0))],
        out_specs=[keys, lane_blk, lane_blk],
        out_shape=[jax.ShapeDtypeStruct((n, n_heads, n_keys), F32),
                   jax.ShapeDtypeStruct((n, n_heads, LANES), F32),
                   jax.ShapeDtypeStruct((n, n_heads, LANES), F32)],
        scratch_shapes=[pltpu.VMEM((LANES, n_keys), BF16)],
        compiler_params=_cparams(1),
    )(scores, gate, bias_keys, q3, kn3, bias0)


def _pages_to_fetch(page_table, sel, pages_per_blk):
    n, n_pages = page_table.shape
    pps = PAGES_PER_STEP
    steps = n * n_pages // pps
    n_blk = n_pages // pages_per_blk
    blk_needed = jnp.max(sel[:, :, :n_blk], axis=1) > 0.0
    needed = jnp.repeat(blk_needed, pages_per_blk, axis=1)
    step_id = jnp.arange(steps, dtype=jnp.int32)[:, None]
    last_needed = lax.cummax(jnp.where(needed.reshape(steps, pps), step_id, 0), axis=0)
    pages = jnp.take_along_axis(page_table.reshape(steps, pps), last_needed, axis=0)
    return pages.reshape(n, n_pages), blk_needed.astype(jnp.int32)


def _moba_sample_pv_kernel(pt_ref, need_ref, p_ref, pown_ref, vn_ref, *refs):
    pps = PAGES_PER_STEP
    v_refs, o_ref, acc_s = refs[:pps], refs[pps], refs[pps + 1]
    n_heads, hd, page = v_refs[0].shape[1:]
    pages_per_blk = MOBA_BLOCK // page
    blks_per_step = pps // pages_per_blk
    b = pl.program_id(0)
    j = pl.program_id(1)

    @pl.when(j == 0)
    def _():
        acc_s[...] = jnp.zeros(acc_s.shape, F32)

    for blk_i in range(blks_per_step):
        @pl.when(need_ref[b, j * blks_per_step + blk_i] > 0)
        def _(blk_i=blk_i):
            acc = acc_s[...]
            for half in range(pages_per_blk):
                i = blk_i * pages_per_blk + half
                w = p_ref[0, :, i * page:(i + 1) * page][:, None, :]
                acc = acc + jnp.where(w > 0.0, v_refs[i][0] * w, 0.0)
            acc_s[...] = acc

    @pl.when(j == pl.num_programs(1) - 1)
    def _():
        eye = (lax.broadcasted_iota(jnp.int32, (hd, hd), 0)
               == lax.broadcasted_iota(jnp.int32, (hd, hd), 1))
        for h in range(n_heads):
            col = jnp.sum(acc_s[h], axis=1, keepdims=True)
            row = jnp.sum(jnp.where(eye, col, 0.0), axis=0, keepdims=True)
            o_ref[0, h:h + 1, :] = row + pown_ref[0, h:h + 1, 0:hd] * vn_ref[0, h:h + 1, :]


def _moba_sample_pv(pages, blk_needed, probs, p_own, vn3, cache_t):
    n, n_pages = pages.shape
    n_heads, hd, page = cache_t.shape[1:]
    pps = PAGES_PER_STEP
    v_spec = lambda i: pl.BlockSpec((1, n_heads, hd, page),
                                    lambda b, j, pt, need, i=i: (pt[b, j * pps + i], 0, 0, 0))
    tok = pl.BlockSpec((1, n_heads, hd), lambda b, j, pt, need: (b, 0, 0))
    grid_spec = pltpu.PrefetchScalarGridSpec(
        num_scalar_prefetch=2,
        grid=(n, n_pages // pps),
        in_specs=[pl.BlockSpec((1, n_heads, pps * page), lambda b, j, pt, need: (b, 0, j)),
                  pl.BlockSpec((1, n_heads, LANES), lambda b, j, pt, need: (b, 0, 0)), tok]
                 + [v_spec(i) for i in range(pps)],
        out_specs=tok,
        scratch_shapes=[pltpu.VMEM((n_heads, hd, page), F32)])
    return pl.pallas_call(
        _moba_sample_pv_kernel,
        grid_spec=grid_spec,
        out_shape=jax.ShapeDtypeStruct((n, n_heads, hd), F32),
        compiler_params=_cparams(2),
    )(pages, blk_needed, probs, p_own, vn3, *([cache_t] * pps))


def _sample_key_bias(tab, past_len):
    n_heads, n_dist = tab.shape
    assert past_len >= n_dist
    tab_x = jnp.concatenate([tab, tab[:, -1:]], axis=1)
    near = jnp.flip(tab_x[:, 1:n_dist + 1], axis=1)
    far = jnp.broadcast_to(tab[:, -1:], (n_heads, past_len - n_dist))
    return jnp.concatenate([far, near], axis=1)


def _moba_sample_key_pass_args(q, cache_k_l, page_table):
    n, width = q.shape
    n_pool, page, n_heads, hd = cache_k_l.shape
    n_pages = page_table.shape[1]
    past_len = n_pages * page
    assert hd == HEAD_DIM and n_heads * hd == width and page == LANES
    assert past_len % MOBA_BLOCK == 0 and past_len // MOBA_BLOCK >= MOBA_TOPK
    assert n_pages % PAGES_PER_STEP == 0 and PAGES_PER_STEP % (MOBA_BLOCK // page) == 0
    ck = jnp.transpose(cache_k_l, (0, 2, 3, 1))
    q_cols = jnp.broadcast_to(q.reshape(n, n_heads, hd)[..., None], (n, n_heads, hd, page))
    return page_table, q_cols, ck


def _moba_sample_finish(scores, gate, q, k_new, v_new, cache_v_l, page_table, tab):
    n, width = q.shape
    n_pool, page, n_heads, hd = cache_v_l.shape
    past_len = page_table.shape[1] * page
    heads = lambda a: a.reshape(n, n_heads, hd)
    q3, kn3, vn3 = heads(q), heads(k_new), heads(v_new)
    cv = jnp.transpose(cache_v_l, (0, 2, 3, 1))
    probs, p_own, sel = _moba_sample_select(scores, gate, _sample_key_bias(tab, past_len), q3, kn3,
                                            tab[:, 0:1])
    v_pages, blk_needed = _pages_to_fetch(page_table, sel, MOBA_BLOCK // page)
    return _moba_sample_pv(v_pages, blk_needed, probs, p_own, vn3, cv).reshape(n, width)


def _moba_sample(q, k_new, v_new, cache_k_l, cache_v_l, page_table, tab):
    scores, gate = _moba_sample_scores(*_moba_sample_key_pass_args(q, cache_k_l, page_table))
    return _moba_sample_finish(scores, gate, q, k_new, v_new, cache_v_l, page_table, tab)


def _layer_weights(l, precise, norm1_g, w_in, shift_mu, w0, w2, a0, a2, g2, k_k, k_a, r_k,
                   lnx_g, lnx_b, w_out, norm2_g, w_rg, b_rg, w_re, b_re, w_eg, w_eu, w_ed,
                   w_ple, w_pleg):
    wd = F32 if precise else BF16
    row = lambda x: x.reshape(1, -1).astype(F32)
    wa3 = 3 * (w_out.shape[1] - w0.shape[1])
    rw_cols = w_in.shape[2] - wa3
    z_pad = -rw_cols % LANES
    w2p, a2p, g2p = _lora_padded(w2[l], a2[l], g2[l])
    n_rt = N_GROUPS + N_GROUPS * EXPERTS_PER_GROUP
    w_router = jnp.pad(jnp.concatenate([w_rg[l], w_re[l]], axis=1), ((0, 0), (0, LANES - n_rt)))
    b_router = jnp.pad(jnp.concatenate([b_rg[l], b_re[l]]), (0, LANES - n_rt))
    return dict(
        norm1=row(norm1_g[l]),
        wqkv=w_in[l][:, :wa3].astype(wd),
        wz=jnp.pad(w_in[l][:, wa3:], ((0, 0), (0, z_pad))).astype(wd),
        mu=jnp.pad(shift_mu[l], (0, z_pad))[None],
        w0=row(w0[l]), w2p=w2p, a0=row(a0[l]), a2p=a2p, g2p=g2p,
        kk=row(k_k[l]), ka=row(k_a[l]), rk=row(r_k[l]), lg=row(lnx_g[l]), lb=row(lnx_b[l]),
        wo=w_out[l].astype(wd), norm2=row(norm2_g[l]), w_router=w_router, b_router=b_router[None],
        weg=w_eg[l].astype(wd), weu=w_eu[l].astype(wd),
        wed=w_ed[l].reshape(N_GROUPS, -1, w_ed.shape[3]).astype(wd),
        wple=w_ple[l].astype(wd), wgate=w_pleg[l].astype(wd),
        rw_cols=rw_cols, wa=wa3 // 3)


def _post_attention(x2d, att, rw, p2d, w, fg, *, precise, final):
    tm = min(x2d.shape[0], MOE_TOKEN_TILE)
    x1, h2, comb = _outproj_router(x2d, att, rw, w["wo"], w["norm2"], w["w_router"], w["b_router"],
                                   tm=tm, precise=precise)
    return _moe_ple(h2, comb, x1, w["weg"], w["weu"], w["wed"], p2d, w["wple"], w["wgate"], fg,
                    tm=tm, final=final)


def kernel(x_prompt, x_sample, p_prompt, p_sample, cache_k, cache_v, page_table, state_wkv, state_shift, norm1_g, w_in, rel_bias, shift_mu, w0, w2, a0, a2, g2, k_k, k_a, r_k, lnx_g, lnx_b, w_out, norm2_g, w_rg, b_rg, w_re, b_re, w_eg, w_eu, w_ed, w_ple, w_pleg, final_g):
    depth = norm1_g.shape[0]
    b, s, d = x_prompt.shape
    n = x_sample.shape[0]
    assert x_sample.shape[1] == 1
    n_heads = state_wkv.shape[2]
    layer_params = (norm1_g, w_in, shift_mu, w0, w2, a0, a2, g2, k_k, k_a, r_k, lnx_g, lnx_b,
                    w_out, norm2_g, w_rg, b_rg, w_re, b_re, w_eg, w_eu, w_ed, w_ple, w_pleg)
    fg = final_g.reshape(1, -1)
    tab = _rel_bias_table(rel_bias, 2 * MOBA_BLOCK)
    bias_tiles = _bias_tiles(tab * LOG2_E)
    xp = x_prompt.reshape(b * s, d)
    xs = x_sample.reshape(n, d)
    outs = [[] for _ in range(8)]
    for l in range(depth):
        final = l == depth - 1
        w = _layer_weights(l, False, *layer_params)
        wa, rw_cols = w["wa"], w["rw_cols"]
        heads = lambda a, lead: a.reshape(lead + (wa // HEAD_DIM, HEAD_DIM))
        qk, k_t, v_t, z = _inproj_seq(xp, w["norm1"], w["wqkv"], w["wz"], b=b, tm=512)
        qkv_s, z_s = _inproj(xs, w["norm1"], w["wqkv"], w["wz"], tm=n, precise=False)
        q_s, k_s, v_s = qkv_s[:, :wa], qkv_s[:, wa:2 * wa], qkv_s[:, 2 * wa:]
        att = _moba_prompt(qk.reshape(b, s, 2 * wa), v_t, bias_tiles).reshape(b * s, wa)
        seq_major = lambda a: jnp.transpose(a.reshape(b, wa // HEAD_DIM, HEAD_DIM, s), (0, 3, 1, 2))
        z3 = z.reshape(b, s, -1)
        key_pass = _moba_sample_key_pass_args(q_s, cache_k[l], page_table)
        rwkv_args = (z3, w["mu"], w["w0"], w["w2p"], w["a0"], w["a2p"], w["g2p"], w["kk"], w["ka"],
                     w["rk"], w["lg"], w["lb"])
        rwkv_steps = b * (w["w0"].shape[1] // MXU_DIM) * (s // RWKV_TOKEN_TILE)
        if rwkv_steps == n * (page_table.shape[1] // PAGES_PER_STEP):
            rw, st, scores, gate = _rwkv_prompt(*rwkv_args, tc=RWKV_TOKEN_TILE, key_pass=key_pass)
        else:
            rw, st = _rwkv_prompt(*rwkv_args, tc=RWKV_TOKEN_TILE)
            scores, gate = _moba_sample_scores(*key_pass)
        xp = _post_attention(xp, att, rw.reshape(b * s, -1), p_prompt[l].reshape(b * s, -1), w, fg,
                             precise=False, final=final)
        outs[0].append(seq_major(k_t))
        outs[1].append(seq_major(v_t))
        outs[4].append(_state_from_tiles(st))
        outs[6].append(z3[:, -1, :rw_cols])
        att = _moba_sample_finish(scores, gate, q_s, k_s, v_s, cache_v[l], page_table, tab)
        z_prev = jnp.pad(state_shift[l], ((0, 0), (0, z_s.shape[1] - rw_cols)))
        per_tok = _rwkv_sample_prep(z_s, z_prev, w["mu"], w["w0"], w["w2p"], w["a0"], w["a2p"], w["g2p"],
                                    w["kk"], w["ka"])
        per_head = [a.reshape(n, n_heads, HEAD_DIM) for a in per_tok]
        hp = lambda a: a.reshape(n_heads, HEAD_DIM)
        st_s, rw_s = _rwkv_sample_step(state_wkv[l], per_head, hp(w["rk"]), hp(w["lg"]), hp(w["lb"]))
        xs = _post_attention(xs, att, rw_s.reshape(n, -1), p_sample[l].reshape(n, -1), w, fg,
                             precise=False, final=final)
        outs[2].append(heads(k_s, (n, 1)))
        outs[3].append(heads(v_s, (n, 1)))
        outs[5].append(st_s)
        outs[7].append(z_s[:, :rw_cols])
    kp, vp, ks, vs, wp, ws, sp, ss = [jnp.stack(o) for o in outs]
    return (xp.reshape(b, s, d), xs.reshape(n, 1, d), kp, vp, ks, vs, wp, ws, sp, ss)
```
